```python
import math
import jax
import jax.numpy as jnp
from jax import lax
import numpy as np

D_MODEL = 1024
BATCH = 2
SEQ = 8192
DEPTH = 4
DEC_BATCH = 32
DEC_SEQ = 4
PAST_LEN = 8192
PAGE_SIZE = 128

N_MIXERS = 3
N_LAYERS_A = (DEPTH + 2) // 3
N_LAYERS_B = (DEPTH + 1) // 3
N_LAYERS_C = DEPTH // 3

N_HEADS = 16
HEAD_DIM = D_MODEL // N_HEADS
N_KV_HEADS = 4
GROUP = N_HEADS // N_KV_HEADS
KV_WIDTH = N_KV_HEADS * HEAD_DIM
Q_BLOCK = 128
ATTN_SCALE = HEAD_DIM ** -0.5

IDX_HEADS = 8
IDX_DIM = 64
IDX_SCALE = (IDX_HEADS * IDX_DIM) ** -0.5
DSA_TOPK = 256
A_IN = D_MODEL + 2 * KV_WIDTH + IDX_HEADS * IDX_DIM + IDX_DIM + IDX_HEADS

CMP_LEN = 32
CMP_HIDDEN = 2 * HEAD_DIM
SEL_BLOCK = 64
N_SEL = 16
WINDOW = 512
B_IN = D_MODEL + 6 * KV_WIDTH + 3 * N_HEADS

MLSTM_INNER = 2 * D_MODEL
MLSTM_HEADS = 4
MLSTM_HEAD_DIM = MLSTM_INNER // MLSTM_HEADS
CONV_WIDTH = 4
MLSTM_CHUNK = 64

PEER_HEADS = 8
PEER_KEYS = 128
N_EXPERTS = PEER_KEYS * PEER_KEYS
PEER_KEY_DIM = 256
PEER_TOPK = 16
PEER_BLOCK = 128

ALPHA = (2 * DEPTH) ** 0.25
BETA = (8 * DEPTH) ** -0.25
LN_EPS = 1e-5

kernel_name = "hybrid_dsa_nsa_mlstm_peer_step"


def layer_norm(x, g, b):
    xf = x.astype(jnp.float32)
    mu = jnp.mean(xf, -1, keepdims=True)
    var = jnp.mean(jnp.square(xf - mu), -1, keepdims=True)
    return ((xf - mu) * lax.rsqrt(var + LN_EPS) * g + b).astype(x.dtype)


def masked_softmax(logits, mask):
    z = jnp.where(mask, logits.astype(jnp.float32), -jnp.inf)
    zmax = jnp.max(z, axis=-1, keepdims=True)
    zmax = jnp.where(jnp.isfinite(zmax), zmax, 0.0)
    e = jnp.where(mask, jnp.exp(z - zmax), 0.0)
    s = jnp.sum(e, axis=-1, keepdims=True)
    return e / jnp.maximum(s, 1e-30)


def to_blocks(a, blk):
    B, S = a.shape[:2]
    return jnp.moveaxis(a.reshape((B, S // blk, blk) + a.shape[2:]), 1, 0)


def from_blocks(a):
    nb, B, blk = a.shape[:3]
    return jnp.moveaxis(a, 0, 1).reshape((B, nb * blk) + a.shape[3:])


def gather_pages(pool, layer, page_table):
    g = pool[layer, page_table]
    return g.reshape((g.shape[0], g.shape[1] * g.shape[2]) + g.shape[3:])


def dsa_project(h, w_in):
    B, S, _ = h.shape
    p = h @ w_in
    o = D_MODEL
    q = p[..., :o].reshape(B, S, N_KV_HEADS, GROUP, HEAD_DIM)
    k = p[..., o:o + KV_WIDTH].reshape(B, S, N_KV_HEADS, HEAD_DIM)
    o += KV_WIDTH
    v = p[..., o:o + KV_WIDTH].reshape(B, S, N_KV_HEADS, HEAD_DIM)
    o += KV_WIDTH
    qi = p[..., o:o + IDX_HEADS * IDX_DIM].reshape(B, S, IDX_HEADS, IDX_DIM)
    o += IDX_HEADS * IDX_DIM
    ki = p[..., o:o + IDX_DIM]
    o += IDX_DIM
    wi = p[..., o:] * IDX_SCALE
    return q, k, v, qi, ki, wi


def dsa_attend(q, qi, wi, qpos, k_all, v_all, ki_all, topk):
    B = q.shape[0]
    L = k_all.shape[1]
    kpos = jnp.arange(L)
    s = jnp.einsum('bqhd,bld->bqhl', qi, ki_all)
    score = jnp.einsum('bqhl,bqh->bql', jax.nn.relu(s).astype(jnp.float32), wi.astype(jnp.float32))
    score = jnp.where((kpos[None, :] <= qpos[:, None])[None], score, -jnp.inf)
    _, top_idx = lax.top_k(score, topk)
    valid = top_idx <= qpos[None, :, None]
    bidx = jnp.arange(B)[:, None, None]
    k_sel = k_all[bidx, top_idx]
    v_sel = v_all[bidx, top_idx]
    logits = jnp.einsum('bqhgd,bqkhd->bqhgk', q, k_sel) * ATTN_SCALE
    p = masked_softmax(logits, valid[:, :, None, None, :])
    return jnp.einsum('bqhgk,bqkhd->bqhgd', p.astype(v_sel.dtype), v_sel)


def dsa_prompt(h, w_in, w_o):
    B, S, _ = h.shape
    q, k, v, qi, ki, wi = dsa_project(h, w_in)
    topk = min(DSA_TOPK, S // 4)
    nb = S // Q_BLOCK

    def blk(args):
        bi, qb, qib, wib = args
        qpos = bi * Q_BLOCK + jnp.arange(Q_BLOCK)
        return dsa_attend(qb, qib, wib, qpos, k, v, ki, topk)

    o = lax.map(blk, (jnp.arange(nb), to_blocks(q, Q_BLOCK), to_blocks(qi, Q_BLOCK), to_blocks(wi, Q_BLOCK)))
    o = from_blocks(o).reshape(B, S, D_MODEL)
    return o @ w_o, jnp.stack([k, v], axis=2), ki


def dsa_sample(h, cache_kv, cache_ki, layer, page_table, w_in, w_o):
    DB, T, _ = h.shape
    q, k, v, qi, ki, wi = dsa_project(h, w_in)
    past_kv = gather_pages(cache_kv, layer, page_table)
    past_ki = gather_pages(cache_ki, layer, page_table)
    P = past_kv.shape[1]
    k_all = jnp.concatenate([past_kv[:, :, 0], k], axis=1)
    v_all = jnp.concatenate([past_kv[:, :, 1], v], axis=1)
    ki_all = jnp.concatenate([past_ki, ki], axis=1)
    topk = min(DSA_TOPK, (P + T) // 4)
    qpos = P + jnp.arange(T)
    o = dsa_attend(q, qi, wi, qpos, k_all, v_all, ki_all, topk).reshape(DB, T, D_MODEL)
    return o @ w_o, jnp.stack([k, v], axis=2), ki


def nsa_project(h, w_in):
    B, S, _ = h.shape
    p = h @ w_in
    q = p[..., :D_MODEL].reshape(B, S, N_KV_HEADS, GROUP, HEAD_DIM)
    kv = p[..., D_MODEL:D_MODEL + 6 * KV_WIDTH].reshape(B, S, 3, 2, N_KV_HEADS, HEAD_DIM)
    gates = jax.nn.sigmoid(p[..., D_MODEL + 6 * KV_WIDTH:]).reshape(B, S, N_KV_HEADS, GROUP, 3)
    return q, kv, gates


def compress(rows, pe, w1, w2):
    B, L = rows.shape[:2]
    nc = L // CMP_LEN
    blk = rows[:, :nc * CMP_LEN].reshape(B, nc, CMP_LEN, N_KV_HEADS, HEAD_DIM) + pe[None, None, :, None, :]
    blk = jnp.moveaxis(blk, 3, 2).reshape(B, nc, N_KV_HEADS, CMP_LEN * HEAD_DIM)
    return jax.nn.gelu(blk @ w1) @ w2


def sel_blocks(rows):
    B, L = rows.shape[:2]
    ns = -(-L // SEL_BLOCK)
    rows = jnp.pad(rows, ((0, 0), (0, ns * SEL_BLOCK - L), (0, 0), (0, 0)))
    return jnp.transpose(rows.reshape(B, ns, SEL_BLOCK, N_KV_HEADS, HEAD_DIM), (0, 3, 1, 2, 4))


def nsa_long_range(rows, cmp_pe, cmp_w1, cmp_w2):
    kc = compress(rows[:, :, 0, 0], cmp_pe[0], cmp_w1[0], cmp_w2[0])
    vc = compress(rows[:, :, 0, 1], cmp_pe[1], cmp_w1[1], cmp_w2[1])
    return kc, vc, sel_blocks(rows[:, :, 1, 0]), sel_blocks(rows[:, :, 1, 1])


def nsa_attend(q, gates, qpos, kc, vc, ks_blk, vs_blk, kw, vw, kwpos):
    B, Q = q.shape[:2]
    nc = kc.shape[1]
    cmask = ((jnp.arange(nc) + 1) * CMP_LEN - 1)[None, :] <= qpos[:, None]
    pc = masked_softmax(jnp.einsum('bqhgd,bchd->bqhgc', q, kc) * ATTN_SCALE, cmask[None, :, None, None, :])
    o_cmp = jnp.einsum('bqhgc,bchd->bqhgd', pc.astype(vc.dtype), vc)
    ns = ks_blk.shape[2]
    ratio = SEL_BLOCK // CMP_LEN
    imp = jnp.pad(pc.sum(axis=3), ((0, 0), (0, 0), (0, 0), (0, ns * ratio - nc)))
    imp = imp.reshape(B, Q, N_KV_HEADS, ns, ratio).sum(-1)
    blk = jnp.arange(ns)[None, :]
    cur = (qpos // SEL_BLOCK)[:, None]
    forced = (blk == 0) | (blk == cur) | (blk == cur - 1)
    imp = jnp.where(forced[None, :, None, :], jnp.inf, imp)
    imp = jnp.where((blk <= cur)[None, :, None, :], imp, -jnp.inf)
    _, sel = lax.top_k(imp, min(N_SEL, ns))
    bidx = jnp.arange(B)[:, None, None, None]
    hidx = jnp.arange(N_KV_HEADS)[None, None, :, None]
    k_sel = ks_blk[bidx, hidx, sel]
    v_sel = vs_blk[bidx, hidx, sel]
    n = sel.shape[-1]
    spos = sel[..., None] * SEL_BLOCK + jnp.arange(SEL_BLOCK)
    smask = (spos <= qpos[None, :, None, None, None]).reshape(B, Q, N_KV_HEADS, 1, n * SEL_BLOCK)
    ls = jnp.einsum('bqhgd,bqhnsd->bqhgns', q, k_sel).reshape(B, Q, N_KV_HEADS, GROUP, n * SEL_BLOCK) * ATTN_SCALE
    ps = masked_softmax(ls, smask)
    o_sel = jnp.einsum('bqhgk,bqhkd->bqhgd', ps.astype(v_sel.dtype), v_sel.reshape(B, Q, N_KV_HEADS, n * SEL_BLOCK, HEAD_DIM))
    dist = qpos[:, None] - kwpos[None, :]
    wmask = (dist >= 0) & (dist < WINDOW) & (kwpos[None, :] >= 0)
    pw = masked_softmax(jnp.einsum('bqhgd,bkhd->bqhgk', q, kw) * ATTN_SCALE, wmask[None, :, None, None, :])
    o_win = jnp.einsum('bqhgk,bkhd->bqhgd', pw.astype(vw.dtype), vw)
    g = gates.astype(q.dtype)
    return g[..., 0:1] * o_cmp + g[..., 1:2] * o_sel + g[..., 2:3] * o_win


def nsa_prompt(h, w_in, cmp_pe, cmp_w1, cmp_w2, w_o):
    B, S, _ = h.shape
    q, kv, gates = nsa_project(h, w_in)
    kc, vc, ks_blk, vs_blk = nsa_long_range(kv[:, :, :2], cmp_pe, cmp_w1, cmp_w2)
    pad = ((0, 0), (WINDOW, 0), (0, 0), (0, 0))
    kw_pad = jnp.pad(kv[:, :, 2, 0], pad)
    vw_pad = jnp.pad(kv[:, :, 2, 1], pad)
    nb = S // Q_BLOCK

    def blk(args):
        bi, qb, gb = args
        t0 = bi * Q_BLOCK
        qpos = t0 + jnp.arange(Q_BLOCK)
        kw = lax.dynamic_slice_in_dim(kw_pad, t0, WINDOW + Q_BLOCK, axis=1)
        vw = lax.dynamic_slice_in_dim(vw_pad, t0, WINDOW + Q_BLOCK, axis=1)
        kwpos = t0 - WINDOW + jnp.arange(WINDOW + Q_BLOCK)
        return nsa_attend(qb, gb, qpos, kc, vc, ks_blk, vs_blk, kw, vw, kwpos)

    o = lax.map(blk, (jnp.arange(nb), to_blocks(q, Q_BLOCK), to_blocks(gates, Q_BLOCK)))
    o = from_blocks(o).reshape(B, S, D_MODEL)
    w_keep = min(WINDOW, S)
    return o @ w_o, kv[:, :, :2], kv[:, S - w_keep:, 2]


def nsa_sample(h, cache_kv, win_buf, layer, page_table, w_in, cmp_pe, cmp_w1, cmp_w2, w_o):
    DB, T, _ = h.shape
    q, kv, gates = nsa_project(h, w_in)
    past = gather_pages(cache_kv, layer, page_table)
    P = past.shape[1]
    full = jnp.concatenate([past, kv[:, :, :2]], axis=1)
    kc, vc, ks_blk, vs_blk = nsa_long_range(full, cmp_pe, cmp_w1, cmp_w2)
    win = jnp.concatenate([win_buf, kv[:, :, 2]], axis=1)
    wb = win_buf.shape[1]
    kwpos = P - wb + jnp.arange(wb + T)
    qpos = P + jnp.arange(T)
    o = nsa_attend(q, gates, qpos, kc, vc, ks_blk, vs_blk, win[:, :, 0], win[:, :, 1], kwpos)
    return o.reshape(DB, T, D_MODEL) @ w_o, kv[:, :, :2], win[:, T:]


def causal_conv(x, buf, w, b):
    S = x.shape[1]
    xp = jnp.concatenate([buf, x], axis=1)
    y = b
    for j in range(CONV_WIDTH):
        y = y + w[j] * xp[:, j:j + S]
    return y, xp[:, -(CONV_WIDTH - 1):]


def mlstm_chunkwise(q, k, v, ig, lf, C0, n0, m0, chunk):
    f32 = jnp.float32
    xs = tuple(to_blocks(a.astype(f32), chunk) for a in (q, k, v, ig, lf))
    tri = jnp.tril(jnp.ones((chunk, chunk), bool))

    def step(carry, inp):
        C, n, m = carry
        qc, kc, vc, ic, fc = inp
        b = jnp.cumsum(fc, axis=1)
        dmat = b[:, :, None, :] - b[:, None, :, :] + ic[:, None, :, :]
        dmat = jnp.where(tri[None, :, :, None], dmat, -jnp.inf)
        inter = b + m[:, None, :]
        m_loc = jnp.maximum(inter, jnp.max(dmat, axis=2))
        a = jnp.einsum('bjhd,bshd->bjsh', qc, kc) * jnp.exp(dmat - m_loc[:, :, None, :])
        w_inter = jnp.exp(inter - m_loc)
        num = jnp.einsum('bjsh,bshd->bjhd', a, vc) + w_inter[..., None] * jnp.einsum('bhvk,bjhk->bjhv', C, qc)
        den = a.sum(axis=2) + w_inter * jnp.einsum('bhk,bjhk->bjh', n, qc)
        hout = num / jnp.maximum(jnp.abs(den), jnp.exp(-m_loc))[..., None]
        b_end = b[:, -1]
        lg = b_end[:, None, :] - b + ic
        m_new = jnp.maximum(b_end + m, jnp.max(lg, axis=1))
        wg = jnp.exp(lg - m_new[:, None, :])
        decay = jnp.exp(b_end + m - m_new)
        C_new = decay[..., None, None] * C + jnp.einsum('bsh,bshv,bshk->bhvk', wg, vc, kc)
        n_new = decay[..., None] * n + jnp.einsum('bsh,bshk->bhk', wg, kc)
        return (C_new, n_new, m_new), hout

    (C, n, m), hs = lax.scan(step, (C0.astype(f32), n0.astype(f32), m0.astype(f32)), xs)
    return from_blocks(hs), C, n, m


def mlstm_mixer(h, conv_buf, C0, n0, m0, w_in, conv_w, conv_b, w_qkv, w_gate, b_gate, norm_g, skip, w_o):
    B, S, _ = h.shape
    p = h @ w_in
    xm, z = p[..., :MLSTM_INNER], p[..., MLSTM_INNER:]
    xc, new_buf = causal_conv(xm, conv_buf, conv_w, conv_b)
    xc = jax.nn.silu(xc)
    heads = lambda a: a.reshape(B, S, MLSTM_HEADS, MLSTM_HEAD_DIM)
    flat = lambda a: a.reshape(B, S, MLSTM_INNER)
    q = jnp.einsum('bshd,hde->bshe', heads(xc), w_qkv[0])
    k = jnp.einsum('bshd,hde->bshe', heads(xc), w_qkv[1]) * MLSTM_HEAD_DIM ** -0.5
    v = jnp.einsum('bshd,hde->bshe', heads(xm), w_qkv[2])
    g = (jnp.concatenate([flat(q), flat(k), flat(v)], axis=-1) @ w_gate + b_gate).astype(jnp.float32)
    ig = g[..., :MLSTM_HEADS]
    lf = jax.nn.log_sigmoid(g[..., MLSTM_HEADS:])
    hc, C, n, m = mlstm_chunkwise(q, k, v, ig, lf, C0, n0, m0, math.gcd(S, MLSTM_CHUNK))
    mu = jnp.mean(hc, -1, keepdims=True)
    var = jnp.mean(jnp.square(hc - mu), -1, keepdims=True)
    hn = ((hc - mu) * lax.rsqrt(var + LN_EPS)).reshape(B, S, MLSTM_INNER) * norm_g
    out = (hn.astype(h.dtype) + skip * xc) * jax.nn.silu(z)
    return out @ w_o, new_buf, C, n, m


def peer(h, w_q, sub_keys, u_tab, v_tab):
    B, S, _ = h.shape
    x = h.reshape(B * S, D_MODEL)
    T = x.shape[0]
    Tp = -(-T // PEER_BLOCK) * PEER_BLOCK
    x = jnp.pad(x, ((0, Tp - T), (0, 0)))

    def blk(xb):
        tb = xb.shape[0]
        q = (xb @ w_q).reshape(tb, PEER_HEADS, 2, PEER_KEY_DIM // 2)
        s = jnp.einsum('thcd,cnd->thcn', q, sub_keys).astype(jnp.float32)
        v_half, i_half = lax.top_k(s, PEER_TOPK)
        cand = v_half[:, :, 0, :, None] + v_half[:, :, 1, None, :]
        cval, cidx = lax.top_k(cand.reshape(tb, PEER_HEADS, PEER_TOPK * PEER_TOPK), PEER_TOPK)
        i1 = jnp.take_along_axis(i_half[:, :, 0], cidx // PEER_TOPK, axis=-1)
        i2 = jnp.take_along_axis(i_half[:, :, 1], cidx % PEER_TOPK, axis=-1)
        eidx = i1 * PEER_KEYS + i2
        gate = jax.nn.softmax(cval, axis=-1)
        act = jax.nn.gelu(jnp.einsum('thkd,td->thk', u_tab[eidx], xb).astype(jnp.float32))
        coef = (gate * act).astype(xb.dtype)
        return jnp.einsum('thk,thkd->td', coef, v_tab[eidx])

    out = lax.map(blk, x.reshape(Tp // PEER_BLOCK, PEER_BLOCK, D_MODEL)).reshape(Tp, D_MODEL)[:T]
    return out.reshape(B, S, D_MODEL)


def adaln(c, w, b):
    return (jax.nn.silu(c) @ w + b).reshape(c.shape[0], 2, 3, D_MODEL)


def modulate(x, mod, s):
    return x * (1 + mod[:, s, 1, None, :]) + mod[:, s, 0, None, :]


def post_norm(x, y, mod, s, g, b):
    return layer_norm(ALPHA * x + mod[:, s, 2, None, :] * y, g, b)


def setup_inputs(seed: int = 0) -> dict:
    key = jax.random.key(seed)
    keys = jax.random.split(key, 48)
    cnt = [0]

    def nxt():
        cnt[0] += 1
        return keys[cnt[0] - 1]

    def nrm(shape, scale):
        return jax.random.normal(nxt(), shape, jnp.float32) * scale

    n_pages = PAST_LEN // PAGE_SIZE
    n_used = DEC_BATCH * n_pages
    n_pool = n_used + -(-n_used // 4)
    w_buf = min(WINDOW, PAST_LEN)
    perm = jax.random.permutation(nxt(), n_pool)
    page_table = perm[:n_used].reshape(DEC_BATCH, n_pages).astype(jnp.int32)
    D = D_MODEL
    return {
        "x_prompt": nrm((BATCH, SEQ, D), 1.0),
        "x_sample": nrm((DEC_BATCH, DEC_SEQ, D), 1.0),
        "cache_a_kv": nrm((N_LAYERS_A, n_pool, PAGE_SIZE, 2, N_KV_HEADS, HEAD_DIM), 1.0),
        "cache_a_kidx": nrm((N_LAYERS_A, n_pool, PAGE_SIZE, IDX_DIM), 1.0),
        "cache_b_kv": nrm((N_LAYERS_B, n_pool, PAGE_SIZE, 2, 2, N_KV_HEADS, HEAD_DIM), 1.0),
        "state_b_win": nrm((N_LAYERS_B, DEC_BATCH, w_buf, 2, N_KV_HEADS, HEAD_DIM), 1.0),
        "state_c_conv": nrm((N_LAYERS_C, DEC_BATCH, CONV_WIDTH - 1, MLSTM_INNER), 1.0),
        "state_c_C": nrm((N_LAYERS_C, DEC_BATCH, MLSTM_HEADS, MLSTM_HEAD_DIM, MLSTM_HEAD_DIM), 0.1),
        "state_c_n": nrm((N_LAYERS_C, DEC_BATCH, MLSTM_HEADS, MLSTM_HEAD_DIM), 0.1),
        "state_c_m": nrm((N_LAYERS_C, DEC_BATCH, MLSTM_HEADS), 0.5),
        "page_table": page_table,
        "c_prompt": nrm((BATCH, D), 1.0),
        "c_sample": nrm((DEC_BATCH, D), 1.0),
        "a_w_in": nrm((N_LAYERS_A, D, A_IN), D ** -0.5),
        "a_w_o": nrm((N_LAYERS_A, D, D), BETA * D ** -0.5),
        "b_w_in": nrm((N_LAYERS_B, D, B_IN), D ** -0.5),
        "b_cmp_pe": nrm((N_LAYERS_B, 2, CMP_LEN, HEAD_DIM), 0.1),
        "b_cmp_w1": nrm((N_LAYERS_B, 2, CMP_LEN * HEAD_DIM, CMP_HIDDEN), (CMP_LEN * HEAD_DIM) ** -0.5),
        "b_cmp_w2": nrm((N_LAYERS_B, 2, CMP_HIDDEN, HEAD_DIM), CMP_HIDDEN ** -0.5),
        "b_w_o": nrm((N_LAYERS_B, D, D), BETA * D ** -0.5),
        "c_w_in": nrm((N_LAYERS_C, D, 2 * MLSTM_INNER), D ** -0.5),
        "c_conv_w": nrm((N_LAYERS_C, CONV_WIDTH, MLSTM_INNER), CONV_WIDTH ** -0.5),
        "c_conv_b": nrm((N_LAYERS_C, MLSTM_INNER), 0.01),
        "c_w_qkv": nrm((N_LAYERS_C, 3, MLSTM_HEADS, MLSTM_HEAD_DIM, MLSTM_HEAD_DIM), MLSTM_HEAD_DIM ** -0.5),
        "c_w_gate": nrm((N_LAYERS_C, 3 * MLSTM_INNER, 2 * MLSTM_HEADS), (3 * MLSTM_INNER) ** -0.5),
        "c_b_gate": jnp.concatenate([nrm((N_LAYERS_C, MLSTM_HEADS), 0.1),
                                     3.0 + 3.0 * jax.random.uniform(nxt(), (N_LAYERS_C, MLSTM_HEADS), jnp.float32)], axis=-1),
        "c_norm_g": 1.0 + nrm((N_LAYERS_C, MLSTM_INNER), 0.02),
        "c_skip": 1.0 + nrm((N_LAYERS_C, MLSTM_INNER), 0.02),
        "c_w_o": nrm((N_LAYERS_C, MLSTM_INNER, D), BETA * MLSTM_INNER ** -0.5),
        "ada_w": nrm((DEPTH, D, 6 * D), D ** -0.5),
        "ada_b": nrm((DEPTH, 6 * D), 0.02),
        "ln_g": 1.0 + nrm((DEPTH, 2, D), 0.02),
        "ln_b": nrm((DEPTH, 2, D), 0.02),
        "peer_w_q": nrm((DEPTH, D, PEER_HEADS * PEER_KEY_DIM), D ** -0.5),
        "peer_sub_keys": nrm((DEPTH, 2, PEER_KEYS, PEER_KEY_DIM // 2), (PEER_KEY_DIM // 2) ** -0.5),
        "peer_u": nrm((DEPTH, N_EXPERTS, D), D ** -0.5),
        "peer_v": nrm((DEPTH, N_EXPERTS, D), BETA),
    }


def reference(x_prompt, x_sample, cache_a_kv, cache_a_kidx, cache_b_kv, state_b_win, state_c_conv,
              state_c_C, state_c_n, state_c_m, page_table, c_prompt, c_sample,
              a_w_in, a_w_o, b_w_in, b_cmp_pe, b_cmp_w1, b_cmp_w2, b_w_o,
              c_w_in, c_conv_w, c_conv_b, c_w_qkv, c_w_gate, c_b_gate, c_norm_g, c_skip, c_w_o,
              ada_w, ada_b, ln_g, ln_b, peer_w_q, peer_sub_keys, peer_u, peer_v):
    xp, xs = x_prompt, x_sample
    B, S, _ = x_prompt.shape
    a_kv_p, a_kv_s, a_ki_p, a_ki_s = [], [], [], []
    b_kv_p, b_kv_s, b_win_p, b_win_s = [], [], [], []
    c_conv_p, c_conv_s, c_C_p, c_C_s, c_n_p, c_n_s, c_m_p, c_m_s = [], [], [], [], [], [], [], []
    for i in range(DEPTH):
        kind, j = i % N_MIXERS, i // N_MIXERS
        mp = adaln(c_prompt, ada_w[i], ada_b[i])
        ms = adaln(c_sample, ada_w[i], ada_b[i])
        hp, hs = modulate(xp, mp, 0), modulate(xs, ms, 0)
        if kind == 0:
            op, kvp, kip = dsa_prompt(hp, a_w_in[j], a_w_o[j])
            os_, kvs, kis = dsa_sample(hs, cache_a_kv, cache_a_kidx, j, page_table, a_w_in[j], a_w_o[j])
            a_kv_p.append(kvp); a_kv_s.append(kvs); a_ki_p.append(kip); a_ki_s.append(kis)
        elif kind == 1:
            op, kvp, winp = nsa_prompt(hp, b_w_in[j], b_cmp_pe[j], b_cmp_w1[j], b_cmp_w2[j], b_w_o[j])
            os_, kvs, wins = nsa_sample(hs, cache_b_kv, state_b_win[j], j, page_table, b_w_in[j],
                                        b_cmp_pe[j], b_cmp_w1[j], b_cmp_w2[j], b_w_o[j])
            b_kv_p.append(kvp); b_kv_s.append(kvs); b_win_p.append(winp); b_win_s.append(wins)
        else:
            cw = (c_w_in[j], c_conv_w[j], c_conv_b[j], c_w_qkv[j], c_w_gate[j], c_b_gate[j], c_norm_g[j], c_skip[j], c_w_o[j])
            zbuf = jnp.zeros((B, CONV_WIDTH - 1, MLSTM_INNER), hp.dtype)
            zC = jnp.zeros((B, MLSTM_HEADS, MLSTM_HEAD_DIM, MLSTM_HEAD_DIM), jnp.float32)
            zn = jnp.zeros((B, MLSTM_HEADS, MLSTM_HEAD_DIM), jnp.float32)
            zm = jnp.zeros((B, MLSTM_HEADS), jnp.float32)
            op, bp, Cp, np_, mp_ = mlstm_mixer(hp, zbuf, zC, zn, zm, *cw)
            os_, bs, Cs, ns_, ms_ = mlstm_mixer(hs, state_c_conv[j], state_c_C[j], state_c_n[j], state_c_m[j], *cw)
            c_conv_p.append(bp); c_conv_s.append(bs); c_C_p.append(Cp); c_C_s.append(Cs)
            c_n_p.append(np_); c_n_s.append(ns_); c_m_p.append(mp_); c_m_s.append(ms_)
        xp = post_norm(xp, op, mp, 0, ln_g[i, 0], ln_b[i, 0])
        xs = post_norm(xs, os_, ms, 0, ln_g[i, 0], ln_b[i, 0])
        fp = peer(modulate(xp, mp, 1), peer_w_q[i], peer_sub_keys[i], peer_u[i], peer_v[i])
        fs = peer(modulate(xs, ms, 1), peer_w_q[i], peer_sub_keys[i], peer_u[i], peer_v[i])
        xp = post_norm(xp, fp, mp, 1, ln_g[i, 1], ln_b[i, 1])
        xs = post_norm(xs, fs, ms, 1, ln_g[i, 1], ln_b[i, 1])
    y_prompt, y_sample = xp, xs
    return (y_prompt, y_sample,
            jnp.stack(a_kv_p), jnp.stack(a_kv_s), jnp.stack(a_ki_p), jnp.stack(a_ki_s),
            jnp.stack(b_kv_p), jnp.stack(b_kv_s), jnp.stack(b_win_p), jnp.stack(b_win_s),
            jnp.stack(c_conv_p), jnp.stack(c_conv_s), jnp.stack(c_C_p), jnp.stack(c_C_s),
            jnp.stack(c_n_p), jnp.stack(c_n_s), jnp.stack(c_m_p), jnp.stack(c_m_s))
```

```python
import functools
import math

import jax
import jax.numpy as jnp
from jax import lax
from jax.experimental import pallas as pl
from jax.experimental.pallas import tpu as pltpu

F32 = jnp.float32
BF16 = jnp.bfloat16

D_MODEL = 1024
DEPTH = 4
PAGE_SIZE = 128
N_MIXERS = 3

N_HEADS = 16
HEAD_DIM = D_MODEL // N_HEADS
N_KV_HEADS = 4
GROUP = N_HEADS // N_KV_HEADS
KV_WIDTH = N_KV_HEADS * HEAD_DIM
Q_BLOCK = 128
ATTN_SCALE = HEAD_DIM ** -0.5

IDX_HEADS = 8
IDX_DIM = 64
IDX_SCALE = (IDX_HEADS * IDX_DIM) ** -0.5
DSA_TOPK = 256

CMP_LEN = 32
CMP_HIDDEN = 2 * HEAD_DIM
SEL_BLOCK = 64
N_SEL = 16
WINDOW = 512

MLSTM_INNER = 2 * D_MODEL
MLSTM_HEADS = 4
MLSTM_HEAD_DIM = MLSTM_INNER // MLSTM_HEADS
CONV_WIDTH = 4
MLSTM_CHUNK = 64

PEER_HEADS = 8
PEER_KEYS = 128
N_EXPERTS = PEER_KEYS * PEER_KEYS
PEER_KEY_DIM = 256
PEER_TOPK = 16

ALPHA = (2 * DEPTH) ** 0.25
LN_EPS = 1e-5

LANES = 128
VMEM_LIMIT = 56 * 1024 * 1024

_NT = (((1,), (1,)), ((), ()))


def _cparams(*sem):
    return pltpu.CompilerParams(dimension_semantics=sem, vmem_limit_bytes=VMEM_LIMIT)


def _gelu(x):
    return 0.5 * x * (1.0 + jnp.tanh(math.sqrt(2.0 / math.pi) * (x + 0.044715 * (x * x * x))))


def _post_norm_math(x, y, gate, g, b):
    z = ALPHA * x + gate * y
    mu = jnp.mean(z, axis=-1, keepdims=True)
    zc = z - mu
    var = jnp.mean(zc * zc, axis=-1, keepdims=True)
    return zc * lax.rsqrt(var + LN_EPS) * g + b


def _adaln_kernel(c_ref, w_ref, b_ref, o_ref):
    c = c_ref[...]
    s = (c * jax.nn.sigmoid(c)).astype(BF16)
    o_ref[0] = jnp.dot(s, w_ref[0].astype(BF16), preferred_element_type=F32) + b_ref[0]


def adaln_all(c, ada_w, ada_b):
    M = c.shape[0]
    n = ada_w.shape[-1] // D_MODEL
    return pl.pallas_call(
        _adaln_kernel,
        out_shape=jax.ShapeDtypeStruct((DEPTH, M, n * D_MODEL), F32),
        grid=(DEPTH, n),
        in_specs=[pl.BlockSpec((M, D_MODEL), lambda i, j: (0, 0)),
                  pl.BlockSpec((1, D_MODEL, D_MODEL), lambda i, j: (i, 0, j)),
                  pl.BlockSpec((1, 1, D_MODEL), lambda i, j: (i, 0, j))],
        out_specs=pl.BlockSpec((1, M, D_MODEL), lambda i, j: (i, 0, j)),
        compiler_params=_cparams("arbitrary", "arbitrary"),
    )(c, ada_w, ada_b.reshape(DEPTH, 1, -1))


def _proj_kernel(x_ref, sh_ref, sc_ref, *refs, n_w, out_map):
    w_refs, o_refs = refs[:n_w], refs[n_w:]
    h = (x_ref[0] * (1.0 + sc_ref[0]) + sh_ref[0]).astype(BF16)
    done = {}
    for o_ref, wi in zip(o_refs, out_map):
        if wi not in done:
            done[wi] = jnp.dot(h, w_refs[wi][...], preferred_element_type=F32)
        o_ref[0] = done[wi].astype(o_ref.dtype)


def proj(x, shift, scale, weights, outs, tm):
    nb, S, D = x.shape
    rows = shift.shape[1]
    mblk = (1, tm, D) if rows == S else (1, 1, D)
    mmap = (lambda b, i: (b, i, 0)) if rows == S else (lambda b, i: (b, 0, 0))
    in_specs = [pl.BlockSpec((1, tm, D), lambda b, i: (b, i, 0)),
                pl.BlockSpec(mblk, mmap), pl.BlockSpec(mblk, mmap)]
    in_specs += [pl.BlockSpec(w.shape, lambda b, i: (0, 0)) for w in weights]
    out_shape = [jax.ShapeDtypeStruct((nb, S, weights[wi].shape[1]), dt) for wi, dt in outs]
    out_specs = [pl.BlockSpec((1, tm, weights[wi].shape[1]), lambda b, i: (b, i, 0)) for wi, _ in outs]
    return pl.pallas_call(
        functools.partial(_proj_kernel, n_w=len(weights), out_map=tuple(wi for wi, _ in outs)),
        out_shape=out_shape, grid=(nb, S // tm), in_specs=in_specs, out_specs=out_specs,
        compiler_params=_cparams("parallel", "parallel"),
    )(x, shift, scale, *weights)


def _outproj_kernel(o_ref, w_ref, x_ref, gate_ref, g_ref, b_ref, y_ref):
    y = jnp.dot(o_ref[0], w_ref[...], preferred_element_type=F32)
    y_ref[0] = _post_norm_math(x_ref[0], y, gate_ref[0], g_ref[...], b_ref[...])


def outproj_postnorm(o, w, x, gate, g, b, tm):
    nb, S, K = o.shape
    D = x.shape[-1]
    rows = gate.shape[1]
    mblk = (1, tm, D) if rows == S else (1, 1, D)
    mmap = (lambda bi, i: (bi, i, 0)) if rows == S else (lambda bi, i: (bi, 0, 0))
    return pl.pallas_call(
        _outproj_kernel,
        out_shape=jax.ShapeDtypeStruct(x.shape, F32),
        grid=(nb, S // tm),
        in_specs=[pl.BlockSpec((1, tm, K), lambda bi, i: (bi, i, 0)),
                  pl.BlockSpec((K, D), lambda bi, i: (0, 0)),
                  pl.BlockSpec((1, tm, D), lambda bi, i: (bi, i, 0)),
                  pl.BlockSpec(mblk, mmap),
                  pl.BlockSpec((1, D), lambda bi, i: (0, 0)),
                  pl.BlockSpec((1, D), lambda bi, i: (0, 0))],
        out_specs=pl.BlockSpec((1, tm, D), lambda bi, i: (bi, i, 0)),
        compiler_params=_cparams("parallel", "parallel"),
    )(o, w, x, gate, g.reshape(1, D), b.reshape(1, D))


def _top_rows(x, iota, n):
    big = float(x.shape[0])
    vals, idxs = [], []
    for _ in range(n):
        m = jnp.max(x, axis=0, keepdims=True)
        idx = jnp.min(jnp.where(x == m, iota, big), axis=0, keepdims=True)
        vals.append(m)
        idxs.append(idx)
        x = jnp.where(iota == idx, -jnp.inf, x)
    return vals, idxs


def _peer_route_kernel(x_ref, sh_ref, sc_ref, wq_ref, keys_ref, i1_ref, i2_ref, g_ref):
    tm = x_ref.shape[1]
    h = (x_ref[0] * (1.0 + sc_ref[0]) + sh_ref[0]).astype(BF16)
    q = jnp.dot(h, wq_ref[...], preferred_element_type=F32).astype(BF16)
    iota_k = lax.broadcasted_iota(jnp.int32, (PEER_KEYS, tm), 0).astype(F32)
    n_cand = PEER_TOPK * PEER_TOPK
    iota_c = lax.broadcasted_iota(jnp.int32, (n_cand, tm), 0).astype(F32)
    half = PEER_KEY_DIM // 2
    i1_rows, i2_rows, g_rows = [], [], []
    for hd in range(PEER_HEADS):
        tops = []
        for c in range(2):
            col = (hd * 2 + c) * half
            s_t = lax.dot_general(keys_ref[c], q[:, col:col + half], _NT, preferred_element_type=F32)
            tops.append(_top_rows(s_t, iota_k, PEER_TOPK))
        (v1, id1), (v2, id2) = tops
        v2_all = jnp.concatenate(v2, axis=0)
        cand = jnp.concatenate([v1[r] + v2_all for r in range(PEER_TOPK)], axis=0)
        cvals, cidx = _top_rows(cand, iota_c, PEER_TOPK)
        cv = jnp.concatenate(cvals, axis=0)
        ci = jnp.concatenate(cidx, axis=0)
        r1 = jnp.floor(ci * (1.0 / PEER_TOPK))
        r2 = ci - r1 * PEER_TOPK
        i1 = jnp.zeros_like(ci)
        i2 = jnp.zeros_like(ci)
        for r in range(PEER_TOPK):
            i1 = i1 + jnp.where(r1 == float(r), id1[r], 0.0)
            i2 = i2 + jnp.where(r2 == float(r), id2[r], 0.0)
        e = jnp.exp(cv - cvals[0])
        gate = e / jnp.sum(e, axis=0, keepdims=True)
        i1_rows.append(i1)
        i2_rows.append(i2)
        g_rows.append(gate)
    i1_ref[0] = jnp.concatenate(i1_rows, axis=0).T
    i2_ref[0] = jnp.concatenate(i2_rows, axis=0).T
    g_ref[0] = jnp.concatenate(g_rows, axis=0).T


def _mod_specs(rows, S, tm, D):
    if rows == S:
        return pl.BlockSpec((1, tm, D), lambda b, i, *_: (b, i, 0))
    return pl.BlockSpec((1, 1, D), lambda b, i, *_: (b, 0, 0))


def peer_route(x, shift, scale, wq, keys, tm):
    nb, S, D = x.shape
    nsel = PEER_HEADS * PEER_TOPK
    mspec = _mod_specs(shift.shape[1], S, tm, D)
    return pl.pallas_call(
        _peer_route_kernel,
        out_shape=[jax.ShapeDtypeStruct((nb, S, nsel), F32)] * 3,
        grid=(nb, S // tm),
        in_specs=[pl.BlockSpec((1, tm, D), lambda b, i: (b, i, 0)), mspec, mspec,
                  pl.BlockSpec(wq.shape, lambda b, i: (0, 0)),
                  pl.BlockSpec(keys.shape, lambda b, i: (0, 0, 0))],
        out_specs=[pl.BlockSpec((1, tm, nsel), lambda b, i: (b, i, 0))] * 3,
        compiler_params=_cparams("parallel", "parallel"),
    )(x, shift, scale, wq, keys)


def _peer_expert_kernel(x_ref, sh_ref, sc_ref, gt_ref, i1_ref, i2_ref, g_ref, u_ref, v_ref, lg_ref, lb_ref,
                        y_ref, w3_ref, acc_ref, h_ref, *, ac):
    tm = x_ref.shape[1]
    j = pl.program_id(2)

    @pl.when(j == 0)
    def _():
        h_ref[...] = (x_ref[0] * (1.0 + sc_ref[0]) + sh_ref[0]).astype(BF16)
        acc_ref[...] = jnp.zeros_like(acc_ref)
        iota_s = lax.broadcasted_iota(jnp.int32, (PEER_KEYS, LANES), 0).astype(F32)

        def build(t, carry):
            i1 = i1_ref[0, pl.ds(t, 1), :]
            i2 = i2_ref[0, pl.ds(t, 1), :]
            g = g_ref[0, pl.ds(t, 1), :]
            g1t = jnp.where(iota_s == i1, g, 0.0).astype(BF16)
            o2t = jnp.where(iota_s == i2, 1.0, 0.0).astype(BF16)
            wt = lax.dot_general(g1t, o2t, _NT, preferred_element_type=F32)
            w3_ref[pl.ds(pl.multiple_of(t * PEER_KEYS, PEER_KEYS), PEER_KEYS), :] = wt
            return carry

        lax.fori_loop(0, tm, build, 0)

    act = _gelu(lax.dot_general(h_ref[...], u_ref[...], _NT, preferred_element_type=F32))
    wc = jnp.concatenate([w3_ref[pl.ds(j * ac + a, tm, stride=PEER_KEYS), :] for a in range(ac)], axis=1)
    coef = (wc * act).astype(BF16)
    acc_ref[...] += jnp.dot(coef, v_ref[...], preferred_element_type=F32)

    @pl.when(j == pl.num_programs(2) - 1)
    def _():
        y_ref[0] = _post_norm_math(x_ref[0], acc_ref[...], gt_ref[0], lg_ref[...], lb_ref[...])


def peer_experts(x, shift, scale, gate, i1, i2, g, u, v, ln_g, ln_b, tm, ac):
    nb, S, D = x.shape
    nsel = PEER_HEADS * PEER_TOPK
    ce = ac * PEER_KEYS
    mspec = _mod_specs(shift.shape[1], S, tm, D)
    pick = pl.BlockSpec((1, tm, nsel), lambda b, i, j: (b, i, 0))
    return pl.pallas_call(
        functools.partial(_peer_expert_kernel, ac=ac),
        out_shape=jax.ShapeDtypeStruct(x.shape, F32),
        grid=(nb, S // tm, PEER_KEYS // ac),
        in_specs=[pl.BlockSpec((1, tm, D), lambda b, i, j: (b, i, 0)), mspec, mspec, mspec, pick, pick, pick,
                  pl.BlockSpec((ce, D), lambda b, i, j: (j, 0)),
                  pl.BlockSpec((ce, D), lambda b, i, j: (j, 0)),
                  pl.BlockSpec((1, D), lambda b, i, j: (0, 0)),
                  pl.BlockSpec((1, D), lambda b, i, j: (0, 0))],
        out_specs=pl.BlockSpec((1, tm, D), lambda b, i, j: (b, i, 0)),
        scratch_shapes=[pltpu.VMEM((tm * PEER_KEYS, PEER_KEYS), F32),
                        pltpu.VMEM((tm, D), F32),
                        pltpu.VMEM((tm, D), BF16)],
        compiler_params=_cparams("parallel", "parallel", "arbitrary"),
    )(x, shift, scale, gate, i1, i2, g, u, v, ln_g.reshape(1, D), ln_b.reshape(1, D))


def peer_layer(x, shift, scale, gate, wq, keys, u, v, ln_g, ln_b, tm):
    i1, i2, g = peer_route(x, shift, scale, wq, keys, tm)
    return peer_experts(x, shift, scale, gate, i1, i2, g, u, v, ln_g, ln_b, tm, ac=8)


INT_MIN = -2 ** 31
NEG_INF = float("-inf")


def _to_key(x):
    b = lax.bitcast_convert_type(x, jnp.int32)
    return b ^ ((b >> 31) & 0x7FFFFFFF)


def _count_cols(key_ref, nck, kc, pred):
    rows = key_ref.shape[0]

    def body(c, acc):
        c0 = pl.multiple_of(c * kc, kc)
        hit = jnp.where(pred(key_ref[:, pl.ds(c0, kc)], c0), 1.0, 0.0)
        part = hit[:, 0:LANES]
        for u in range(1, kc // LANES):
            part = part + hit[:, u * LANES:(u + 1) * LANES]
        return acc + part

    acc = lax.fori_loop(0, nck, body, jnp.zeros((rows, LANES), F32))
    return jnp.sum(acc, axis=1, keepdims=True)


def _topk_threshold(key_ref, nck, kc, k, idx_bits):
    rows = key_ref.shape[0]
    kf = float(k)

    def bit_step(p, t_u):
        cand_u = t_u | jnp.left_shift(jnp.int32(1), 31 - p)
        cand_s = cand_u ^ INT_MIN
        cnt = _count_cols(key_ref, nck, kc, lambda tile, c0: tile >= cand_s)
        return jnp.where(cnt >= kf, cand_u, t_u)

    t_u = lax.fori_loop(0, 32, bit_step, jnp.zeros((rows, 1), jnp.int32))
    thr = jnp.maximum(t_u ^ INT_MIN, INT_MIN + 1)
    n_ge = _count_cols(key_ref, nck, kc, lambda tile, c0: tile >= thr)

    @pl.when(jnp.max(n_ge) > kf)
    def _():
        need = kf - _count_cols(key_ref, nck, kc, lambda tile, c0: tile > thr)
        iota = lax.broadcasted_iota(jnp.int32, (rows, kc), 1)

        def idx_step(p, j_hi):
            cand = j_hi | jnp.left_shift(jnp.int32(1), idx_bits - 1 - p)
            cnt = _count_cols(key_ref, nck, kc, lambda tile, c0: (tile == thr) & (c0 + iota < cand))
            return jnp.where(cnt <= need, cand, j_hi)

        j_hi = lax.fori_loop(0, idx_bits, idx_step, jnp.zeros((rows, 1), jnp.int32))
        surplus = n_ge > kf

        def lower(c, carry):
            c0 = pl.multiple_of(c * kc, kc)
            tile = key_ref[:, pl.ds(c0, kc)]
            drop = (tile == thr) & (c0 + iota >= j_hi) & surplus
            key_ref[:, pl.ds(c0, kc)] = jnp.where(drop, thr - 1, tile)
            return carry

        lax.fori_loop(0, nck, lower, 0)

    return thr


def _stack_heads(q_ref, qs_ref, tq):
    for hd in range(N_HEADS):
        qs_ref[hd * tq:(hd + 1) * tq, :] = (q_ref[0, :, hd * HEAD_DIM:(hd + 1) * HEAD_DIM] * ATTN_SCALE).astype(BF16)


def _flash_init(m_ref, l_ref, acc_ref):
    m_ref[...] = jnp.full(m_ref.shape, NEG_INF, F32)
    l_ref[...] = jnp.zeros(l_ref.shape, F32)
    acc_ref[...] = jnp.zeros(acc_ref.shape, F32)


def _flash_update(s, rows, v, m_ref, l_ref, acc_ref):
    m_prev = m_ref[rows, :]
    m_new = jnp.maximum(m_prev, jnp.max(s, axis=1, keepdims=True))
    m_safe = jnp.where(m_new == NEG_INF, 0.0, m_new)
    p = jnp.exp(s - m_safe[:, :1])
    alpha = jnp.exp(m_prev - m_safe)
    l_ref[rows, :] = alpha * l_ref[rows, :] + jnp.sum(p, axis=1, keepdims=True)
    acc_ref[rows, :] = alpha[:, :HEAD_DIM] * acc_ref[rows, :] + jnp.dot(p.astype(BF16), v, preferred_element_type=F32)
    m_ref[rows, :] = m_new


def _flash_result(rows, l_ref, acc_ref):
    return acc_ref[rows, :] / jnp.maximum(l_ref[rows, :HEAD_DIM], 1e-30)


def _grouped_flash_step(mask_of, kch, vch, qs_ref, m_ref, l_ref, acc_ref, tq):
    for h in range(N_KV_HEADS):
        cols = slice(h * HEAD_DIM, (h + 1) * HEAD_DIM)
        logits = lax.dot_general(qs_ref[h * GROUP * tq:(h + 1) * GROUP * tq, :], kch[:, cols], _NT,
                                 preferred_element_type=F32)
        mask = mask_of(h)
        for g in range(GROUP):
            hd = h * GROUP + g
            s = jnp.where(mask, logits[g * tq:(g + 1) * tq], NEG_INF)
            _flash_update(s, slice(hd * tq, (hd + 1) * tq), vch[:, cols], m_ref, l_ref, acc_ref)


def _index_scores(qi, wi, kic):
    sc = None
    for hh in range(IDX_HEADS):
        s = lax.dot_general(qi[:, hh * IDX_DIM:(hh + 1) * IDX_DIM], kic, _NT, preferred_element_type=F32)
        term = jnp.maximum(s, 0.0) * wi[:, hh:hh + 1]
        sc = term if sc is None else sc + term
    return sc


def _dsa_prompt_kernel(q_ref, qi_ref, wi_ref, k_ref, v_ref, ki_ref, o_ref,
                       key_ref, qs_ref, m_ref, l_ref, acc_ref, *, kc, topk, idx_bits):
    tq = q_ref.shape[1]
    t0 = pl.program_id(1) * tq
    nck = (t0 + tq + kc - 1) // kc
    _stack_heads(q_ref, qs_ref, tq)
    qpos = t0 + lax.broadcasted_iota(jnp.int32, (tq, 1), 0)
    kiota = lax.broadcasted_iota(jnp.int32, (1, kc), 1)
    qi = qi_ref[0]
    wi = wi_ref[0] * IDX_SCALE

    def score(c, carry):
        c0 = pl.multiple_of(c * kc, kc)
        sc = _index_scores(qi, wi, ki_ref[0, pl.ds(c0, kc), :])
        key_ref[:, pl.ds(c0, kc)] = jnp.where(c0 + kiota <= qpos, _to_key(sc), INT_MIN)
        return carry

    lax.fori_loop(0, nck, score, 0)
    thr = _topk_threshold(key_ref, nck, kc, topk, idx_bits)
    _flash_init(m_ref, l_ref, acc_ref)

    def attend(c, carry):
        c0 = pl.multiple_of(c * kc, kc)
        sel = key_ref[:, pl.ds(c0, kc)] >= thr
        _grouped_flash_step(lambda h: sel, k_ref[0, pl.ds(c0, kc), :], v_ref[0, pl.ds(c0, kc), :],
                            qs_ref, m_ref, l_ref, acc_ref, tq)
        return carry

    lax.fori_loop(0, nck, attend, 0)
    for hd in range(N_HEADS):
        o_ref[0, :, hd * HEAD_DIM:(hd + 1) * HEAD_DIM] = _flash_result(
            slice(hd * tq, (hd + 1) * tq), l_ref, acc_ref).astype(o_ref.dtype)


def _flash_scratch(tq):
    return [pltpu.VMEM((N_HEADS * tq, HEAD_DIM), BF16),
            pltpu.VMEM((N_HEADS * tq, LANES), F32),
            pltpu.VMEM((N_HEADS * tq, LANES), F32),
            pltpu.VMEM((N_HEADS * tq, HEAD_DIM), F32)]


def dsa_prompt_attend(q, qi, wi, k, v, ki, tq=Q_BLOCK, kc=512):
    B, S, D = q.shape
    kc = min(kc, S)
    topk = min(DSA_TOPK, S // 4)
    blk = lambda w: pl.BlockSpec((1, tq, w), lambda b, i: (b, i, 0))
    full = lambda w: pl.BlockSpec((1, S, w), lambda b, i: (b, 0, 0))
    return pl.pallas_call(
        functools.partial(_dsa_prompt_kernel, kc=kc, topk=topk, idx_bits=S.bit_length()),
        out_shape=jax.ShapeDtypeStruct((B, S, D), BF16),
        grid=(B, S // tq),
        in_specs=[blk(D), blk(qi.shape[-1]), blk(wi.shape[-1]), full(KV_WIDTH), full(KV_WIDTH), full(IDX_DIM)],
        out_specs=blk(D),
        scratch_shapes=[pltpu.VMEM((tq, S), jnp.int32)] + _flash_scratch(tq),
        compiler_params=_cparams("parallel", "arbitrary"),
    )(q, qi, wi, k, v, ki)


def _dsa_sample_kernel(pt_ref, q_ref, qi_ref, wi_ref, kvn_ref, kin_ref, ckv_ref, cki_ref, o_ref,
                       key_ref, kv_ref, qs_ref, m_ref, l_ref, acc_ref, *, kc, topk, idx_bits, n_new):
    tq = q_ref.shape[1]
    j = pl.program_id(1)
    n_pages = pl.num_programs(1)
    past = n_pages * PAGE_SIZE
    qi = qi_ref[0].astype(BF16)
    wi = wi_ref[0] * IDX_SCALE
    p0 = pl.multiple_of(j * PAGE_SIZE, PAGE_SIZE)
    kv_ref[pl.ds(p0, PAGE_SIZE), :] = ckv_ref[0, 0].astype(BF16)
    key_ref[:, pl.ds(p0, PAGE_SIZE)] = _to_key(_index_scores(qi, wi, cki_ref[0, 0].astype(BF16)))

    @pl.when(j == n_pages - 1)
    def _():
        nck = (past + PAGE_SIZE) // kc
        kv_ref[past:past + PAGE_SIZE, :] = kvn_ref[0]
        trow = lax.broadcasted_iota(jnp.int32, (tq, PAGE_SIZE), 0)
        ncol = lax.broadcasted_iota(jnp.int32, (tq, PAGE_SIZE), 1)
        visible = (ncol <= trow) & (ncol < n_new)
        key_ref[:, past:past + PAGE_SIZE] = jnp.where(visible, _to_key(_index_scores(qi, wi, kin_ref[0])), INT_MIN)
        _stack_heads(q_ref, qs_ref, tq)
        thr = _topk_threshold(key_ref, nck, kc, topk, idx_bits)
        _flash_init(m_ref, l_ref, acc_ref)

        def attend(c, carry):
            c0 = pl.multiple_of(c * kc, kc)
            sel = key_ref[:, pl.ds(c0, kc)] >= thr
            _grouped_flash_step(lambda h: sel, kv_ref[pl.ds(c0, kc), 0:KV_WIDTH],
                                kv_ref[pl.ds(c0, kc), KV_WIDTH:2 * KV_WIDTH], qs_ref, m_ref, l_ref, acc_ref, tq)
            return carry

        lax.fori_loop(0, nck, attend, 0)
        for hd in range(N_HEADS):
            o_ref[0, :, hd * HEAD_DIM:(hd + 1) * HEAD_DIM] = _flash_result(
                slice(hd * tq, (hd + 1) * tq), l_ref, acc_ref).astype(o_ref.dtype)


def _key_chunk(total):
    n = total // LANES
    return LANES * max(d for d in range(1, 9) if n % d == 0)


def dsa_sample_attend(q, qi, wi, kv_new, ki_new, cache_kv, cache_ki, layer, page_table, n_new):
    DB, tq, D = q.shape
    n_pages = page_table.shape[1]
    past = n_pages * PAGE_SIZE
    total = past + PAGE_SIZE
    kc = _key_chunk(total)
    topk = min(DSA_TOPK, (past + n_new) // 4)
    per_b = lambda r, w: pl.BlockSpec((1, r, w), lambda b, j, pt: (b, 0, 0))
    grid_spec = pltpu.PrefetchScalarGridSpec(
        num_scalar_prefetch=1,
        grid=(DB, n_pages),
        in_specs=[per_b(tq, D), per_b(tq, qi.shape[-1]), per_b(tq, wi.shape[-1]),
                  per_b(PAGE_SIZE, 2 * KV_WIDTH), per_b(PAGE_SIZE, IDX_DIM),
                  pl.BlockSpec((1, 1, PAGE_SIZE, 2 * KV_WIDTH), lambda b, j, pt: (layer, pt[b, j], 0, 0)),
                  pl.BlockSpec((1, 1, PAGE_SIZE, IDX_DIM), lambda b, j, pt: (layer, pt[b, j], 0, 0))],
        out_specs=per_b(tq, D),
        scratch_shapes=[pltpu.VMEM((tq, total), jnp.int32),
                        pltpu.VMEM((total, 2 * KV_WIDTH), BF16)] + _flash_scratch(tq),
    )
    return pl.pallas_call(
        functools.partial(_dsa_sample_kernel, kc=kc, topk=topk, idx_bits=total.bit_length(), n_new=n_new),
        out_shape=jax.ShapeDtypeStruct((DB, tq, D), BF16),
        grid_spec=grid_spec,
        compiler_params=_cparams("parallel", "arbitrary"),
    )(page_table, q, qi, wi, kv_new, ki_new, cache_kv, cache_ki)


def _nsa_compress_kernel(pt_ref, x_ref, pe_ref, w1_ref, w2_ref, o_ref, rows_ref):
    j = pl.program_id(2)
    n_pages = pl.num_programs(2)
    n_slabs = rows_ref.shape[0]
    p0 = pl.multiple_of(j * PAGE_SIZE, PAGE_SIZE)
    for s in range(n_slabs):
        rows_ref[s, pl.ds(p0, PAGE_SIZE), :] = x_ref[0, 0, :, s * LANES:(s + 1) * LANES]

    @pl.when(j == n_pages - 1)
    def _():
        half = rows_ref.shape[1] // (2 * CMP_LEN)
        acc = None
        for l in range(CMP_LEN):
            strided = lambda r0: jnp.concatenate(
                [rows_ref[s, pl.ds(r0, half, stride=2 * CMP_LEN), :] for s in range(n_slabs)], axis=1)
            xl = (jnp.concatenate([strided(l), strided(CMP_LEN + l)], axis=0) + pe_ref[0, l:l + 1, :]).astype(BF16)
            part = jnp.dot(xl, w1_ref[0, l], preferred_element_type=F32)
            acc = part if acc is None else acc + part
        hid = _gelu(acc).astype(BF16)
        o_ref[0] = jnp.dot(hid, w2_ref[0], preferred_element_type=F32).astype(o_ref.dtype)


def nsa_compress(rows, layer, page_table, pe_t, w1_bd, w2_bd):
    NB, n_pages = page_table.shape
    L = n_pages * PAGE_SIZE
    nc = L // CMP_LEN
    grid_spec = pltpu.PrefetchScalarGridSpec(
        num_scalar_prefetch=1,
        grid=(NB, 2, n_pages),
        in_specs=[pl.BlockSpec((1, 1, PAGE_SIZE, KV_WIDTH), lambda b, s, j, pt: (layer, pt[b, j], 0, s)),
                  pl.BlockSpec((1, CMP_LEN, KV_WIDTH), lambda b, s, j, pt: (s, 0, 0)),
                  pl.BlockSpec((1, CMP_LEN, KV_WIDTH, N_KV_HEADS * CMP_HIDDEN), lambda b, s, j, pt: (s, 0, 0, 0)),
                  pl.BlockSpec((1, N_KV_HEADS * CMP_HIDDEN, KV_WIDTH), lambda b, s, j, pt: (s, 0, 0))],
        out_specs=pl.BlockSpec((1, nc, KV_WIDTH), lambda b, s, j, pt: (b, 0, s)),
        scratch_shapes=[pltpu.VMEM((KV_WIDTH // LANES, L, LANES), F32)],
    )
    return pl.pallas_call(
        _nsa_compress_kernel,
        out_shape=jax.ShapeDtypeStruct((NB, nc, 2 * KV_WIDTH), BF16),
        grid_spec=grid_spec,
        compiler_params=_cparams("parallel", "arbitrary", "arbitrary"),
    )(page_table, rows, pe_t, w1_bd, w2_bd)


def nsa_compress_weights(cmp_pe, cmp_w1, cmp_w2):
    eye = jnp.eye(N_KV_HEADS, dtype=F32)
    w1 = cmp_w1.reshape(2, CMP_LEN, HEAD_DIM, CMP_HIDDEN)
    w1_bd = jnp.einsum('hg,kldj->klhdgj', eye, w1).reshape(2, CMP_LEN, KV_WIDTH, N_KV_HEADS * CMP_HIDDEN)
    w2_bd = jnp.einsum('hg,kjd->khjgd', eye, cmp_w2).reshape(2, N_KV_HEADS * CMP_HIDDEN, KV_WIDTH)
    pe_t = jnp.tile(cmp_pe, (1, 1, N_KV_HEADS))
    return pe_t, w1_bd.astype(BF16), w2_bd.astype(BF16)


def _nsa_compressed_branch(qs_ref, kvc, qpos, gate_of, out_ref, tq):
    nc = kvc.shape[0]
    half = nc // 2
    pcol = lax.broadcasted_iota(jnp.int32, (1, nc), 1)
    cidx = jnp.where(pcol < half, 2 * pcol, 2 * (pcol - half) + 1)
    visible = (cidx + 1) * CMP_LEN - 1 <= qpos
    imps = []
    for h in range(N_KV_HEADS):
        cols = slice(h * HEAD_DIM, (h + 1) * HEAD_DIM)
        logits = lax.dot_general(qs_ref[h * GROUP * tq:(h + 1) * GROUP * tq, :], kvc[:, cols], _NT,
                                 preferred_element_type=F32)
        vcols = slice(KV_WIDTH + h * HEAD_DIM, KV_WIDTH + (h + 1) * HEAD_DIM)
        imp = None
        for g in range(GROUP):
            hd = h * GROUP + g
            s = jnp.where(visible, logits[g * tq:(g + 1) * tq], NEG_INF)
            m = jnp.max(s, axis=1, keepdims=True)
            e = jnp.exp(s - jnp.where(m == NEG_INF, 0.0, m))
            pc = e / jnp.maximum(jnp.sum(e, axis=1, keepdims=True), 1e-30)
            o = jnp.dot(pc.astype(BF16), kvc[:, vcols], preferred_element_type=F32)
            out_ref[hd * tq:(hd + 1) * tq, :] = gate_of(hd, 0) * o
            pair = pc[:, :half] + pc[:, half:]
            imp = pair if imp is None else imp + pair
        imps.append(imp)
    return imps


def _select_blocks(imp, cur, n_blocks, n_pick, axis):
    blk = lax.broadcasted_iota(jnp.int32, imp.shape, axis)
    forced = (blk == 0) | (blk == cur) | (blk == cur - 1)
    x = jnp.where(forced, 16.0, imp)
    x = jnp.where(blk <= cur, x, -1.0)
    x = jnp.where(blk < n_blocks, x, -2.0)
    blk_f = blk.astype(F32)
    sel = jnp.zeros(imp.shape, F32)
    for _ in range(n_pick):
        m = jnp.max(x, axis=axis, keepdims=True)
        first = jnp.min(jnp.where(x == m, blk_f, float(imp.shape[axis])), axis=axis, keepdims=True)
        hit = blk_f == first
        sel = jnp.where(hit, 1.0, sel)
        x = jnp.where(hit, -2.0, x)
    return sel


def _nsa_finish_branch(br, gate_of, out_ref, l_ref, acc_ref, tq):
    for hd in range(N_HEADS):
        rows = slice(hd * tq, (hd + 1) * tq)
        out_ref[rows, :] = out_ref[rows, :] + gate_of(hd, br) * _flash_result(rows, l_ref, acc_ref)


def _nsa_selected_branch(sel_ref, kv_chunk, nck, kc, qpos, qs_ref, m_ref, l_ref, acc_ref, tq):
    ns_pad = sel_ref.shape[2]
    _flash_init(m_ref, l_ref, acc_ref)
    brow = lax.broadcasted_iota(jnp.int32, (ns_pad, kc), 0)
    kcol = lax.broadcasted_iota(jnp.int32, (ns_pad, kc), 1)
    kiota = lax.broadcasted_iota(jnp.int32, (1, kc), 1)

    def attend(c, carry):
        c0 = pl.multiple_of(c * kc, kc)
        expand = jnp.where((c0 + kcol) // SEL_BLOCK == brow, 1.0, 0.0).astype(BF16)
        causal = c0 + kiota <= qpos
        kch, vch = kv_chunk(c0)

        def mask_of(h):
            picked = jnp.dot(sel_ref[h], expand, preferred_element_type=F32)
            return jnp.where(causal, picked, 0.0) > 0.5

        _grouped_flash_step(mask_of, kch, vch, qs_ref, m_ref, l_ref, acc_ref, tq)
        return carry

    lax.fori_loop(0, nck, attend, 0)


def _nsa_window_branch(kw, vw, kwpos, qpos, qs_ref, m_ref, l_ref, acc_ref, tq):
    _flash_init(m_ref, l_ref, acc_ref)
    dist = qpos - kwpos
    wmask = (dist >= 0) & (dist < WINDOW) & (kwpos >= 0)
    _grouped_flash_step(lambda h: wmask, kw, vw, qs_ref, m_ref, l_ref, acc_ref, tq)


def _gate_fn(gates_ref):
    sig = jax.nn.sigmoid(gates_ref[0])
    return lambda hd, br: sig[:, hd * 3 + br:hd * 3 + br + 1]


def _nsa_write_out(o_ref, out_ref, tq):
    for hd in range(N_HEADS):
        o_ref[0, :, hd * HEAD_DIM:(hd + 1) * HEAD_DIM] = out_ref[hd * tq:(hd + 1) * tq, :].astype(o_ref.dtype)


def _nsa_prompt_kernel(q_ref, gates_ref, kvc_ref, sw_ref, o_ref,
                       sel_ref, out_ref, qs_ref, m_ref, l_ref, acc_ref, *, kc, wlen):
    tq = q_ref.shape[1]
    S = sw_ref.shape[1]
    t0 = pl.program_id(1) * tq
    _stack_heads(q_ref, qs_ref, tq)
    qpos = t0 + lax.broadcasted_iota(jnp.int32, (tq, 1), 0)
    qpos_row = t0 + lax.broadcasted_iota(jnp.int32, (1, tq), 1)
    gate_of = _gate_fn(gates_ref)
    imps = _nsa_compressed_branch(qs_ref, kvc_ref[0], qpos, gate_of, out_ref, tq)
    n_blocks = S // SEL_BLOCK
    for h in range(N_KV_HEADS):
        sel_t = _select_blocks(imps[h].T, qpos_row // SEL_BLOCK, n_blocks, min(N_SEL, n_blocks), 0)
        sel_ref[h] = sel_t.T.astype(BF16)
    nck = (t0 + tq + kc - 1) // kc
    kv_chunk = lambda c0: (sw_ref[0, pl.ds(c0, kc), 0:KV_WIDTH], sw_ref[0, pl.ds(c0, kc), KV_WIDTH:2 * KV_WIDTH])
    _nsa_selected_branch(sel_ref, kv_chunk, nck, kc, qpos, qs_ref, m_ref, l_ref, acc_ref, tq)
    _nsa_finish_branch(1, gate_of, out_ref, l_ref, acc_ref, tq)
    start = pl.multiple_of(jnp.clip(t0 + tq - wlen, 0, S - wlen), tq)
    kwpos = start + lax.broadcasted_iota(jnp.int32, (1, wlen), 1)
    _nsa_window_branch(sw_ref[0, pl.ds(start, wlen), 2 * KV_WIDTH:3 * KV_WIDTH],
                       sw_ref[0, pl.ds(start, wlen), 3 * KV_WIDTH:4 * KV_WIDTH],
                       kwpos, qpos, qs_ref, m_ref, l_ref, acc_ref, tq)
    _nsa_finish_branch(2, gate_of, out_ref, l_ref, acc_ref, tq)
    _nsa_write_out(o_ref, out_ref, tq)


def nsa_prompt_attend(q, gates, kvc, selwin, tq=Q_BLOCK, kc=512):
    B, S, D = q.shape
    kc = min(kc, S)
    wlen = min(WINDOW + tq, S)
    n_blocks = S // SEL_BLOCK
    blk = lambda w: pl.BlockSpec((1, tq, w), lambda b, i: (b, i, 0))
    return pl.pallas_call(
        functools.partial(_nsa_prompt_kernel, kc=kc, wlen=wlen),
        out_shape=jax.ShapeDtypeStruct((B, S, D), BF16),
        grid=(B, S // tq),
        in_specs=[blk(D), blk(gates.shape[-1]),
                  pl.BlockSpec((1,) + kvc.shape[1:], lambda b, i: (b, 0, 0)),
                  pl.BlockSpec((1, S, selwin.shape[-1]), lambda b, i: (b, 0, 0), pipeline_mode=pl.Buffered(1))],
        out_specs=blk(D),
        scratch_shapes=[pltpu.VMEM((N_KV_HEADS, tq, n_blocks), BF16),
                        pltpu.VMEM((N_HEADS * tq, HEAD_DIM), F32)] + _flash_scratch(tq),
        compiler_params=_cparams("parallel", "arbitrary"),
    )(q, gates, kvc, selwin)


def _nsa_sample_kernel(pt_ref, q_ref, gates_ref, kvc_ref, kvn_ref, win_ref, winn_ref, csel_ref, o_ref,
                       kv_ref, wkv_ref, sel_ref, out_ref, qs_ref, m_ref, l_ref, acc_ref, *, kc, n_new):
    tq = q_ref.shape[1]
    j = pl.program_id(1)
    n_pages = pl.num_programs(1)
    past = n_pages * PAGE_SIZE
    kv_ref[pl.ds(pl.multiple_of(j * PAGE_SIZE, PAGE_SIZE), PAGE_SIZE), :] = csel_ref[0, 0].astype(BF16)

    @pl.when(j == n_pages - 1)
    def _():
        kv_ref[past:past + PAGE_SIZE, :] = kvn_ref[0]
        wb = win_ref.shape[1]
        wkv_ref[0:wb, :] = win_ref[0].astype(BF16)
        wkv_ref[wb:wb + PAGE_SIZE, :] = winn_ref[0]
        _stack_heads(q_ref, qs_ref, tq)
        qpos = past + jnp.minimum(lax.broadcasted_iota(jnp.int32, (tq, 1), 0), n_new - 1)
        gate_of = _gate_fn(gates_ref)
        imps = _nsa_compressed_branch(qs_ref, kvc_ref[0], qpos, gate_of, out_ref, tq)
        n_blocks = -(-(past + n_new) // SEL_BLOCK)
        ns_pad = sel_ref.shape[2]
        for h in range(N_KV_HEADS):
            imp = imps[h]
            imp = jnp.concatenate([imp, jnp.zeros((tq, ns_pad - imp.shape[1]), F32)], axis=1)
            sel_ref[h] = _select_blocks(imp, qpos // SEL_BLOCK, n_blocks, min(N_SEL, n_blocks), 1).astype(BF16)
        nck = (past + PAGE_SIZE) // kc
        kv_chunk = lambda c0: (kv_ref[pl.ds(c0, kc), 0:KV_WIDTH], kv_ref[pl.ds(c0, kc), KV_WIDTH:2 * KV_WIDTH])
        _nsa_selected_branch(sel_ref, kv_chunk, nck, kc, qpos, qs_ref, m_ref, l_ref, acc_ref, tq)
        _nsa_finish_branch(1, gate_of, out_ref, l_ref, acc_ref, tq)
        kwpos = past - wb + lax.broadcasted_iota(jnp.int32, (1, wb + PAGE_SIZE), 1)
        _nsa_window_branch(wkv_ref[:, 0:KV_WIDTH], wkv_ref[:, KV_WIDTH:2 * KV_WIDTH], kwpos, qpos,
                           qs_ref, m_ref, l_ref, acc_ref, tq)
        _nsa_finish_branch(2, gate_of, out_ref, l_ref, acc_ref, tq)
        _nsa_write_out(o_ref, out_ref, tq)


def nsa_sample_attend(q, gates, kvc, sel_new, win_buf, win_new, cache_kv, layer, page_table, n_new):
    DB, tq, D = q.shape
    n_pages = page_table.shape[1]
    past = n_pages * PAGE_SIZE
    total = past + PAGE_SIZE
    kc = _key_chunk(total)
    wb = win_buf.shape[1]
    ns_pad = -(-(total // SEL_BLOCK) // LANES) * LANES
    per_b = lambda r, w: pl.BlockSpec((1, r, w), lambda b, j, pt: (b, 0, 0))
    grid_spec = pltpu.PrefetchScalarGridSpec(
        num_scalar_prefetch=1,
        grid=(DB, n_pages),
        in_specs=[per_b(tq, D), per_b(tq, gates.shape[-1]), per_b(kvc.shape[1], kvc.shape[2]),
                  per_b(PAGE_SIZE, 2 * KV_WIDTH), per_b(wb, 2 * KV_WIDTH), per_b(PAGE_SIZE, 2 * KV_WIDTH),
                  pl.BlockSpec((1, 1, PAGE_SIZE, 2 * KV_WIDTH), lambda b, j, pt: (layer, pt[b, j], 0, 1))],
        out_specs=per_b(tq, D),
        scratch_shapes=[pltpu.VMEM((total, 2 * KV_WIDTH), BF16),
                        pltpu.VMEM((wb + PAGE_SIZE, 2 * KV_WIDTH), BF16),
                        pltpu.VMEM((N_KV_HEADS, tq, ns_pad), BF16),
                        pltpu.VMEM((N_HEADS * tq, HEAD_DIM), F32)] + _flash_scratch(tq),
    )
    return pl.pallas_call(
        functools.partial(_nsa_sample_kernel, kc=kc, n_new=n_new),
        out_shape=jax.ShapeDtypeStruct((DB, tq, D), BF16),
        grid_spec=grid_spec,
        compiler_params=_cparams("parallel", "arbitrary"),
    )(page_table, q, gates, kvc, sel_new, win_buf, win_new, cache_kv)


HALO = 8
G_IG, G_LF, G_CUM = 0, MLSTM_HEADS, 2 * MLSTM_HEADS
MASKED_GATE = -1e30


def _split3(x):
    hi = x.astype(BF16)
    r1 = x - hi.astype(F32)
    mid = r1.astype(BF16)
    lo = (r1 - mid.astype(F32)).astype(BF16)
    return hi, mid, lo


def _mlstm_pre_kernel(xm_ref, halo_ref, cw_ref, cb_ref, wq_ref, wk_ref, wv_ref, wvt_ref, wg_ref, bg_ref,
                      q_ref, k_ref, v_ref, vt_ref, xc_ref, g_ref, gt_ref, xp_ref, *, chunk, n_valid):
    tm = xm_ref.shape[1]
    hd = MLSTM_HEAD_DIM
    xp_ref[0:HALO, :] = halo_ref[0, 0]
    xp_ref[HALO:HALO + tm, :] = xm_ref[0]
    y = cb_ref[...]
    for j in range(CONV_WIDTH):
        off = HALO - (CONV_WIDTH - 1) + j
        y = y + cw_ref[j:j + 1, :] * xp_ref[off:off + tm, :]
    xc = y * jax.nn.sigmoid(y)
    xc_ref[0] = xc
    xcb = xc.astype(BF16)
    xmb = xm_ref[0].astype(BF16)
    g = bg_ref[...]
    for h in range(MLSTM_HEADS):
        cols = slice(h * hd, (h + 1) * hd)
        q = jnp.dot(xcb[:, cols], wq_ref[h], preferred_element_type=F32).astype(BF16)
        k = (jnp.dot(xcb[:, cols], wk_ref[h], preferred_element_type=F32) * hd ** -0.5).astype(BF16)
        v = jnp.dot(xmb[:, cols], wv_ref[h], preferred_element_type=F32).astype(BF16)
        q_ref[0, :, cols] = q
        k_ref[0, :, cols] = k
        v_ref[0, :, cols] = v
        vt_ref[0, cols, :] = lax.dot_general(wvt_ref[h], xmb[:, cols], _NT, preferred_element_type=F32).astype(BF16)
        for i, a in enumerate((q, k, v)):
            g = g + jnp.dot(a, wg_ref[i * MLSTM_INNER + h * hd:i * MLSTM_INNER + (h + 1) * hd, :],
                            preferred_element_type=F32)
    col = lax.broadcasted_iota(jnp.int32, (tm, LANES), 1)
    row = pl.program_id(1) * tm + lax.broadcasted_iota(jnp.int32, (tm, LANES), 0)
    log_f = jnp.minimum(g, 0.0) - jnp.log1p(jnp.exp(-jnp.abs(g)))
    g = jnp.where(col < G_LF, g, log_f)
    live = row < n_valid
    g = jnp.where(live, g, jnp.where(col < G_LF, MASKED_GATE, 0.0))
    tri = (lax.broadcasted_iota(jnp.int32, (chunk, chunk), 0) >= lax.broadcasted_iota(jnp.int32, (chunk, chunk), 1))
    tri = jnp.where(tri, 1.0, 0.0).astype(BF16)
    for c in range(tm // chunk):
        rows = slice(c * chunk, (c + 1) * chunk)
        cum = None
        for piece in _split3(g[rows]):
            part = jnp.dot(tri, piece, preferred_element_type=F32)
            cum = part if cum is None else cum + part
        out = jnp.where(col[rows] < G_CUM, g[rows], pltpu.roll(cum, G_CUM - G_LF, 1))
        g_ref[0, rows, :] = out
        gt_ref[0, :, rows] = out.T[0:gt_ref.shape[1], :]


def mlstm_pre(xm, halo, conv_w, conv_b, wq, wk, wv, wvt, wg, bg, tm, chunk, n_valid):
    NB, S, W = xm.shape
    hd = MLSTM_HEAD_DIM
    const = lambda a: pl.BlockSpec(a.shape, lambda b, i: (0,) * a.ndim)
    row_blk = lambda w: pl.BlockSpec((1, tm, w), lambda b, i: (b, i, 0))
    sds = jax.ShapeDtypeStruct
    return pl.pallas_call(
        functools.partial(_mlstm_pre_kernel, chunk=chunk, n_valid=n_valid),
        out_shape=[sds((NB, S, W), BF16)] * 3 + [sds((NB, W, S), BF16), sds((NB, S, W), F32),
                                                 sds((NB, S, LANES), F32), sds((NB, 2 * HALO, S), F32)],
        grid=(NB, S // tm),
        in_specs=[row_blk(W), pl.BlockSpec((1, 1, HALO, W), lambda b, i: (b, i, 0, 0)),
                  const(conv_w), const(conv_b), const(wq), const(wk), const(wv), const(wvt), const(wg), const(bg)],
        out_specs=[row_blk(W)] * 3 + [pl.BlockSpec((1, W, tm), lambda b, i: (b, 0, i)), row_blk(W), row_blk(LANES),
                                      pl.BlockSpec((1, 2 * HALO, tm), lambda b, i: (b, 0, i))],
        scratch_shapes=[pltpu.VMEM((HALO + tm, W), F32)],
        compiler_params=_cparams("parallel", "parallel"),
    )(xm, halo, conv_w, conv_b, wq, wk, wv, wvt, wg, bg)


def _mlstm_scan_kernel(q_ref, k_ref, v_ref, vt_ref, g_ref, gt_ref, xc_ref, z_ref, ng_ref, sk_ref, c0_ref, n0_ref, m0_ref,
                       o_ref, c_out, n_out, m_out, c_ref, n_ref, m_ref):
    L = q_ref.shape[1]
    h = pl.program_id(1)
    ci = pl.program_id(2)

    @pl.when(ci == 0)
    def _():
        c_ref[...] = c0_ref[0, 0]
        n_ref[...] = n0_ref[0, 0]
        m_ref[...] = m0_ref[0, 0]

    q, k, v = q_ref[0], k_ref[0], v_ref[0]
    col = lax.broadcasted_iota(jnp.int32, (L, LANES), 1)
    b_col = jnp.sum(jnp.where(col == G_CUM + h, g_ref[0], 0.0), axis=1, keepdims=True)
    i_row = gt_ref[0, pl.ds(G_IG + h, 1), :]
    b_row = gt_ref[0, pl.ds(G_CUM + h, 1), :]
    m = m_ref[0:1, 0:1]
    inter = b_col + m
    dmat = b_col - b_row + i_row
    tril = lax.broadcasted_iota(jnp.int32, (L, L), 0) >= lax.broadcasted_iota(jnp.int32, (L, L), 1)
    dmat = jnp.where(tril, dmat, NEG_INF)
    m_loc = jnp.maximum(inter, jnp.max(dmat, axis=1, keepdims=True))
    a = lax.dot_general(q, k, _NT, preferred_element_type=F32) * jnp.exp(dmat - m_loc)
    w_inter = jnp.exp(inter - m_loc)
    c_prev = c_ref[...]
    n_prev = n_ref[...]
    num = (jnp.dot(a.astype(BF16), v, preferred_element_type=F32)
           + w_inter * lax.dot_general(q, c_prev.astype(BF16), _NT, preferred_element_type=F32))
    qn = lax.dot_general(q, n_prev.astype(BF16), _NT, preferred_element_type=F32)[:, 0:1]
    den = jnp.sum(a, axis=1, keepdims=True) + w_inter * qn
    hc = num / jnp.maximum(jnp.abs(den), jnp.exp(-m_loc))
    b_end = b_row[:, L - 1:L]
    lg = b_end - b_row + i_row
    m_new = jnp.maximum(b_end + m, jnp.max(lg, axis=1, keepdims=True))
    wg = jnp.exp(lg - m_new)
    decay = jnp.exp(b_end + m - m_new)
    c_ref[...] = decay * c_prev + jnp.dot((vt_ref[0] * wg).astype(BF16), k, preferred_element_type=F32)
    n_ref[...] = decay * n_prev + jnp.dot(jnp.broadcast_to(wg, (n_ref.shape[0], L)).astype(BF16), k,
                                          preferred_element_type=F32)
    m_ref[...] = jnp.broadcast_to(m_new, m_ref.shape)
    mu = jnp.mean(hc, axis=1, keepdims=True)
    hz = hc - mu
    var = jnp.mean(hz * hz, axis=1, keepdims=True)
    hn = hz * lax.rsqrt(var + LN_EPS) * ng_ref[...]
    z = z_ref[0]
    o_ref[0] = ((hn + sk_ref[...] * xc_ref[0]) * (z * jax.nn.sigmoid(z))).astype(o_ref.dtype)

    @pl.when(ci == pl.num_programs(2) - 1)
    def _():
        c_out[0, 0] = c_ref[...]
        n_out[0, 0] = n_ref[...]
        m_out[0, 0] = m_ref[...]


def mlstm_scan(q, k, v, vt, g, gt, xc, z, norm_g, skip, c0, n0, m0, chunk):
    NB, S, W = q.shape
    hd = MLSTM_HEAD_DIM
    H = W // hd
    seq = lambda: pl.BlockSpec((1, chunk, hd), lambda b, h, c: (b, c, h))
    vec = pl.BlockSpec((1, hd), lambda b, h, c: (0, h))
    st = lambda r, w: pl.BlockSpec((1, 1, r, w), lambda b, h, c: (b, h, 0, 0))
    sds = jax.ShapeDtypeStruct
    return pl.pallas_call(
        _mlstm_scan_kernel,
        out_shape=[sds((NB, S, W), BF16), sds((NB, H, hd, hd), F32), sds((NB, H, HALO, hd), F32), sds((NB, H, HALO, LANES), F32)],
        grid=(NB, H, S // chunk),
        in_specs=[seq(), seq(), seq(), pl.BlockSpec((1, hd, chunk), lambda b, h, c: (b, h, c)),
                  pl.BlockSpec((1, chunk, LANES), lambda b, h, c: (b, c, 0)),
                  pl.BlockSpec((1, 2 * HALO, chunk), lambda b, h, c: (b, 0, c)),
                  seq(), seq(), vec, vec, st(hd, hd), st(HALO, hd), st(HALO, LANES)],
        out_specs=[seq(), st(hd, hd), st(HALO, hd), st(HALO, LANES)],
        scratch_shapes=[pltpu.VMEM((hd, hd), F32), pltpu.VMEM((HALO, hd), F32), pltpu.VMEM((HALO, LANES), F32)],
        compiler_params=_cparams("parallel", "parallel", "arbitrary"),
    )(q, k, v, vt, g, gt, xc, z, norm_g.reshape(1, W), skip.reshape(1, W), c0, n0, m0)


SAMPLE_Q_ROWS = 16
TM_PROMPT = 512
TM_PEER = 256
TM_MLSTM = 256
CHUNK_MLSTM = 256


def _pad_rows(a, rows):
    return jnp.pad(a, ((0, 0), (0, rows - a.shape[1]), (0, 0)))


def _pad_cols(a, cols):
    return jnp.pad(a, ((0, 0),) * (a.ndim - 1) + ((0, cols - a.shape[-1]),))


def _dsa_layer(xp, xs, mp, ms, w_in, w_o, ln_g, ln_b, cache_kv, cache_ki, layer, page_table, DB, T):
    B, S, D = xp.shape
    wb = w_in.astype(BF16)
    o1, o2, o3, o4 = D, D + KV_WIDTH, D + 2 * KV_WIDTH, D + 2 * KV_WIDTH + IDX_HEADS * IDX_DIM
    weights = [wb[:, :o1], wb[:, o1:o3], wb[:, o1:o2], wb[:, o2:o3], wb[:, o3:o4], wb[:, o4:o4 + IDX_DIM],
               _pad_cols(wb[:, o4 + IDX_DIM:], LANES)]
    outs = [(0, BF16), (1, F32), (2, BF16), (3, BF16), (4, BF16), (5, F32), (5, BF16), (6, F32)]
    w_ob = w_o.astype(BF16)
    q, kv32, kb, vb, qi, ki32, kib, wi = proj(xp, mp(0, 0), mp(0, 1), weights, outs, TM_PROMPT)
    o = dsa_prompt_attend(q, qi, wi, kb, vb, kib)
    xp = outproj_postnorm(o, w_ob, xp, mp(0, 2), ln_g, ln_b, TM_PROMPT)
    kv_p = kv32.reshape(B, S, 2, N_KV_HEADS, HEAD_DIM)
    n = DB * T
    q, kv32s, kb, vb, qi, ki32s, kib, wi = proj(xs, ms(0, 0), ms(0, 1), weights, outs, n)
    per_b = lambda a: a.reshape(DB, T, a.shape[-1])
    kv_new = _pad_rows(jnp.concatenate([per_b(kb), per_b(vb)], axis=-1), PAGE_SIZE)
    o = dsa_sample_attend(_pad_rows(per_b(q), SAMPLE_Q_ROWS), _pad_rows(per_b(qi), SAMPLE_Q_ROWS),
                          _pad_rows(per_b(wi), SAMPLE_Q_ROWS), kv_new, _pad_rows(per_b(kib), PAGE_SIZE),
                          cache_kv.reshape(cache_kv.shape[:3] + (2 * KV_WIDTH,)), cache_ki, layer, page_table, T)
    xs = outproj_postnorm(o[:, :T].reshape(1, n, D), w_ob, xs, ms(0, 2), ln_g, ln_b, n)
    kv_s = kv32s.reshape(DB, T, 2, N_KV_HEADS, HEAD_DIM)
    return xp, xs, kv_p, kv_s, ki32, ki32s.reshape(DB, T, IDX_DIM)


def _nsa_layer(xp, xs, mp, ms, w_in, cmp_pe, cmp_w1, cmp_w2, w_o, ln_g, ln_b, cache_kv, win_state, layer,
               page_table, DB, T):
    B, S, D = xp.shape
    wb = w_in.astype(BF16)
    c1, c2, c3 = D + 4 * KV_WIDTH, D + 6 * KV_WIDTH, D + 2 * KV_WIDTH
    weights = [wb[:, :D], wb[:, D:c1], wb[:, c1:c2], wb[:, c3:c2], _pad_cols(wb[:, c2:], LANES)]
    outs = [(0, BF16), (1, F32), (2, F32), (3, BF16), (4, F32)]
    w_ob = w_o.astype(BF16)
    cmp_w = nsa_compress_weights(cmp_pe, cmp_w1, cmp_w2)
    q, cs32, win32, selwin, gates = proj(xp, mp(0, 0), mp(0, 1), weights, outs, TM_PROMPT)
    pages = S // PAGE_SIZE
    ident = jnp.arange(B * pages, dtype=jnp.int32).reshape(B, pages)
    kvc = nsa_compress(cs32.reshape(1, B * pages, PAGE_SIZE, 4 * KV_WIDTH), 0, ident, *cmp_w)
    o = nsa_prompt_attend(q, gates, kvc, selwin)
    xp = outproj_postnorm(o, w_ob, xp, mp(0, 2), ln_g, ln_b, TM_PROMPT)
    kv_p = cs32.reshape(B, S, 2, 2, N_KV_HEADS, HEAD_DIM)
    keep = min(WINDOW, S)
    win_p = win32[:, S - keep:].reshape(B, keep, 2, N_KV_HEADS, HEAD_DIM)
    n = DB * T
    q, cs32s, win32s, selwin, gates = proj(xs, ms(0, 0), ms(0, 1), weights, outs, n)
    per_b = lambda a: a.reshape(DB, T, a.shape[-1])
    kvc = nsa_compress(cache_kv.reshape(cache_kv.shape[:3] + (4 * KV_WIDTH,)), layer, page_table, *cmp_w)
    selwin = per_b(selwin)
    wbuf = win_state[layer]
    o = nsa_sample_attend(_pad_rows(per_b(q), SAMPLE_Q_ROWS), _pad_rows(per_b(gates), SAMPLE_Q_ROWS), kvc,
                          _pad_rows(selwin[..., :2 * KV_WIDTH], PAGE_SIZE), wbuf.reshape(DB, wbuf.shape[1], 2 * KV_WIDTH),
                          _pad_rows(selwin[..., 2 * KV_WIDTH:], PAGE_SIZE),
                          cache_kv.reshape(cache_kv.shape[:3] + (4 * KV_WIDTH,)), layer, page_table, T)
    xs = outproj_postnorm(o[:, :T].reshape(1, n, D), w_ob, xs, ms(0, 2), ln_g, ln_b, n)
    kv_s = cs32s.reshape(DB, T, 2, 2, N_KV_HEADS, HEAD_DIM)
    win_s = jnp.concatenate([wbuf, win32s.reshape(DB, T, 2, N_KV_HEADS, HEAD_DIM)], axis=1)[:, T:]
    return xp, xs, kv_p, kv_s, win_p, win_s


def _mlstm_layer(xp, xs, mp, ms, w_in, conv_w, conv_b, w_qkv, w_gate, b_gate, norm_g, skip, w_o, ln_g, ln_b,
                 conv_state, c_state, n_state, m_state, DB, T):
    B, S, D = xp.shape
    W, H, hd = MLSTM_INNER, MLSTM_HEADS, MLSTM_HEAD_DIM
    wb = w_in.astype(BF16)
    weights = [wb[:, :W], wb[:, W:]]
    outs = [(0, F32), (1, F32)]
    wq, wk, wv = (w_qkv[i].astype(BF16) for i in range(3))
    pre_w = (conv_w, conv_b.reshape(1, W), wq, wk, wv, jnp.swapaxes(wv, 1, 2), _pad_cols(w_gate, LANES).astype(BF16),
             _pad_cols(b_gate.reshape(1, -1), LANES))
    w_ob = w_o.astype(BF16)
    keep = CONV_WIDTH - 1
    rep = lambda a, r: jnp.broadcast_to(a[..., None, :], a.shape[:-1] + (r, a.shape[-1]))
    xm, z = proj(xp, mp(0, 0), mp(0, 1), weights, outs, TM_PROMPT)
    tm = min(TM_MLSTM, S)
    tiles = xm.reshape(B, S // tm, tm, W)
    halo = jnp.concatenate([jnp.zeros((B, 1, HALO, W), F32), tiles[:, :-1, tm - HALO:]], axis=1)
    chunk = min(CHUNK_MLSTM, S)
    q, k, v, vt, xc, g, gt = mlstm_pre(xm, halo, *pre_w, tm=tm, chunk=chunk, n_valid=S)
    zeros = lambda *s: jnp.zeros(s, F32)
    o, c_p, n_p, m_p = mlstm_scan(q, k, v, vt, g, gt, xc, z, norm_g, skip, zeros(B, H, hd, hd), zeros(B, H, HALO, hd),
                                  zeros(B, H, HALO, LANES), chunk)
    xp = outproj_postnorm(o, w_ob, xp, mp(0, 2), ln_g, ln_b, TM_PROMPT)
    conv_p = jnp.concatenate([zeros(B, keep, W), xm], axis=1)[:, -keep:]
    n = DB * T
    xm_s, z_s = proj(xs, ms(0, 0), ms(0, 1), weights, outs, n)
    xm_s = xm_s.reshape(DB, T, W)
    rows = LANES
    halo = jnp.concatenate([zeros(DB, HALO - keep, W), conv_state], axis=1)[:, None]
    q, k, v, vt, xc, g, gt = mlstm_pre(_pad_rows(xm_s, rows), halo, *pre_w, tm=rows, chunk=rows, n_valid=T)
    m0 = jnp.broadcast_to(m_state[..., None, None], (DB, H, HALO, LANES))
    o, c_s, n_s, m_s = mlstm_scan(q, k, v, vt, g, gt, xc, _pad_rows(z_s.reshape(DB, T, W), rows), norm_g, skip,
                                  c_state, rep(n_state, HALO), m0, rows)
    xs = outproj_postnorm(o[:, :T].reshape(1, n, W), w_ob, xs, ms(0, 2), ln_g, ln_b, n)
    conv_s = jnp.concatenate([conv_state, xm_s], axis=1)[:, -keep:]
    return (xp, xs, conv_p, conv_s, c_p, c_s, n_p[:, :, 0], n_s[:, :, 0], m_p[:, :, 0, 0], m_s[:, :, 0, 0])


def kernel(x_prompt, x_sample, cache_a_kv, cache_a_kidx, cache_b_kv, state_b_win, state_c_conv, state_c_C, state_c_n,
           state_c_m, page_table, c_prompt, c_sample, a_w_in, a_w_o, b_w_in, b_cmp_pe, b_cmp_w1, b_cmp_w2, b_w_o,
           c_w_in, c_conv_w, c_conv_b, c_w_qkv, c_w_gate, c_b_gate, c_norm_g, c_skip, c_w_o,
           ada_w, ada_b, ln_g, ln_b, peer_w_q, peer_sub_keys, peer_u, peer_v):
    B, S, D = x_prompt.shape
    DB, T, _ = x_sample.shape
    n = DB * T
    n_cond = B + DB
    cond = _pad_rows(jnp.concatenate([c_prompt, c_sample], axis=0)[None], -(-n_cond // 8) * 8)[0]
    mods = adaln_all(cond, ada_w, ada_b).reshape(DEPTH, cond.shape[0], 2, 3, D)
    xp, xs = x_prompt, x_sample.reshape(1, n, D)
    outs = {name: [] for name in ("a_kv_p", "a_kv_s", "a_ki_p", "a_ki_s", "b_kv_p", "b_kv_s", "b_win_p", "b_win_s",
                                  "conv_p", "conv_s", "C_p", "C_s", "n_p", "n_s", "m_p", "m_s")}
    for i in range(DEPTH):
        kind, j = i % N_MIXERS, i // N_MIXERS
        mod_p = mods[i, :B]
        mod_s = jnp.repeat(mods[i, B:n_cond], T, axis=0)
        mp = lambda s, r, mod_p=mod_p: mod_p[:, s, r][:, None, :]
        ms = lambda s, r, mod_s=mod_s: mod_s[:, s, r][None]
        if kind == 0:
            xp, xs, kvp, kvs, kip, kis = _dsa_layer(xp, xs, mp, ms, a_w_in[j], a_w_o[j], ln_g[i, 0], ln_b[i, 0],
                                                    cache_a_kv, cache_a_kidx, j, page_table, DB, T)
            for name, val in zip(("a_kv_p", "a_kv_s", "a_ki_p", "a_ki_s"), (kvp, kvs, kip, kis)):
                outs[name].append(val)
        elif kind == 1:
            xp, xs, kvp, kvs, wp, ws = _nsa_layer(xp, xs, mp, ms, b_w_in[j], b_cmp_pe[j], b_cmp_w1[j], b_cmp_w2[j],
                                                  b_w_o[j], ln_g[i, 0], ln_b[i, 0], cache_b_kv, state_b_win, j,
                                                  page_table, DB, T)
            for name, val in zip(("b_kv_p", "b_kv_s", "b_win_p", "b_win_s"), (kvp, kvs, wp, ws)):
                outs[name].append(val)
        else:
            res = _mlstm_layer(xp, xs, mp, ms, c_w_in[j], c_conv_w[j], c_conv_b[j], c_w_qkv[j], c_w_gate[j], c_b_gate[j],
                               c_norm_g[j], c_skip[j], c_w_o[j], ln_g[i, 0], ln_b[i, 0],
                               state_c_conv[j], state_c_C[j], state_c_n[j], state_c_m[j], DB, T)
            xp, xs = res[0], res[1]
            for name, val in zip(("conv_p", "conv_s", "C_p", "C_s", "n_p", "n_s", "m_p", "m_s"), res[2:]):
                outs[name].append(val)
        peer_w = (peer_w_q[i].astype(BF16), peer_sub_keys[i].astype(BF16), peer_u[i].astype(BF16),
                  peer_v[i].astype(BF16), ln_g[i, 1], ln_b[i, 1])
        xp = peer_layer(xp, mp(1, 0), mp(1, 1), mp(1, 2), *peer_w, tm=min(TM_PEER, S))
        xs = peer_layer(xs, ms(1, 0), ms(1, 1), ms(1, 2), *peer_w, tm=n)
    return (xp, xs.reshape(DB, T, D)) + tuple(jnp.stack(outs[name]) for name in outs)
```

```python
import functools
import math

import jax
import jax.numpy as jnp
from jax import lax
from jax.experimental import pallas as pl
from jax.experimental.pallas import tpu as pltpu

F32 = jnp.float32
BF16 = jnp.bfloat16

D_MODEL = 1024
DEPTH = 4
PAGE_SIZE = 128
N_MIXERS = 3

N_HEADS = 16
HEAD_DIM = D_MODEL // N_HEADS
N_KV_HEADS = 4
GROUP = N_HEADS // N_KV_HEADS
KV_WIDTH = N_KV_HEADS * HEAD_DIM
Q_BLOCK = 128
ATTN_SCALE = HEAD_DIM ** -0.5

IDX_HEADS = 8
IDX_DIM = 64
IDX_SCALE = (IDX_HEADS * IDX_DIM) ** -0.5
DSA_TOPK = 256

CMP_LEN = 32
CMP_HIDDEN = 2 * HEAD_DIM
SEL_BLOCK = 64
N_SEL = 16
WINDOW = 512

MLSTM_INNER = 2 * D_MODEL
MLSTM_HEADS = 4
MLSTM_HEAD_DIM = MLSTM_INNER // MLSTM_HEADS
CONV_WIDTH = 4
MLSTM_CHUNK = 64

PEER_HEADS = 8
PEER_KEYS = 128
N_EXPERTS = PEER_KEYS * PEER_KEYS
PEER_KEY_DIM = 256
PEER_TOPK = 16

ALPHA = (2 * DEPTH) ** 0.25
LN_EPS = 1e-5

LANES = 128
VMEM_LIMIT = 56 * 1024 * 1024

_NT = (((1,), (1,)), ((), ()))


def _cparams(*sem):
    return pltpu.CompilerParams(dimension_semantics=sem, vmem_limit_bytes=VMEM_LIMIT)


def _gelu(x):
    return 0.5 * x * (1.0 + jnp.tanh(math.sqrt(2.0 / math.pi) * (x + 0.044715 * (x * x * x))))


def _post_norm_math(x, y, gate, g, b):
    z = ALPHA * x + gate * y
    mu = jnp.mean(z, axis=-1, keepdims=True)
    zc = z - mu
    var = jnp.mean(zc * zc, axis=-1, keepdims=True)
    return zc * lax.rsqrt(var + LN_EPS) * g + b


def _adaln_kernel(c_ref, w_ref, b_ref, o_ref):
    c = c_ref[...]
    s = (c * jax.nn.sigmoid(c)).astype(BF16)
    o_ref[0] = jnp.dot(s, w_ref[0].astype(BF16), preferred_element_type=F32) + b_ref[0]


def adaln_all(c, ada_w, ada_b):
    M = c.shape[0]
    n = ada_w.shape[-1] // D_MODEL
    return pl.pallas_call(
        _adaln_kernel,
        out_shape=jax.ShapeDtypeStruct((DEPTH, M, n * D_MODEL), F32),
        grid=(DEPTH, n),
        in_specs=[pl.BlockSpec((M, D_MODEL), lambda i, j: (0, 0)),
                  pl.BlockSpec((1, D_MODEL, D_MODEL), lambda i, j: (i, 0, j)),
                  pl.BlockSpec((1, 1, D_MODEL), lambda i, j: (i, 0, j))],
        out_specs=pl.BlockSpec((1, M, D_MODEL), lambda i, j: (i, 0, j)),
        compiler_params=_cparams("arbitrary", "arbitrary"),
        name="adaln",
    )(c, ada_w, ada_b.reshape(DEPTH, 1, -1))


def _proj_kernel(x_ref, sh_ref, sc_ref, *refs, n_w, out_map):
    w_refs, o_refs = refs[:n_w], refs[n_w:]
    h = (x_ref[0] * (1.0 + sc_ref[0]) + sh_ref[0]).astype(BF16)
    done = {}
    for o_ref, wi in zip(o_refs, out_map):
        if wi not in done:
            done[wi] = jnp.dot(h, w_refs[wi][...], preferred_element_type=F32)
        o_ref[0] = done[wi].astype(o_ref.dtype)


def proj(x, shift, scale, weights, outs, tm):
    nb, S, D = x.shape
    rows = shift.shape[1]
    mblk = (1, tm, D) if rows == S else (1, 1, D)
    mmap = (lambda b, i: (b, i, 0)) if rows == S else (lambda b, i: (b, 0, 0))
    in_specs = [pl.BlockSpec((1, tm, D), lambda b, i: (b, i, 0)),
                pl.BlockSpec(mblk, mmap), pl.BlockSpec(mblk, mmap)]
    in_specs += [pl.BlockSpec(w.shape, lambda b, i: (0, 0)) for w in weights]
    out_shape = [jax.ShapeDtypeStruct((nb, S, weights[wi].shape[1]), dt) for wi, dt in outs]
    out_specs = [pl.BlockSpec((1, tm, weights[wi].shape[1]), lambda b, i: (b, i, 0)) for wi, _ in outs]
    return pl.pallas_call(
        functools.partial(_proj_kernel, n_w=len(weights), out_map=tuple(wi for wi, _ in outs)),
        out_shape=out_shape, grid=(nb, S // tm), in_specs=in_specs, out_specs=out_specs,
        compiler_params=_cparams("parallel", "parallel"),
        name="proj",
    )(x, shift, scale, *weights)


def _outproj_kernel(o_ref, w_ref, x_ref, gate_ref, g_ref, b_ref, y_ref):
    y = jnp.dot(o_ref[0], w_ref[...], preferred_element_type=F32)
    y_ref[0] = _post_norm_math(x_ref[0], y, gate_ref[0], g_ref[...], b_ref[...])


def outproj_postnorm(o, w, x, gate, g, b, tm):
    nb, S, K = o.shape
    D = x.shape[-1]
    rows = gate.shape[1]
    mblk = (1, tm, D) if rows == S else (1, 1, D)
    mmap = (lambda bi, i: (bi, i, 0)) if rows == S else (lambda bi, i: (bi, 0, 0))
    return pl.pallas_call(
        _outproj_kernel,
        out_shape=jax.ShapeDtypeStruct(x.shape, F32),
        grid=(nb, S // tm),
        in_specs=[pl.BlockSpec((1, tm, K), lambda bi, i: (bi, i, 0)),
                  pl.BlockSpec((K, D), lambda bi, i: (0, 0)),
                  pl.BlockSpec((1, tm, D), lambda bi, i: (bi, i, 0)),
                  pl.BlockSpec(mblk, mmap),
                  pl.BlockSpec((1, D), lambda bi, i: (0, 0)),
                  pl.BlockSpec((1, D), lambda bi, i: (0, 0))],
        out_specs=pl.BlockSpec((1, tm, D), lambda bi, i: (bi, i, 0)),
        compiler_params=_cparams("parallel", "parallel"),
        name="outproj_postnorm",
    )(o, w, x, gate, g.reshape(1, D), b.reshape(1, D))


def _top_rows(x, iota, n):
    vals, idxs = [], []
    for _ in range(n):
        m = jnp.max(x, axis=0, keepdims=True)
        idx = jnp.min(jnp.where(x == m, iota, jnp.inf), axis=0, keepdims=True)
        vals.append(m)
        idxs.append(idx)
        x = jnp.where(iota == idx, -jnp.inf, x)
    return vals, idxs


def _peer_route_kernel(x_ref, sh_ref, sc_ref, wq_ref, keys_ref, i1_ref, i2_ref, g_ref):
    tm = x_ref.shape[1]
    h = (x_ref[0] * (1.0 + sc_ref[0]) + sh_ref[0]).astype(BF16)
    q = jnp.dot(h, wq_ref[...], preferred_element_type=F32).astype(BF16)
    iota_k = lax.broadcasted_iota(jnp.int32, (PEER_KEYS, tm), 0).astype(F32)
    half = PEER_KEY_DIM // 2
    n = PEER_TOPK
    row16 = lax.broadcasted_iota(jnp.int32, (n, tm), 0).astype(F32)
    row8 = lax.broadcasted_iota(jnp.int32, (8, tm), 0).astype(F32)
    pieces = [(0, 1, 16, 0, 15), (1, 1, 8, 0, 7), (2, 1, 8, 0, 4), (3, 1, 8, 0, 3),
              (0, 2, 16, 4, 15), (1, 2, 8, 4, 7), (2, 2, 8, 4, 4)]
    ids = []
    for fixed, which, rows, lo, hi in pieces:
        r = row16 if rows == 16 else row8
        pair = fixed * n + r if which == 1 else r * n + fixed
        ids.append(jnp.where(r < lo, float(n * n), jnp.where(r > hi, float(n * n), pair)))
    cand_id = jnp.concatenate(ids, axis=0)
    cand_ok = cand_id < float(n * n)
    i1_rows, i2_rows, g_rows = [], [], []
    for hd in range(PEER_HEADS):
        tops = []
        for c in range(2):
            col = (hd * 2 + c) * half
            s_t = lax.dot_general(keys_ref[c], q[:, col:col + half], _NT, preferred_element_type=F32)
            tops.append(_top_rows(s_t, iota_k, n))
        (v1, id1), (v2, id2) = tops
        stacked = {(w, rows): jnp.concatenate((v1, v2)[w - 1][:rows], axis=0) for w in (1, 2) for rows in (8, n)}
        sums = [(v1[fixed] + stacked[2, rows]) if which == 1 else (stacked[1, rows] + v2[fixed])
                for fixed, which, rows, _, _ in pieces]
        cand = jnp.where(cand_ok, jnp.concatenate(sums, axis=0), NEG_INF)
        cvals, cidx = _top_rows(cand, cand_id, n)
        cv = jnp.concatenate(cvals, axis=0)
        ci = jnp.concatenate(cidx, axis=0)
        r1 = jnp.floor(ci * (1.0 / PEER_TOPK))
        r2 = ci - r1 * PEER_TOPK
        i1 = jnp.zeros_like(ci)
        i2 = jnp.zeros_like(ci)
        for r in range(PEER_TOPK):
            i1 = i1 + jnp.where(r1 == float(r), id1[r], 0.0)
            i2 = i2 + jnp.where(r2 == float(r), id2[r], 0.0)
        e = jnp.exp(cv - cvals[0])
        gate = e / jnp.sum(e, axis=0, keepdims=True)
        i1_rows.append(i1)
        i2_rows.append(i2)
        g_rows.append(gate)
    i1_ref[0] = jnp.concatenate(i1_rows, axis=0).T
    i2_ref[0] = jnp.concatenate(i2_rows, axis=0).T
    g_ref[0] = jnp.concatenate(g_rows, axis=0).T


def _mod_specs(rows, S, tm, D):
    if rows == S:
        return pl.BlockSpec((1, tm, D), lambda b, i, *_: (b, i, 0))
    return pl.BlockSpec((1, 1, D), lambda b, i, *_: (b, 0, 0))


def peer_route(x, shift, scale, wq, keys, tm):
    nb, S, D = x.shape
    nsel = PEER_HEADS * PEER_TOPK
    mspec = _mod_specs(shift.shape[1], S, tm, D)
    return pl.pallas_call(
        _peer_route_kernel,
        out_shape=[jax.ShapeDtypeStruct((nb, S, nsel), F32)] * 3,
        grid=(nb, S // tm),
        in_specs=[pl.BlockSpec((1, tm, D), lambda b, i: (b, i, 0)), mspec, mspec,
                  pl.BlockSpec(wq.shape, lambda b, i: (0, 0)),
                  pl.BlockSpec(keys.shape, lambda b, i: (0, 0, 0))],
        out_specs=[pl.BlockSpec((1, tm, nsel), lambda b, i: (b, i, 0))] * 3,
        compiler_params=_cparams("parallel", "parallel"),
        name="peer_route",
    )(x, shift, scale, wq, keys)


PEER_TOKEN_GROUP = 16


def _peer_expert_kernel(x_ref, sh_ref, sc_ref, gt_ref, i1_ref, i2_ref, g_ref, ut_ref, v_ref, lg_ref, lb_ref,
                        y_ref, w3_ref, acc_ref, h_ref, *, ac):
    tm = x_ref.shape[1]
    j = pl.program_id(2)

    @pl.when(j == 0)
    def _():
        h_ref[...] = (x_ref[0] * (1.0 + sc_ref[0]) + sh_ref[0]).astype(BF16)
        acc_ref[...] = jnp.zeros_like(acc_ref)
        iota_s = lax.broadcasted_iota(jnp.int32, (PEER_KEYS, LANES), 0).astype(F32)

        def build(gi, carry):
            t0 = pl.multiple_of(gi * PEER_TOKEN_GROUP, PEER_TOKEN_GROUP)
            i1g = i1_ref[0, pl.ds(t0, PEER_TOKEN_GROUP), :]
            i2g = i2_ref[0, pl.ds(t0, PEER_TOKEN_GROUP), :]
            gg = g_ref[0, pl.ds(t0, PEER_TOKEN_GROUP), :]
            tiles = []
            for t in range(PEER_TOKEN_GROUP):
                g1t = jnp.where(iota_s == i1g[t:t + 1], gg[t:t + 1], 0.0).astype(BF16)
                o2t = jnp.where(iota_s == i2g[t:t + 1], 1.0, 0.0).astype(BF16)
                tiles.append(lax.dot_general(g1t, o2t, _NT, preferred_element_type=F32))
            w3_ref[:, pl.ds(t0, PEER_TOKEN_GROUP), :] = jnp.swapaxes(jnp.stack(tiles, axis=0), 0, 1).astype(BF16)
            return carry

        lax.fori_loop(0, tm // PEER_TOKEN_GROUP, build, 0)

    act = _gelu(jnp.dot(h_ref[...], ut_ref[...], preferred_element_type=F32))
    wc = jnp.concatenate([w3_ref[j * ac + a] for a in range(ac)], axis=1).astype(F32)
    coef = (wc * act).astype(BF16)
    acc_ref[...] += jnp.dot(coef, v_ref[...], preferred_element_type=F32)

    @pl.when(j == pl.num_programs(2) - 1)
    def _():
        y_ref[0] = _post_norm_math(x_ref[0], acc_ref[...], gt_ref[0], lg_ref[...], lb_ref[...])


def peer_experts(x, shift, scale, gate, i1, i2, g, ut, v, ln_g, ln_b, tm, ac):
    nb, S, D = x.shape
    assert tm % PEER_TOKEN_GROUP == 0 and S % tm == 0, (S, tm)
    nsel = PEER_HEADS * PEER_TOPK
    ce = ac * PEER_KEYS
    mspec = _mod_specs(shift.shape[1], S, tm, D)
    pick =pl.BlockSpec((1, tm, nsel), lambda b, i, j: (b, i, 0))
    return pl.pallas_call(
        functools.partial(_peer_expert_kernel, ac=ac),
        out_shape=jax.ShapeDtypeStruct(x.shape, F32),
        grid=(nb, S // tm, PEER_KEYS // ac),
        in_specs=[pl.BlockSpec((1, tm, D), lambda b, i, j: (b, i, 0)), mspec, mspec, mspec, pick, pick, pick,
                  pl.BlockSpec((D, ce), lambda b, i, j: (0, j)),
                  pl.BlockSpec((ce, D), lambda b, i, j: (j, 0)),
                  pl.BlockSpec((1, D), lambda b, i, j: (0, 0)),
                  pl.BlockSpec((1, D), lambda b, i, j: (0, 0))],
        out_specs=pl.BlockSpec((1, tm, D), lambda b, i, j: (b, i, 0)),
        scratch_shapes=[pltpu.VMEM((PEER_KEYS, tm, PEER_KEYS), BF16),
                        pltpu.VMEM((tm, D), F32),
                        pltpu.VMEM((tm, D), BF16)],
        compiler_params=_cparams("parallel", "parallel", "arbitrary"),
        name="peer_experts",
    )(x, shift, scale, gate, i1, i2, g, ut, v, ln_g.reshape(1, D), ln_b.reshape(1, D))


def peer_layer(x, shift, scale, gate, wq, keys, ut, v, ln_g, ln_b, tm_route, tm):
    i1, i2, g = peer_route(x, shift, scale, wq, keys, tm_route)
    return peer_experts(x, shift, scale, gate, i1, i2, g, ut, v, ln_g, ln_b, tm, ac=8)


INT_MIN = -2 ** 31
NEG_INF = float("-inf")
FLASH_ROWS_PER_UPDATE = 512
FLASH_ROW_TILE = 64


def _to_key(x):
    b = lax.bitcast_convert_type(x, jnp.int32)
    return b ^ ((b >> 31) & 0x7FFFFFFF)


def _count_cols(key_ref, nck, kc, pred):
    rows = key_ref.shape[0]

    def body(c, acc):
        c0 = pl.multiple_of(c * kc, kc)
        hit = jnp.where(pred(key_ref[:, pl.ds(c0, kc)], c0), 1.0, 0.0)
        part = hit[:, 0:LANES]
        for u in range(1, kc // LANES):
            part = part + hit[:, u * LANES:(u + 1) * LANES]
        return acc + part

    acc = lax.fori_loop(0, nck, body, jnp.zeros((rows, LANES), F32))
    return jnp.sum(acc, axis=1, keepdims=True)


def _topk_threshold(key_ref, nck, kc, k, idx_bits):
    rows = key_ref.shape[0]
    kf = float(k)

    def bit_step(p, t_u):
        cand_u = t_u | jnp.left_shift(jnp.int32(1), 31 - p)
        cand_s = cand_u ^ INT_MIN
        cnt = _count_cols(key_ref, nck, kc, lambda tile, c0: tile >= cand_s)
        return jnp.where(cnt >= kf, cand_u, t_u)

    t_u = lax.fori_loop(0, 32, bit_step, jnp.zeros((rows, 1), jnp.int32))
    thr = jnp.maximum(t_u ^ INT_MIN, INT_MIN + 1)
    n_ge = _count_cols(key_ref, nck, kc, lambda tile, c0: tile >= thr)

    @pl.when(jnp.max(n_ge) > kf)
    def _():
        need = kf - _count_cols(key_ref, nck, kc, lambda tile, c0: tile > thr)
        iota = lax.broadcasted_iota(jnp.int32, (rows, kc), 1)

        def idx_step(p, j_hi):
            cand = j_hi | jnp.left_shift(jnp.int32(1), idx_bits - 1 - p)
            cnt = _count_cols(key_ref, nck, kc, lambda tile, c0: (tile == thr) & (c0 + iota < cand))
            return jnp.where(cnt <= need, cand, j_hi)

        j_hi = lax.fori_loop(0, idx_bits, idx_step, jnp.zeros((rows, 1), jnp.int32))
        surplus = n_ge > kf

        def lower(c, carry):
            c0 = pl.multiple_of(c * kc, kc)
            tile = key_ref[:, pl.ds(c0, kc)]
            drop = (tile == thr) & (c0 + iota >= j_hi) & surplus
            key_ref[:, pl.ds(c0, kc)] = jnp.where(drop, thr - 1, tile)
            return carry

        lax.fori_loop(0, nck, lower, 0)

    return thr


def _stack_heads(q_ref, qs_ref, tq):
    for hd in range(N_HEADS):
        qs_ref[hd * tq:(hd + 1) * tq, :] = (q_ref[0, :, hd * HEAD_DIM:(hd + 1) * HEAD_DIM] * ATTN_SCALE).astype(BF16)


def _flash_init(m_ref, l_ref, acc_ref):
    m_ref[...] = jnp.full(m_ref.shape, NEG_INF, F32)
    l_ref[...] = jnp.zeros(l_ref.shape, F32)
    acc_ref[...] = jnp.zeros(acc_ref.shape, F32)


def _flash_result(rows, l_ref, acc_ref):
    return acc_ref[rows, :] / jnp.maximum(l_ref[rows, :HEAD_DIM], 1e-30)


def _tiled_flash_step(mask_of, kch, vch, qs_ref, m_ref, l_ref, acc_ref, tq, s_ref, p_ref, b_ref):
    rows_h = GROUP * tq
    kc = kch.shape[0]
    rt = FLASH_ROW_TILE
    for h in range(N_KV_HEADS):
        cols = slice(h * HEAD_DIM, (h + 1) * HEAD_DIM)
        s_ref[:, 0:kc] = lax.dot_general(qs_ref[h * rows_h:(h + 1) * rows_h, :], kch[:, cols], _NT,
                                         preferred_element_type=F32)
        b_ref[:, 0:kc] = jnp.where(mask_of(h), 0.0, NEG_INF)
        for r0 in range(0, rows_h, rt):
            rows = slice(h * rows_h + r0, h * rows_h + r0 + rt)
            q0 = r0 % tq
            s = s_ref[r0:r0 + rt, 0:kc] + b_ref[q0:q0 + rt, 0:kc]
            m_prev = m_ref[rows, :]
            m_new = jnp.maximum(m_prev, jnp.max(s, axis=1, keepdims=True))
            m_safe = jnp.where(m_new == NEG_INF, 0.0, m_new)
            p = jnp.exp(s - pltpu.repeat(m_safe, s.shape[1] // LANES, axis=1))
            alpha = jnp.exp(m_prev - m_safe)
            l_ref[rows, :] = alpha * l_ref[rows, :] + jnp.sum(p, axis=1, keepdims=True)
            p_ref[r0:r0 + rt, 0:kc] = p.astype(BF16)
            acc_ref[rows, :] = alpha[:, :HEAD_DIM] * acc_ref[rows, :]
            m_ref[rows, :] = m_new
        rows = slice(h * rows_h, (h + 1) * rows_h)
        acc_ref[rows, :] = acc_ref[rows, :] + jnp.dot(p_ref[:, 0:kc], vch[:, cols], preferred_element_type=F32)


def _grouped_flash_step(mask_of, kch, vch, qs_ref, m_ref, l_ref, acc_ref, tq, tiles=None):
    rows_h = GROUP * tq
    if rows_h * N_KV_HEADS > FLASH_ROWS_PER_UPDATE:
        return _tiled_flash_step(mask_of, kch, vch, qs_ref, m_ref, l_ref, acc_ref, tq, *tiles)
    kv_per_update = N_KV_HEADS
    for h0 in range(0, N_KV_HEADS, kv_per_update):
        heads = range(h0, h0 + kv_per_update)
        parts = []
        for h in heads:
            logits = lax.dot_general(qs_ref[h * rows_h:(h + 1) * rows_h, :], kch[:, h * HEAD_DIM:(h + 1) * HEAD_DIM],
                                     _NT, preferred_element_type=F32)
            kc = logits.shape[1]
            parts.append(jnp.where(mask_of(h)[None], logits.reshape(GROUP, tq, kc), NEG_INF).reshape(rows_h, kc))
        s = parts[0] if kv_per_update == 1 else jnp.concatenate(parts, axis=0)
        rows = slice(h0 * rows_h, (h0 + kv_per_update) * rows_h)
        m_prev = m_ref[rows, :]
        m_new = jnp.maximum(m_prev, jnp.max(s, axis=1, keepdims=True))
        m_safe = jnp.where(m_new == NEG_INF, 0.0, m_new)
        p = jnp.exp(s - pltpu.repeat(m_safe, s.shape[1] // LANES, axis=1))
        alpha = jnp.exp(m_prev - m_safe)
        l_ref[rows, :] = alpha * l_ref[rows, :] + jnp.sum(p, axis=1, keepdims=True)
        pb = p.astype(BF16)
        pv = [jnp.dot(pb[i * rows_h:(i + 1) * rows_h], vch[:, h * HEAD_DIM:(h + 1) * HEAD_DIM],
                      preferred_element_type=F32) for i, h in enumerate(heads)]
        acc_ref[rows, :] = alpha[:, :HEAD_DIM] * acc_ref[rows, :] + (pv[0] if kv_per_update == 1
                                                                     else jnp.concatenate(pv, axis=0))
        m_ref[rows, :] = m_new


def _index_scores(qi, wi, kic):
    sc = None
    for hh in range(IDX_HEADS):
        s = lax.dot_general(qi[:, hh * IDX_DIM:(hh + 1) * IDX_DIM], kic, _NT, preferred_element_type=F32)
        term = jnp.maximum(s, 0.0) * wi[:, hh:hh + 1]
        sc = term if sc is None else sc + term
    return sc


def _dsa_prompt_kernel(q_ref, qi_ref, wi_ref, k_ref, v_ref, ki_ref, o_ref,
                       key_ref, qs_ref, m_ref, l_ref, acc_ref, *tiles, kc,topk, idx_bits):
    tq = q_ref.shape[1]
    t0 = pl.program_id(1) * tq
    nck = (t0 + tq + kc - 1) // kc
    _stack_heads(q_ref, qs_ref, tq)
    qpos = t0 + lax.broadcasted_iota(jnp.int32, (tq, 1), 0)
    kiota = lax.broadcasted_iota(jnp.int32, (1, kc), 1)
    qi = qi_ref[0]
    wi = wi_ref[0] * IDX_SCALE

    def score(c, carry):
        c0 = pl.multiple_of(c * kc, kc)
        sc = _index_scores(qi, wi, ki_ref[0, pl.ds(c0, kc), :])
        key_ref[:, pl.ds(c0, kc)] = jnp.where(c0 + kiota <= qpos, _to_key(sc), INT_MIN)
        return carry

    lax.fori_loop(0, nck, score, 0)
    thr = _topk_threshold(key_ref, nck, kc, topk, idx_bits)
    _flash_init(m_ref, l_ref, acc_ref)

    def attend(c, carry):
        c0 = pl.multiple_of(c * kc, kc)
        sel = key_ref[:, pl.ds(c0, kc)] >= thr
        _grouped_flash_step(lambda h: sel, k_ref[0, pl.ds(c0, kc), :], v_ref[0, pl.ds(c0, kc), :],
                            qs_ref, m_ref, l_ref, acc_ref, tq, tiles)
        return carry

    lax.fori_loop(0, nck, attend, 0)
    for hd in range(N_HEADS):
        o_ref[0, :, hd * HEAD_DIM:(hd + 1) * HEAD_DIM] = _flash_result(
            slice(hd * tq, (hd + 1) * tq), l_ref, acc_ref).astype(o_ref.dtype)


def _flash_scratch(tq, kmax):
    rows_h = GROUP * tq
    return [pltpu.VMEM((N_HEADS * tq, HEAD_DIM), BF16),
            pltpu.VMEM((N_HEADS * tq, LANES), F32),
            pltpu.VMEM((N_HEADS * tq, LANES), F32),
            pltpu.VMEM((N_HEADS * tq, HEAD_DIM), F32),
            pltpu.VMEM((rows_h, kmax), F32),
            pltpu.VMEM((rows_h, kmax), BF16),
            pltpu.VMEM((tq, kmax), F32)]


def dsa_prompt_attend(q, qi, wi, k, v, ki, tq=Q_BLOCK, kc=512):
    B, S, D = q.shape
    kc = min(kc, S)
    topk = min(DSA_TOPK, S // 4)
    blk = lambda w: pl.BlockSpec((1, tq, w), lambda b, i: (b, i, 0))
    full = lambda w: pl.BlockSpec((1, S, w), lambda b, i: (b, 0, 0))
    return pl.pallas_call(
        functools.partial(_dsa_prompt_kernel, kc=kc, topk=topk, idx_bits=S.bit_length()),
        out_shape=jax.ShapeDtypeStruct((B, S, D), BF16),
        grid=(B, S // tq),
        in_specs=[blk(D), blk(qi.shape[-1]), blk(wi.shape[-1]), full(KV_WIDTH), full(KV_WIDTH), full(IDX_DIM)],
        out_specs=blk(D),
        scratch_shapes=[pltpu.VMEM((tq, S), jnp.int32)] + _flash_scratch(tq, kc),
        compiler_params=_cparams("parallel", "arbitrary"),
        name="dsa_prompt_attend",
    )(q, qi, wi, k, v, ki)


def _dsa_sample_kernel(pt_ref, q_ref, qi_ref, wi_ref, kvn_ref, kin_ref, *refs, pg, kc, topk, idx_bits, n_new):
    ckv_refs, cki_refs = refs[:pg], refs[pg:2 * pg]
    o_ref, key_ref, kv_ref, qs_ref, m_ref, l_ref, acc_ref = refs[2 * pg:2 * pg + 7]
    tiles = refs[2 * pg + 7:]
    tq = q_ref.shape[1]
    j = pl.program_id(1)
    n_steps = pl.num_programs(1)
    past = kv_ref.shape[0] - PAGE_SIZE
    qi = qi_ref[0].astype(BF16)
    wi = wi_ref[0] * IDX_SCALE
    for k in range(pg):
        p0 = pl.multiple_of((j * pg + k) * PAGE_SIZE, PAGE_SIZE)
        kv_ref[pl.ds(p0, PAGE_SIZE), :] = ckv_refs[k][0, 0].astype(BF16)
        key_ref[:, pl.ds(p0, PAGE_SIZE)] = _to_key(_index_scores(qi, wi, cki_refs[k][0, 0].astype(BF16)))

    @pl.when(j == n_steps - 1)
    def _():
        nck = (past + PAGE_SIZE) // kc
        kv_ref[past:past + PAGE_SIZE, :] = kvn_ref[0]
        trow = lax.broadcasted_iota(jnp.int32, (tq, PAGE_SIZE), 0)
        ncol = lax.broadcasted_iota(jnp.int32, (tq, PAGE_SIZE), 1)
        visible = (ncol <= trow) & (ncol < n_new)
        key_ref[:, past:past + PAGE_SIZE] = jnp.where(visible, _to_key(_index_scores(qi, wi, kin_ref[0])), INT_MIN)
        _stack_heads(q_ref, qs_ref, tq)
        thr = _topk_threshold(key_ref, nck, kc, topk, idx_bits)
        _flash_init(m_ref, l_ref, acc_ref)

        def attend(c, carry):
            c0 = pl.multiple_of(c * kc, kc)
            sel = key_ref[:, pl.ds(c0, kc)] >= thr
            _grouped_flash_step(lambda h: sel, kv_ref[pl.ds(c0, kc), 0:KV_WIDTH],
                                kv_ref[pl.ds(c0, kc), KV_WIDTH:2 * KV_WIDTH], qs_ref, m_ref, l_ref, acc_ref, tq, tiles)
            return carry

        lax.fori_loop(0, nck, attend, 0)
        for hd in range(N_HEADS):
            o_ref[0, :, hd * HEAD_DIM:(hd + 1) * HEAD_DIM] = _flash_result(
                slice(hd * tq, (hd + 1) * tq), l_ref, acc_ref).astype(o_ref.dtype)


PAGES_PER_STEP = 8


def _pages_per_step(n_pages):
    return max(d for d in range(1, PAGES_PER_STEP + 1) if n_pages % d == 0)


def _page_specs(width, col, layer, pg):
    return [pl.BlockSpec((1, 1, PAGE_SIZE, width), lambda b, j, pt, k=k: (layer, pt[b, j * pg + k], 0, col))
            for k in range(pg)]


def _key_chunk(total):
    n = total // LANES
    return LANES * max(d for d in range(1, 9) if n % d == 0)


def dsa_sample_attend(q, qi, wi, kv_new, ki_new, cache_kv, cache_ki, layer, page_table, n_new):
    DB, tq, D = q.shape
    n_pages = page_table.shape[1]
    past = n_pages * PAGE_SIZE
    total = past + PAGE_SIZE
    kc = _key_chunk(total)
    topk = min(DSA_TOPK, (past + n_new) // 4)
    per_b = lambda r, w: pl.BlockSpec((1, r, w), lambda b, j, pt: (b, 0, 0))
    pg = _pages_per_step(n_pages)
    grid_spec = pltpu.PrefetchScalarGridSpec(
        num_scalar_prefetch=1,
        grid=(DB, n_pages // pg),
        in_specs=[per_b(tq, D), per_b(tq, qi.shape[-1]), per_b(tq, wi.shape[-1]),
                  per_b(PAGE_SIZE, 2 * KV_WIDTH), per_b(PAGE_SIZE, IDX_DIM)]
        + _page_specs(2 * KV_WIDTH, 0, layer, pg) + _page_specs(IDX_DIM, 0, layer, pg),
        out_specs=per_b(tq, D),
        scratch_shapes=[pltpu.VMEM((tq, total), jnp.int32),
                        pltpu.VMEM((total, 2 * KV_WIDTH), BF16)] + _flash_scratch(tq, kc),
    )
    return pl.pallas_call(
        functools.partial(_dsa_sample_kernel, pg=pg, kc=kc, topk=topk, idx_bits=total.bit_length(), n_new=n_new),
        out_shape=jax.ShapeDtypeStruct((DB, tq, D), BF16),
        grid_spec=grid_spec,
        compiler_params=_cparams("parallel", "arbitrary"),
        name="dsa_sample_attend",
    )(page_table, q, qi, wi, kv_new, ki_new, *([cache_kv] * pg), *([cache_ki] * pg))


def _nsa_compress_kernel(pt_ref, *refs, pg):
    x_refs = refs[:pg]
    pe_ref, w1_ref, w2_ref, o_ref, rows_ref = refs[pg:]
    j = pl.program_id(2)
    pair = 2 * CMP_LEN
    per_step = pg * PAGE_SIZE // pair
    x = jnp.concatenate([r[0, 0] for r in x_refs], axis=0) if pg > 1 else x_refs[0][0, 0]
    rows_ref[:, pl.ds(pl.multiple_of(j * per_step, per_step), per_step), :] = jnp.swapaxes(
        x.reshape(per_step, pair, x.shape[-1]), 0, 1)

    @pl.when(j == pl.num_programs(2) - 1)
    def _():
        acc = None
        for l in range(CMP_LEN):
            xl = (jnp.concatenate([rows_ref[l], rows_ref[CMP_LEN + l]], axis=0) + pe_ref[0, l:l + 1, :]).astype(BF16)
            part = jnp.dot(xl, w1_ref[0, l], preferred_element_type=F32)
            acc = part if acc is None else acc + part
        hid = _gelu(acc).astype(BF16)
        o_ref[0] = jnp.dot(hid, w2_ref[0], preferred_element_type=F32).astype(o_ref.dtype)


def nsa_compress(rows, layer, page_table, pe_t, w1_bd, w2_bd):
    NB, n_pages = page_table.shape
    nc = n_pages * PAGE_SIZE // CMP_LEN
    pg = _pages_per_step(n_pages)
    grid_spec = pltpu.PrefetchScalarGridSpec(
        num_scalar_prefetch=1,
        grid=(NB, 2, n_pages // pg),
        in_specs=[pl.BlockSpec((1, 1, PAGE_SIZE, KV_WIDTH), lambda b, s, j, pt, k=k: (layer, pt[b, j * pg + k], 0, s))
                  for k in range(pg)]
        + [pl.BlockSpec((1, CMP_LEN, KV_WIDTH), lambda b, s, j, pt: (s, 0, 0)),
           pl.BlockSpec((1, CMP_LEN, KV_WIDTH, N_KV_HEADS * CMP_HIDDEN), lambda b, s, j, pt: (s, 0, 0, 0)),
           pl.BlockSpec((1, N_KV_HEADS * CMP_HIDDEN, KV_WIDTH), lambda b, s, j, pt: (s, 0, 0))],
        out_specs=pl.BlockSpec((1, nc, KV_WIDTH), lambda b, s, j, pt: (b, 0, s)),
        scratch_shapes=[pltpu.VMEM((2 * CMP_LEN, nc // 2, KV_WIDTH), F32)],
    )
    return pl.pallas_call(
        functools.partial(_nsa_compress_kernel, pg=pg),
        out_shape=jax.ShapeDtypeStruct((NB, nc, 2 * KV_WIDTH), BF16),
        grid_spec=grid_spec,
        compiler_params=_cparams("parallel", "arbitrary", "arbitrary"),
        name="nsa_compress",
    )(page_table, *([rows] * pg), pe_t, w1_bd, w2_bd)


def nsa_compress_weights(cmp_pe, cmp_w1, cmp_w2):
    eye = jnp.eye(N_KV_HEADS, dtype=F32)
    w1 = cmp_w1.reshape(2, CMP_LEN, HEAD_DIM, CMP_HIDDEN)
    w1_bd = jnp.einsum('hg,kldj->klhdgj', eye, w1).reshape(2, CMP_LEN, KV_WIDTH, N_KV_HEADS * CMP_HIDDEN)
    w2_bd = jnp.einsum('hg,kjd->khjgd', eye, cmp_w2).reshape(2, N_KV_HEADS * CMP_HIDDEN, KV_WIDTH)
    pe_t = jnp.tile(cmp_pe, (1, 1, N_KV_HEADS))
    return pe_t, w1_bd.astype(BF16), w2_bd.astype(BF16)


def _nsa_compressed_branch(qs_ref, kvc, qpos, gate_of, out_ref, tq):
    nc = kvc.shape[0]
    half = nc // 2
    pcol = lax.broadcasted_iota(jnp.int32, (1, nc), 1)
    cidx = jnp.where(pcol < half, 2 * pcol, 2 * (pcol - half) + 1)
    visible = (cidx + 1) * CMP_LEN - 1 <= qpos
    imps = []
    for h in range(N_KV_HEADS):
        cols = slice(h * HEAD_DIM, (h + 1) * HEAD_DIM)
        logits = lax.dot_general(qs_ref[h * GROUP * tq:(h + 1) * GROUP * tq, :], kvc[:, cols], _NT,
                                 preferred_element_type=F32)
        vcols = slice(KV_WIDTH + h * HEAD_DIM, KV_WIDTH + (h + 1) * HEAD_DIM)
        imp = None
        for g in range(GROUP):
            hd = h * GROUP + g
            s = jnp.where(visible, logits[g * tq:(g + 1) * tq], NEG_INF)
            m = jnp.max(s, axis=1, keepdims=True)
            e = jnp.exp(s - jnp.where(m == NEG_INF, 0.0, m))
            pc = e / jnp.maximum(jnp.sum(e, axis=1, keepdims=True), 1e-30)
            o = jnp.dot(pc.astype(BF16), kvc[:, vcols], preferred_element_type=F32)
            out_ref[hd * tq:(hd + 1) * tq, :] = gate_of(hd, 0) * o
            pair = pc[:, :half] + pc[:, half:]
            imp = pair if imp is None else imp + pair
        imps.append(imp)
    return imps


def _select_blocks(imp, cur, n_blocks, n_pick, axis):
    blk = lax.broadcasted_iota(jnp.int32, imp.shape, axis)
    forced = (blk == 0) | (blk == cur) | (blk == cur - 1)
    x = jnp.where(forced, 16.0, imp)
    x = jnp.where(blk <= cur, x, -1.0)
    x = jnp.where(blk < n_blocks, x, -2.0)
    blk_f = blk.astype(F32)
    sel = jnp.zeros(imp.shape, F32)
    for _ in range(n_pick):
        m = jnp.max(x, axis=axis, keepdims=True)
        first = jnp.min(jnp.where(x == m, blk_f, float(imp.shape[axis])), axis=axis, keepdims=True)
        hit = blk_f == first
        sel = jnp.where(hit, 1.0, sel)
        x = jnp.where(hit, -2.0, x)
    return sel


def _nsa_finish_branch(br, gate_of, out_ref, l_ref, acc_ref, tq):
    for hd in range(N_HEADS):
        rows = slice(hd * tq, (hd + 1) * tq)
        out_ref[rows, :] = out_ref[rows, :] + gate_of(hd, br) * _flash_result(rows, l_ref, acc_ref)


def _nsa_selected_branch(sel_ref, kv_chunk, nck, kc, qpos, qs_ref, m_ref, l_ref, acc_ref, tq, tiles):
    ns_pad = sel_ref.shape[2]
    _flash_init(m_ref, l_ref, acc_ref)
    brow = lax.broadcasted_iota(jnp.int32, (ns_pad, kc), 0)
    kcol = lax.broadcasted_iota(jnp.int32, (ns_pad, kc), 1)
    kiota = lax.broadcasted_iota(jnp.int32, (1, kc), 1)

    def attend(c, carry):
        c0 = pl.multiple_of(c * kc, kc)
        expand = jnp.where((c0 + kcol) // SEL_BLOCK == brow, 1.0, 0.0).astype(BF16)
        causal = c0 + kiota <= qpos
        kch, vch = kv_chunk(c0)

        def mask_of(h):
            picked = jnp.dot(sel_ref[h], expand, preferred_element_type=F32)
            return jnp.where(causal, picked, 0.0) > 0.5

        _grouped_flash_step(mask_of, kch, vch, qs_ref, m_ref, l_ref, acc_ref, tq, tiles)
        return carry

    lax.fori_loop(0, nck, attend, 0)


def _nsa_window_branch(kw, vw, kwpos, qpos, qs_ref, m_ref, l_ref, acc_ref, tq, tiles):
    _flash_init(m_ref, l_ref, acc_ref)
    dist = qpos - kwpos
    wmask = (dist >= 0) & (dist < WINDOW) & (kwpos >= 0)
    _grouped_flash_step(lambda h: wmask, kw, vw, qs_ref, m_ref, l_ref, acc_ref, tq, tiles)


def _gate_fn(gates_ref):
    sig = jax.nn.sigmoid(gates_ref[0])
    return lambda hd, br: sig[:, hd * 3 + br:hd * 3 + br + 1]


def _nsa_write_out(o_ref, out_ref, tq):
    for hd in range(N_HEADS):
        o_ref[0, :, hd * HEAD_DIM:(hd + 1) * HEAD_DIM] = out_ref[hd * tq:(hd + 1) * tq, :].astype(o_ref.dtype)


def _nsa_prompt_kernel(q_ref, gates_ref, kvc_ref, sw_ref, o_ref,
                       sel_ref, out_ref, qs_ref, m_ref, l_ref, acc_ref, *tiles, kc,wlen):
    tq = q_ref.shape[1]
    S = sw_ref.shape[1]
    t0 = pl.program_id(1) * tq
    _stack_heads(q_ref, qs_ref, tq)
    qpos = t0 + lax.broadcasted_iota(jnp.int32, (tq, 1), 0)
    qpos_row = t0 + lax.broadcasted_iota(jnp.int32, (1, tq), 1)
    gate_of = _gate_fn(gates_ref)
    imps = _nsa_compressed_branch(qs_ref, kvc_ref[0], qpos, gate_of, out_ref, tq)
    n_blocks = S // SEL_BLOCK
    for h in range(N_KV_HEADS):
        sel_t = _select_blocks(imps[h].T, qpos_row // SEL_BLOCK, n_blocks, min(N_SEL, n_blocks), 0)
        sel_ref[h] = sel_t.T.astype(BF16)
    nck = (t0 + tq + kc - 1) // kc
    kv_chunk = lambda c0: (sw_ref[0, pl.ds(c0, kc), 0:KV_WIDTH], sw_ref[0, pl.ds(c0, kc), KV_WIDTH:2 * KV_WIDTH])
    _nsa_selected_branch(sel_ref, kv_chunk, nck, kc, qpos, qs_ref, m_ref, l_ref, acc_ref, tq, tiles)
    _nsa_finish_branch(1, gate_of, out_ref, l_ref, acc_ref, tq)
    start = pl.multiple_of(jnp.clip(t0 + tq - wlen, 0, S - wlen), tq)
    kwpos = start + lax.broadcasted_iota(jnp.int32, (1, wlen), 1)
    _nsa_window_branch(sw_ref[0, pl.ds(start, wlen), 2 * KV_WIDTH:3 * KV_WIDTH],
                       sw_ref[0, pl.ds(start, wlen), 3 * KV_WIDTH:4 * KV_WIDTH],
                       kwpos, qpos, qs_ref, m_ref, l_ref, acc_ref, tq, tiles)
    _nsa_finish_branch(2, gate_of, out_ref, l_ref, acc_ref, tq)
    _nsa_write_out(o_ref, out_ref, tq)


def nsa_prompt_attend(q, gates, kvc, selwin, tq=Q_BLOCK, kc=512):
    B, S, D = q.shape
    kc = min(kc, S)
    wlen = min(WINDOW + tq, S)
    n_blocks = S // SEL_BLOCK
    blk = lambda w: pl.BlockSpec((1, tq, w), lambda b, i: (b, i, 0))
    return pl.pallas_call(
        functools.partial(_nsa_prompt_kernel, kc=kc, wlen=wlen),
        out_shape=jax.ShapeDtypeStruct((B, S, D), BF16),
        grid=(B, S // tq),
        in_specs=[blk(D), blk(gates.shape[-1]),
                  pl.BlockSpec((1,) + kvc.shape[1:], lambda b, i: (b, 0, 0)),
                  pl.BlockSpec((1, S, selwin.shape[-1]), lambda b, i: (b, 0, 0), pipeline_mode=pl.Buffered(1))],
        out_specs=blk(D),
        scratch_shapes=[pltpu.VMEM((N_KV_HEADS, tq, n_blocks), BF16),
                        pltpu.VMEM((N_HEADS * tq, HEAD_DIM), F32)] + _flash_scratch(tq, max(kc, wlen)),
        compiler_params=_cparams("parallel", "arbitrary"),
        name="nsa_prompt_attend",
    )(q, gates, kvc, selwin)


def _nsa_sample_kernel(pt_ref, q_ref, gates_ref, kvc_ref, kvn_ref, win_ref, winn_ref, *refs, pg, kc, n_new):
    csel_refs = refs[:pg]
    o_ref, kv_ref, wkv_ref, sel_ref, out_ref, qs_ref, m_ref, l_ref, acc_ref = refs[pg:pg + 9]
    tiles = refs[pg + 9:]
    tq = q_ref.shape[1]
    j = pl.program_id(1)
    past = kv_ref.shape[0] - PAGE_SIZE
    for k in range(pg):
        p0 = pl.multiple_of((j * pg + k) * PAGE_SIZE, PAGE_SIZE)
        kv_ref[pl.ds(p0, PAGE_SIZE), :] = csel_refs[k][0, 0].astype(BF16)

    @pl.when(j == pl.num_programs(1) - 1)
    def _():
        kv_ref[past:past + PAGE_SIZE, :] = kvn_ref[0]
        wb = win_ref.shape[1]
        wkv_ref[0:wb, :] = win_ref[0].astype(BF16)
        wkv_ref[wb:wb + PAGE_SIZE, :] = winn_ref[0]
        _stack_heads(q_ref, qs_ref, tq)
        qpos = past + jnp.minimum(lax.broadcasted_iota(jnp.int32, (tq, 1), 0), n_new - 1)
        gate_of = _gate_fn(gates_ref)
        imps = _nsa_compressed_branch(qs_ref, kvc_ref[0], qpos, gate_of, out_ref, tq)
        n_blocks = -(-(past + n_new) // SEL_BLOCK)
        ns_pad = sel_ref.shape[2]
        for h in range(N_KV_HEADS):
            imp = imps[h]
            imp = jnp.concatenate([imp, jnp.zeros((tq, ns_pad - imp.shape[1]), F32)], axis=1)
            sel_ref[h] = _select_blocks(imp, qpos // SEL_BLOCK, n_blocks, min(N_SEL, n_blocks), 1).astype(BF16)
        nck = (past + PAGE_SIZE) // kc
        kv_chunk = lambda c0: (kv_ref[pl.ds(c0, kc), 0:KV_WIDTH], kv_ref[pl.ds(c0, kc), KV_WIDTH:2 * KV_WIDTH])
        _nsa_selected_branch(sel_ref, kv_chunk, nck, kc, qpos, qs_ref, m_ref, l_ref, acc_ref, tq, tiles)
        _nsa_finish_branch(1, gate_of, out_ref, l_ref, acc_ref, tq)
        kwpos = past - wb + lax.broadcasted_iota(jnp.int32, (1, wb + PAGE_SIZE), 1)
        _nsa_window_branch(wkv_ref[:, 0:KV_WIDTH], wkv_ref[:, KV_WIDTH:2 * KV_WIDTH], kwpos, qpos,
                           qs_ref, m_ref, l_ref, acc_ref, tq, tiles)
        _nsa_finish_branch(2, gate_of, out_ref, l_ref, acc_ref, tq)
        _nsa_write_out(o_ref, out_ref, tq)


def nsa_sample_attend(q, gates, kvc, sel_new, win_buf, win_new, cache_kv, layer, page_table, n_new):
    DB, tq, D = q.shape
    n_pages = page_table.shape[1]
    past = n_pages * PAGE_SIZE
    total = past + PAGE_SIZE
    kc = _key_chunk(total)
    wb = win_buf.shape[1]
    ns_pad = -(-(total // SEL_BLOCK) // LANES) * LANES
    per_b = lambda r, w: pl.BlockSpec((1, r, w), lambda b, j, pt: (b, 0, 0))
    pg = _pages_per_step(n_pages)
    grid_spec = pltpu.PrefetchScalarGridSpec(
        num_scalar_prefetch=1,
        grid=(DB, n_pages // pg),
        in_specs=[per_b(tq, D), per_b(tq, gates.shape[-1]), per_b(kvc.shape[1], kvc.shape[2]),
                  per_b(PAGE_SIZE, 2 * KV_WIDTH), per_b(wb, 2 * KV_WIDTH), per_b(PAGE_SIZE, 2 * KV_WIDTH)]
        + _page_specs(2 * KV_WIDTH, 1, layer, pg),
        out_specs=per_b(tq, D),
        scratch_shapes=[pltpu.VMEM((total, 2 * KV_WIDTH), BF16),
                        pltpu.VMEM((wb + PAGE_SIZE, 2 * KV_WIDTH), BF16),
                        pltpu.VMEM((N_KV_HEADS, tq, ns_pad), BF16),
                        pltpu.VMEM((N_HEADS * tq, HEAD_DIM), F32)] + _flash_scratch(tq, max(kc, wb + PAGE_SIZE)),
    )
    return pl.pallas_call(
        functools.partial(_nsa_sample_kernel, pg=pg, kc=kc, n_new=n_new),
        out_shape=jax.ShapeDtypeStruct((DB, tq, D), BF16),
        grid_spec=grid_spec,
        compiler_params=_cparams("parallel", "arbitrary"),
        name="nsa_sample_attend",
    )(page_table, q, gates, kvc, sel_new, win_buf, win_new, *([cache_kv] * pg))


HALO = 8
G_IG, G_LF, G_CUM = 0, MLSTM_HEADS, 2 * MLSTM_HEADS
MASKED_GATE = -1e30


def _split3(x):
    hi = x.astype(BF16)
    r1 = x - hi.astype(F32)
    mid = r1.astype(BF16)
    lo = (r1 - mid.astype(F32)).astype(BF16)
    return hi, mid, lo


def _mlstm_pre_kernel(xm_ref, halo_ref, cw_ref, cb_ref, wq_ref, wk_ref, wv_ref, wvt_ref, wg_ref, bg_ref,
                      q_ref, k_ref, v_ref, vt_ref, xc_ref, g_ref, gt_ref, xp_ref, *, chunk, n_valid):
    tm = xm_ref.shape[1]
    hd = MLSTM_HEAD_DIM
    xp_ref[0:HALO, :] = halo_ref[0, 0]
    xp_ref[HALO:HALO + tm, :] = xm_ref[0]
    y = cb_ref[...]
    for j in range(CONV_WIDTH):
        off = HALO - (CONV_WIDTH - 1) + j
        y = y + cw_ref[j:j + 1, :] * xp_ref[off:off + tm, :]
    xc = y * jax.nn.sigmoid(y)
    xc_ref[0] = xc
    xcb = xc.astype(BF16)
    xmb = xm_ref[0].astype(BF16)
    g = bg_ref[...]
    for h in range(MLSTM_HEADS):
        cols = slice(h * hd, (h + 1) * hd)
        q = jnp.dot(xcb[:, cols], wq_ref[h], preferred_element_type=F32).astype(BF16)
        k = (jnp.dot(xcb[:, cols], wk_ref[h], preferred_element_type=F32) * hd ** -0.5).astype(BF16)
        v = jnp.dot(xmb[:, cols], wv_ref[h], preferred_element_type=F32).astype(BF16)
        q_ref[0, :, cols] = q
        k_ref[0, :, cols] = k
        v_ref[0, :, cols] = v
        vt_ref[0, cols, :] = lax.dot_general(wvt_ref[h], xmb[:, cols], _NT, preferred_element_type=F32).astype(BF16)
        for i, a in enumerate((q, k, v)):
            g = g + jnp.dot(a, wg_ref[i * MLSTM_INNER + h * hd:i * MLSTM_INNER + (h + 1) * hd, :],
                            preferred_element_type=F32)
    col = lax.broadcasted_iota(jnp.int32, (tm, LANES), 1)
    row = pl.program_id(1) * tm + lax.broadcasted_iota(jnp.int32, (tm, LANES), 0)
    log_f = jnp.minimum(g, 0.0) - jnp.log1p(jnp.exp(-jnp.abs(g)))
    g = jnp.where(col < G_LF, g, log_f)
    live = row < n_valid
    g = jnp.where(live, g, jnp.where(col < G_LF, MASKED_GATE, 0.0))
    tri = (lax.broadcasted_iota(jnp.int32, (chunk, chunk), 0) >= lax.broadcasted_iota(jnp.int32, (chunk, chunk), 1))
    tri = jnp.where(tri, 1.0, 0.0).astype(BF16)
    for c in range(tm // chunk):
        rows = slice(c * chunk, (c + 1) * chunk)
        cum = None
        for piece in _split3(g[rows]):
            part = jnp.dot(tri, piece, preferred_element_type=F32)
            cum = part if cum is None else cum + part
        out = jnp.where(col[rows] < G_CUM, g[rows], pltpu.roll(cum, G_CUM - G_LF, 1))
        g_ref[0, rows, :] = out
        gt_ref[0, :, rows] = out.T[0:gt_ref.shape[1], :]


def mlstm_pre(xm, halo, conv_w, conv_b, wq, wk, wv, wvt, wg, bg, tm, chunk, n_valid):
    NB, S, W = xm.shape
    hd = MLSTM_HEAD_DIM
    const = lambda a: pl.BlockSpec(a.shape, lambda b, i: (0,) * a.ndim)
    row_blk = lambda w: pl.BlockSpec((1, tm, w), lambda b, i: (b, i, 0))
    sds = jax.ShapeDtypeStruct
    return pl.pallas_call(
        functools.partial(_mlstm_pre_kernel, chunk=chunk, n_valid=n_valid),
        out_shape=[sds((NB, S, W), BF16)] * 3 + [sds((NB, W, S), BF16), sds((NB, S, W), F32),
                                                 sds((NB, S, LANES), F32), sds((NB, 2 * HALO, S), F32)],
        grid=(NB, S // tm),
        in_specs=[row_blk(W), pl.BlockSpec((1, 1, HALO, W), lambda b, i: (b, i, 0, 0)),
                  const(conv_w), const(conv_b), const(wq), const(wk), const(wv), const(wvt), const(wg), const(bg)],
        out_specs=[row_blk(W)] * 3 + [pl.BlockSpec((1, W, tm), lambda b, i: (b, 0, i)), row_blk(W), row_blk(LANES),
                                      pl.BlockSpec((1, 2 * HALO, tm), lambda b, i: (b, 0, i))],
        scratch_shapes=[pltpu.VMEM((HALO + tm, W), F32)],
        compiler_params=_cparams("parallel", "parallel"),
        name="mlstm_pre",
    )(xm, halo, conv_w, conv_b, wq, wk, wv, wvt, wg, bg)


def _mlstm_scan_kernel(q_ref, k_ref, v_ref, vt_ref, g_ref, gt_ref, xc_ref, z_ref, ng_ref, sk_ref, c0_ref, n0_ref, m0_ref,
                       o_ref, c_out, n_out, m_out, c_ref, n_ref, m_ref):
    L = q_ref.shape[1]
    h = pl.program_id(1)
    ci = pl.program_id(2)

    @pl.when(ci == 0)
    def _():
        c_ref[...] = c0_ref[0, 0]
        n_ref[...] = n0_ref[0, 0]
        m_ref[...] = m0_ref[0, 0]

    q, k, v = q_ref[0], k_ref[0], v_ref[0]
    col = lax.broadcasted_iota(jnp.int32, (L, LANES), 1)
    b_col = jnp.sum(jnp.where(col == G_CUM + h, g_ref[0], 0.0), axis=1, keepdims=True)
    i_row = gt_ref[0, pl.ds(G_IG + h, 1), :]
    b_row = gt_ref[0, pl.ds(G_CUM + h, 1), :]
    m = m_ref[0:1, 0:1]
    inter = b_col + m
    dmat = b_col - b_row + i_row
    tril = lax.broadcasted_iota(jnp.int32, (L, L), 0) >= lax.broadcasted_iota(jnp.int32, (L, L), 1)
    dmat = jnp.where(tril, dmat, NEG_INF)
    m_loc = jnp.maximum(inter, jnp.max(dmat, axis=1, keepdims=True))
    a = lax.dot_general(q, k, _NT, preferred_element_type=F32) * jnp.exp(dmat - m_loc)
    w_inter = jnp.exp(inter - m_loc)
    c_prev = c_ref[...]
    n_prev = n_ref[...]
    num = (jnp.dot(a.astype(BF16), v, preferred_element_type=F32)
           + w_inter * lax.dot_general(q, c_prev.astype(BF16), _NT, preferred_element_type=F32))
    qn = lax.dot_general(q, n_prev.astype(BF16), _NT, preferred_element_type=F32)[:, 0:1]
    den = jnp.sum(a, axis=1, keepdims=True) + w_inter * qn
    hc = num / jnp.maximum(jnp.abs(den), jnp.exp(-m_loc))
    b_end = b_row[:, L - 1:L]
    lg = b_end - b_row + i_row
    m_new = jnp.maximum(b_end + m, jnp.max(lg, axis=1, keepdims=True))
    wg = jnp.exp(lg - m_new)
    decay = jnp.exp(b_end + m - m_new)
    c_ref[...] = decay * c_prev + jnp.dot((vt_ref[0] * wg).astype(BF16), k, preferred_element_type=F32)
    n_ref[...] = decay * n_prev + jnp.dot(jnp.broadcast_to(wg, (n_ref.shape[0], L)).astype(BF16), k,
                                          preferred_element_type=F32)
    m_ref[...] = jnp.broadcast_to(m_new, m_ref.shape)
    mu = jnp.mean(hc, axis=1, keepdims=True)
    hz = hc - mu
    var = jnp.mean(hz * hz, axis=1, keepdims=True)
    hn = hz * lax.rsqrt(var + LN_EPS) * ng_ref[...]
    z = z_ref[0]
    o_ref[0] = ((hn + sk_ref[...] * xc_ref[0]) * (z * jax.nn.sigmoid(z))).astype(o_ref.dtype)

    @pl.when(ci == pl.num_programs(2) - 1)
    def _():
        c_out[0, 0] = c_ref[...]
        n_out[0, 0] = n_ref[...]
        m_out[0, 0] = m_ref[...]


def mlstm_scan(q, k, v, vt, g, gt, xc, z, norm_g, skip, c0, n0, m0, chunk):
    NB, S, W = q.shape
    hd = MLSTM_HEAD_DIM
    H = W // hd
    seq = lambda: pl.BlockSpec((1, chunk, hd), lambda b, h, c: (b, c, h))
    vec = pl.BlockSpec((1, hd), lambda b, h, c: (0, h))
    st = lambda r, w: pl.BlockSpec((1, 1, r, w), lambda b, h, c: (b, h, 0, 0))
    sds = jax.ShapeDtypeStruct
    return pl.pallas_call(
        _mlstm_scan_kernel,
        out_shape=[sds((NB, S, W), BF16), sds((NB, H, hd, hd), F32), sds((NB, H, HALO, hd), F32), sds((NB, H, HALO, LANES), F32)],
        grid=(NB, H, S // chunk),
        in_specs=[seq(), seq(), seq(), pl.BlockSpec((1, hd, chunk), lambda b, h, c: (b, h, c)),
                  pl.BlockSpec((1, chunk, LANES), lambda b, h, c: (b, c, 0)),
                  pl.BlockSpec((1, 2 * HALO, chunk), lambda b, h, c: (b, 0, c)),
                  seq(), seq(), vec, vec, st(hd, hd), st(HALO, hd), st(HALO, LANES)],
        out_specs=[seq(), st(hd, hd), st(HALO, hd), st(HALO, LANES)],
        scratch_shapes=[pltpu.VMEM((hd, hd), F32), pltpu.VMEM((HALO, hd), F32), pltpu.VMEM((HALO, LANES), F32)],
        compiler_params=_cparams("parallel", "parallel", "arbitrary"),
        name="mlstm_scan",
    )(q, k, v, vt, g, gt, xc, z, norm_g.reshape(1, W), skip.reshape(1, W), c0, n0, m0)


SAMPLE_Q_ROWS = 16
TM_PROMPT = 512
TM_PEER_ROUTE = 256
TM_PEER = 512
TM_MLSTM = 256
CHUNK_MLSTM = 256


def _pad_rows(a, rows):
    return jnp.pad(a, ((0, 0), (0, rows - a.shape[1]), (0, 0)))


def _pad_cols(a, cols):
    return jnp.pad(a, ((0, 0),) * (a.ndim - 1) + ((0, cols - a.shape[-1]),))


def _dsa_layer(xp, xs, mp, ms, w_in, w_o, ln_g, ln_b, cache_kv, cache_ki, layer, page_table, DB, T):
    B, S, D = xp.shape
    wb = w_in.astype(BF16)
    o1, o2, o3, o4 = D, D + KV_WIDTH, D + 2 * KV_WIDTH, D + 2 * KV_WIDTH + IDX_HEADS * IDX_DIM
    weights = [wb[:, :o1], wb[:, o1:o3], wb[:, o1:o2], wb[:, o2:o3], wb[:, o3:o4], wb[:, o4:o4 + IDX_DIM],
               _pad_cols(wb[:, o4 + IDX_DIM:], LANES)]
    outs = [(0, BF16), (1, F32), (2, BF16), (3, BF16), (4, BF16), (5, F32), (5, BF16), (6, F32)]
    w_ob = w_o.astype(BF16)
    q, kv32, kb, vb, qi, ki32, kib, wi = proj(xp, mp(0, 0), mp(0, 1), weights, outs, TM_PROMPT)
    o = dsa_prompt_attend(q, qi, wi, kb, vb, kib)
    xp = outproj_postnorm(o, w_ob, xp, mp(0, 2), ln_g, ln_b, TM_PROMPT)
    kv_p = kv32.reshape(B, S, 2, N_KV_HEADS, HEAD_DIM)
    n = DB * T
    q, kv32s, kb, vb, qi, ki32s, kib, wi = proj(xs, ms(0, 0), ms(0, 1), weights, outs, n)
    per_b = lambda a: a.reshape(DB, T, a.shape[-1])
    kv_new = _pad_rows(jnp.concatenate([per_b(kb), per_b(vb)], axis=-1), PAGE_SIZE)
    o = dsa_sample_attend(_pad_rows(per_b(q), SAMPLE_Q_ROWS), _pad_rows(per_b(qi), SAMPLE_Q_ROWS),
                          _pad_rows(per_b(wi), SAMPLE_Q_ROWS), kv_new, _pad_rows(per_b(kib), PAGE_SIZE),
                          cache_kv.reshape(cache_kv.shape[:3] + (2 * KV_WIDTH,)), cache_ki, layer, page_table, T)
    xs = outproj_postnorm(o[:, :T].reshape(1, n, D), w_ob, xs, ms(0, 2), ln_g, ln_b, n)
    kv_s = kv32s.reshape(DB, T, 2, N_KV_HEADS, HEAD_DIM)
    return xp, xs, kv_p, kv_s, ki32, ki32s.reshape(DB, T, IDX_DIM)


def _nsa_layer(xp, xs, mp, ms, w_in, cmp_pe, cmp_w1, cmp_w2, w_o, ln_g, ln_b, cache_kv, win_state, layer,
               page_table, DB, T):
    B, S, D = xp.shape
    wb = w_in.astype(BF16)
    c1, c2, c3 = D + 4 * KV_WIDTH, D + 6 * KV_WIDTH, D + 2 * KV_WIDTH
    weights = [wb[:, :D], wb[:, D:c1], wb[:, c1:c2], wb[:, c3:c2], _pad_cols(wb[:, c2:], LANES)]
    outs = [(0, BF16), (1, F32), (2, F32), (3, BF16), (4, F32)]
    w_ob = w_o.astype(BF16)
    cmp_w = nsa_compress_weights(cmp_pe, cmp_w1, cmp_w2)
    q, cs32, win32, selwin, gates = proj(xp, mp(0, 0), mp(0, 1), weights, outs, TM_PROMPT)
    pages = S // PAGE_SIZE
    ident = jnp.arange(B * pages, dtype=jnp.int32).reshape(B, pages)
    kvc = nsa_compress(cs32.reshape(1, B * pages, PAGE_SIZE, 4 * KV_WIDTH), 0, ident, *cmp_w)
    o = nsa_prompt_attend(q, gates, kvc, selwin)
    xp = outproj_postnorm(o, w_ob, xp, mp(0, 2), ln_g, ln_b, TM_PROMPT)
    kv_p = cs32.reshape(B, S, 2, 2, N_KV_HEADS, HEAD_DIM)
    keep = min(WINDOW, S)
    win_p = win32[:, S - keep:].reshape(B, keep, 2, N_KV_HEADS, HEAD_DIM)
    n = DB * T
    q, cs32s, win32s, selwin, gates = proj(xs, ms(0, 0), ms(0, 1), weights, outs, n)
    per_b = lambda a: a.reshape(DB, T, a.shape[-1])
    kvc = nsa_compress(cache_kv.reshape(cache_kv.shape[:3] + (4 * KV_WIDTH,)), layer, page_table, *cmp_w)
    selwin = per_b(selwin)
    wbuf = win_state[layer]
    o = nsa_sample_attend(_pad_rows(per_b(q), SAMPLE_Q_ROWS), _pad_rows(per_b(gates), SAMPLE_Q_ROWS), kvc,
                          _pad_rows(selwin[..., :2 * KV_WIDTH], PAGE_SIZE), wbuf.reshape(DB, wbuf.shape[1], 2 * KV_WIDTH),
                          _pad_rows(selwin[..., 2 * KV_WIDTH:], PAGE_SIZE),
                          cache_kv.reshape(cache_kv.shape[:3] + (4 * KV_WIDTH,)), layer, page_table, T)
    xs = outproj_postnorm(o[:, :T].reshape(1, n, D), w_ob, xs, ms(0, 2), ln_g, ln_b, n)
    kv_s = cs32s.reshape(DB, T, 2, 2, N_KV_HEADS, HEAD_DIM)
    win_s = jnp.concatenate([wbuf, win32s.reshape(DB, T, 2, N_KV_HEADS, HEAD_DIM)], axis=1)[:, T:]
    return xp, xs, kv_p, kv_s, win_p, win_s


def _mlstm_layer(xp, xs, mp, ms, w_in, conv_w, conv_b, w_qkv, w_gate, b_gate, norm_g, skip, w_o, ln_g, ln_b,
                 conv_state, c_state, n_state, m_state, DB, T):
    B, S, D = xp.shape
    W, H, hd = MLSTM_INNER, MLSTM_HEADS, MLSTM_HEAD_DIM
    wb = w_in.astype(BF16)
    weights = [wb[:, :W], wb[:, W:]]
    outs = [(0, F32), (1, F32)]
    wq, wk, wv = (w_qkv[i].astype(BF16) for i in range(3))
    pre_w = (conv_w, conv_b.reshape(1, W), wq, wk, wv, jnp.swapaxes(wv, 1, 2), _pad_cols(w_gate, LANES).astype(BF16),
             _pad_cols(b_gate.reshape(1, -1), LANES))
    w_ob = w_o.astype(BF16)
    keep = CONV_WIDTH - 1
    rep = lambda a, r: jnp.broadcast_to(a[..., None, :], a.shape[:-1] + (r, a.shape[-1]))
    xm, z = proj(xp, mp(0, 0), mp(0, 1), weights, outs, TM_PROMPT)
    tm = min(TM_MLSTM, S)
    tiles = xm.reshape(B, S // tm, tm, W)
    halo = jnp.concatenate([jnp.zeros((B, 1, HALO, W), F32), tiles[:, :-1, tm - HALO:]], axis=1)
    chunk = min(CHUNK_MLSTM, S)
    q, k, v, vt, xc, g, gt = mlstm_pre(xm, halo, *pre_w, tm=tm, chunk=chunk, n_valid=S)
    zeros = lambda *s: jnp.zeros(s, F32)
    o, c_p, n_p, m_p = mlstm_scan(q, k, v, vt, g, gt, xc, z, norm_g, skip, zeros(B, H, hd, hd), zeros(B, H, HALO, hd),
                                  zeros(B, H, HALO, LANES), chunk)
    xp = outproj_postnorm(o, w_ob, xp, mp(0, 2), ln_g, ln_b, TM_PROMPT)
    conv_p = jnp.concatenate([zeros(B, keep, W), xm], axis=1)[:, -keep:]
    n = DB * T
    xm_s, z_s = proj(xs, ms(0, 0), ms(0, 1), weights, outs, n)
    xm_s = xm_s.reshape(DB, T, W)
    rows = LANES
    halo = jnp.concatenate([zeros(DB, HALO - keep, W), conv_state], axis=1)[:, None]
    q, k, v, vt, xc, g, gt = mlstm_pre(_pad_rows(xm_s, rows), halo, *pre_w, tm=rows, chunk=rows, n_valid=T)
    m0 = jnp.broadcast_to(m_state[..., None, None], (DB, H, HALO, LANES))
    o, c_s, n_s, m_s = mlstm_scan(q, k, v, vt, g, gt, xc, _pad_rows(z_s.reshape(DB, T, W), rows), norm_g, skip,
                                  c_state, rep(n_state, HALO), m0, rows)
    xs = outproj_postnorm(o[:, :T].reshape(1, n, W), w_ob, xs, ms(0, 2), ln_g, ln_b, n)
    conv_s = jnp.concatenate([conv_state, xm_s], axis=1)[:, -keep:]
    return (xp, xs, conv_p, conv_s, c_p, c_s, n_p[:, :, 0], n_s[:, :, 0], m_p[:, :, 0, 0], m_s[:, :, 0, 0])


def kernel(x_prompt, x_sample, cache_a_kv, cache_a_kidx, cache_b_kv, state_b_win, state_c_conv, state_c_C, state_c_n,
           state_c_m, page_table, c_prompt, c_sample, a_w_in, a_w_o, b_w_in, b_cmp_pe, b_cmp_w1, b_cmp_w2, b_w_o,
           c_w_in, c_conv_w, c_conv_b, c_w_qkv, c_w_gate, c_b_gate, c_norm_g, c_skip, c_w_o,
           ada_w, ada_b, ln_g, ln_b, peer_w_q, peer_sub_keys, peer_u, peer_v):
    B, S, D = x_prompt.shape
    DB, T, _ = x_sample.shape
    n = DB * T
    n_cond = B + DB
    cond = _pad_rows(jnp.concatenate([c_prompt, c_sample], axis=0)[None], -(-n_cond // 8) * 8)[0]
    mods = adaln_all(cond, ada_w, ada_b).reshape(DEPTH, cond.shape[0], 2, 3, D)
    xp, xs = x_prompt, x_sample.reshape(1, n, D)
    outs = {name: [] for name in ("a_kv_p", "a_kv_s", "a_ki_p", "a_ki_s", "b_kv_p", "b_kv_s", "b_win_p", "b_win_s",
                                  "conv_p", "conv_s", "C_p", "C_s", "n_p", "n_s", "m_p", "m_s")}
    for i in range(DEPTH):
        kind, j = i % N_MIXERS, i // N_MIXERS
        mod_p = mods[i, :B]
        mod_s = jnp.repeat(mods[i, B:n_cond], T, axis=0)
        mp = lambda s, r, mod_p=mod_p: mod_p[:, s, r][:, None, :]
        ms = lambda s, r, mod_s=mod_s: mod_s[:, s, r][None]
        if kind == 0:
            xp, xs, kvp, kvs, kip, kis = _dsa_layer(xp, xs, mp, ms, a_w_in[j], a_w_o[j], ln_g[i, 0], ln_b[i, 0],
                                                    cache_a_kv, cache_a_kidx, j, page_table, DB, T)
            for name, val in zip(("a_kv_p", "a_kv_s", "a_ki_p", "a_ki_s"), (kvp, kvs, kip, kis)):
                outs[name].append(val)
        elif kind == 1:
            xp, xs, kvp, kvs, wp, ws = _nsa_layer(xp, xs, mp, ms, b_w_in[j], b_cmp_pe[j], b_cmp_w1[j], b_cmp_w2[j],
                                                  b_w_o[j], ln_g[i, 0], ln_b[i, 0], cache_b_kv, state_b_win, j,
                                                  page_table, DB, T)
            for name, val in zip(("b_kv_p", "b_kv_s", "b_win_p", "b_win_s"), (kvp, kvs, wp, ws)):
                outs[name].append(val)
        else:
            res = _mlstm_layer(xp, xs, mp, ms, c_w_in[j], c_conv_w[j], c_conv_b[j], c_w_qkv[j], c_w_gate[j], c_b_gate[j],
                               c_norm_g[j], c_skip[j], c_w_o[j], ln_g[i, 0], ln_b[i, 0],
                               state_c_conv[j], state_c_C[j], state_c_n[j], state_c_m[j], DB, T)
            xp, xs = res[0], res[1]
            for name, val in zip(("conv_p", "conv_s", "C_p", "C_s", "n_p", "n_s", "m_p", "m_s"), res[2:]):
                outs[name].append(val)
        peer_w = (peer_w_q[i].astype(BF16), peer_sub_keys[i].astype(BF16), peer_u[i].astype(BF16).T,
                  peer_v[i].astype(BF16), ln_g[i, 1], ln_b[i, 1])
        xp = peer_layer(xp, mp(1, 0), mp(1, 1), mp(1, 2), *peer_w, tm_route=min(TM_PEER_ROUTE, S), tm=min(TM_PEER, S))
        xs = peer_layer(xs, ms(1, 0), ms(1, 1), ms(1, 2), *peer_w, tm_route=n, tm=n)
    return (xp, xs.reshape(DB, T, D)) + tuple(jnp.stack(outs[name]) for name in outs)
```

```python
import functools
import math

import jax
import jax.numpy as jnp
from jax import lax
from jax.experimental import pallas as pl
from jax.experimental.pallas import tpu as pltpu

F32 = jnp.float32
BF16 = jnp.bfloat16

D_MODEL = 1024
DEPTH = 4
PAGE_SIZE = 128
N_MIXERS = 3

N_HEADS = 16
HEAD_DIM = D_MODEL // N_HEADS
N_KV_HEADS = 4
GROUP = N_HEADS // N_KV_HEADS
KV_WIDTH = N_KV_HEADS * HEAD_DIM
Q_BLOCK = 128
ATTN_SCALE = HEAD_DIM ** -0.5

IDX_HEADS = 8
IDX_DIM = 64
IDX_SCALE = (IDX_HEADS * IDX_DIM) ** -0.5
DSA_TOPK = 256

CMP_LEN = 32
CMP_HIDDEN = 2 * HEAD_DIM
SEL_BLOCK = 64
N_SEL = 16
WINDOW = 512

MLSTM_INNER = 2 * D_MODEL
MLSTM_HEADS = 4
MLSTM_HEAD_DIM = MLSTM_INNER // MLSTM_HEADS
CONV_WIDTH = 4
MLSTM_CHUNK = 64

PEER_HEADS = 8
PEER_KEYS = 128
N_EXPERTS = PEER_KEYS * PEER_KEYS
PEER_KEY_DIM = 256
PEER_TOPK = 16

ALPHA = (2 * DEPTH) ** 0.25
LN_EPS = 1e-5

LANES = 128
VMEM_LIMIT = 56 * 1024 * 1024

_NT = (((1,), (1,)), ((), ()))


def _cparams(*sem):
    return pltpu.CompilerParams(dimension_semantics=sem, vmem_limit_bytes=VMEM_LIMIT)


def _gelu(x):
    return 0.5 * x * (1.0 + jnp.tanh(math.sqrt(2.0 / math.pi) * (x + 0.044715 * (x * x * x))))


def _post_norm_math(x, y, gate, g, b):
    z = ALPHA * x + gate * y
    mu = jnp.mean(z, axis=-1, keepdims=True)
    zc = z - mu
    var = jnp.mean(zc * zc, axis=-1, keepdims=True)
    return zc * lax.rsqrt(var + LN_EPS) * g + b


def _adaln_kernel(c_ref, w_ref, b_ref, o_ref):
    c = c_ref[...]
    s = (c * jax.nn.sigmoid(c)).astype(BF16)
    o_ref[0] = jnp.dot(s, w_ref[0].astype(BF16), preferred_element_type=F32) + b_ref[0]


def adaln_all(c, ada_w, ada_b):
    M = c.shape[0]
    n = ada_w.shape[-1] // D_MODEL
    return pl.pallas_call(
        _adaln_kernel,
        out_shape=jax.ShapeDtypeStruct((DEPTH, M, n * D_MODEL), F32),
        grid=(DEPTH, n),
        in_specs=[pl.BlockSpec((M, D_MODEL), lambda i, j: (0, 0)),
                  pl.BlockSpec((1, D_MODEL, D_MODEL), lambda i, j: (i, 0, j)),
                  pl.BlockSpec((1, 1, D_MODEL), lambda i, j: (i, 0, j))],
        out_specs=pl.BlockSpec((1, M, D_MODEL), lambda i, j: (i, 0, j)),
        compiler_params=_cparams("arbitrary", "arbitrary"),
        name="adaln",
    )(c, ada_w, ada_b.reshape(DEPTH, 1, -1))


def _proj_kernel(x_ref, sh_ref, sc_ref, *refs, n_w, n_t, out_map):
    w_refs, wt_refs, o_refs = refs[:n_w], refs[n_w:n_w + n_t], refs[n_w + n_t:]
    h = (x_ref[0] * (1.0 + sc_ref[0]) + sh_ref[0]).astype(BF16)
    done = {}
    for o_ref, wi in zip(o_refs, out_map):
        if wi not in done:
            done[wi] = jnp.dot(h, w_refs[wi][...], preferred_element_type=F32)
        o_ref[0] = done[wi].astype(o_ref.dtype)
    for o_ref, wt_ref in zip(o_refs[len(out_map):], wt_refs):
        o_ref[0] = lax.dot_general(wt_ref[...], h, _NT, preferred_element_type=F32).astype(o_ref.dtype)


def proj(x, shift, scale, weights, outs, tm, weights_t=()):
    nb, S, D = x.shape
    rows = shift.shape[1]
    mblk = (1, tm, D) if rows == S else (1, 1, D)
    mmap = (lambda b, i: (b, i, 0)) if rows == S else (lambda b, i: (b, 0, 0))
    in_specs = [pl.BlockSpec((1, tm, D), lambda b, i: (b, i, 0)),
                pl.BlockSpec(mblk, mmap), pl.BlockSpec(mblk, mmap)]
    in_specs += [pl.BlockSpec(w.shape, lambda b, i: (0, 0)) for w in (*weights, *weights_t)]
    out_shape = [jax.ShapeDtypeStruct((nb, S, weights[wi].shape[1]), dt) for wi, dt in outs]
    out_specs = [pl.BlockSpec((1, tm, weights[wi].shape[1]), lambda b, i: (b, i, 0)) for wi, _ in outs]
    out_shape += [jax.ShapeDtypeStruct((nb, w.shape[0], S), BF16) for w in weights_t]
    out_specs += [pl.BlockSpec((1, w.shape[0], tm), lambda b, i: (b, 0, i)) for w in weights_t]
    return pl.pallas_call(
        functools.partial(_proj_kernel, n_w=len(weights), n_t=len(weights_t), out_map=tuple(wi for wi, _ in outs)),
        out_shape=out_shape, grid=(nb, S // tm), in_specs=in_specs, out_specs=out_specs,
        compiler_params=_cparams("parallel", "parallel"),
        name="proj",
    )(x, shift, scale, *weights, *weights_t)


def _outproj_kernel(o_ref, w_ref, x_ref, gate_ref, g_ref, b_ref, y_ref):
    y = jnp.dot(o_ref[0], w_ref[...], preferred_element_type=F32)
    y_ref[0] = _post_norm_math(x_ref[0], y, gate_ref[0], g_ref[...], b_ref[...])


def outproj_postnorm(o, w, x, gate, g, b, tm):
    nb, S, K = o.shape
    D = x.shape[-1]
    rows = gate.shape[1]
    mblk = (1, tm, D) if rows == S else (1, 1, D)
    mmap = (lambda bi, i: (bi, i, 0)) if rows == S else (lambda bi, i: (bi, 0, 0))
    return pl.pallas_call(
        _outproj_kernel,
        out_shape=jax.ShapeDtypeStruct(x.shape, F32),
        grid=(nb, S // tm),
        in_specs=[pl.BlockSpec((1, tm, K), lambda bi, i: (bi, i, 0)),
                  pl.BlockSpec((K, D), lambda bi, i: (0, 0)),
                  pl.BlockSpec((1, tm, D), lambda bi, i: (bi, i, 0)),
                  pl.BlockSpec(mblk, mmap),
                  pl.BlockSpec((1, D), lambda bi, i: (0, 0)),
                  pl.BlockSpec((1, D), lambda bi, i: (0, 0))],
        out_specs=pl.BlockSpec((1, tm, D), lambda bi, i: (bi, i, 0)),
        compiler_params=_cparams("parallel", "parallel"),
        name="outproj_postnorm",
    )(o, w, x, gate, g.reshape(1, D), b.reshape(1, D))


def _top_rows(x, iota, n):
    vals, idxs = [], []
    for _ in range(n):
        m = jnp.max(x, axis=0, keepdims=True)
        idx = jnp.min(jnp.where(x == m, iota, jnp.inf), axis=0, keepdims=True)
        vals.append(m)
        idxs.append(idx)
        x = jnp.where(iota == idx, -jnp.inf, x)
    return vals, idxs


def _peer_route_kernel(x_ref, sh_ref, sc_ref, wq_ref, keys_ref, i1_ref, i2_ref, g_ref):
    tm = x_ref.shape[1]
    h = (x_ref[0] * (1.0 + sc_ref[0]) + sh_ref[0]).astype(BF16)
    q = jnp.dot(h, wq_ref[...], preferred_element_type=F32).astype(BF16)
    iota_k = lax.broadcasted_iota(jnp.int32, (PEER_KEYS, tm), 0).astype(F32)
    half = PEER_KEY_DIM // 2
    n = PEER_TOPK
    row16 = lax.broadcasted_iota(jnp.int32, (n, tm), 0).astype(F32)
    row8 = lax.broadcasted_iota(jnp.int32, (8, tm), 0).astype(F32)
    pieces = [(0, 1, 16, 0, 15), (1, 1, 8, 0, 7), (2, 1, 8, 0, 4), (3, 1, 8, 0, 3),
              (0, 2, 16, 4, 15), (1, 2, 8, 4, 7), (2, 2, 8, 4, 4)]
    ids = []
    for fixed, which, rows, lo, hi in pieces:
        r = row16 if rows == 16 else row8
        pair = fixed * n + r if which == 1 else r * n + fixed
        ids.append(jnp.where(r < lo, float(n * n), jnp.where(r > hi, float(n * n), pair)))
    cand_id = jnp.concatenate(ids, axis=0)
    cand_ok = cand_id < float(n * n)
    i1_rows, i2_rows, g_rows = [], [], []
    for hd in range(PEER_HEADS):
        tops = []
        for c in range(2):
            col = (hd * 2 + c) * half
            s_t = lax.dot_general(keys_ref[c], q[:, col:col + half], _NT, preferred_element_type=F32)
            tops.append(_top_rows(s_t, iota_k, n))
        (v1, id1), (v2, id2) = tops
        stacked = {(w, rows): jnp.concatenate((v1, v2)[w - 1][:rows], axis=0) for w in (1, 2) for rows in (8, n)}
        sums = [(v1[fixed] + stacked[2, rows]) if which == 1 else (stacked[1, rows] + v2[fixed])
                for fixed, which, rows, _, _ in pieces]
        cand = jnp.where(cand_ok, jnp.concatenate(sums, axis=0), NEG_INF)
        cvals, cidx = _top_rows(cand, cand_id, n)
        cv = jnp.concatenate(cvals, axis=0)
        ci = jnp.concatenate(cidx, axis=0)
        r1 = jnp.floor(ci * (1.0 / PEER_TOPK))
        r2 = ci - r1 * PEER_TOPK
        i1 = jnp.zeros_like(ci)
        i2 = jnp.zeros_like(ci)
        for r in range(PEER_TOPK):
            i1 = i1 + jnp.where(r1 == float(r), id1[r], 0.0)
            i2 = i2 + jnp.where(r2 == float(r), id2[r], 0.0)
        e = jnp.exp(cv - cvals[0])
        gate = e / jnp.sum(e, axis=0, keepdims=True)
        i1_rows.append(i1)
        i2_rows.append(i2)
        g_rows.append(gate)
    i1_ref[0] = jnp.concatenate(i1_rows, axis=0).T
    i2_ref[0] = jnp.concatenate(i2_rows, axis=0).T
    g_ref[0] = jnp.concatenate(g_rows, axis=0).T


def _mod_specs(rows, S, tm, D):
    if rows == S:
        return pl.BlockSpec((1, tm, D), lambda b, i, *_: (b, i, 0))
    return pl.BlockSpec((1, 1, D), lambda b, i, *_: (b, 0, 0))


def peer_route(x, shift, scale, wq, keys, tm):
    nb, S, D = x.shape
    nsel = PEER_HEADS * PEER_TOPK
    mspec = _mod_specs(shift.shape[1], S, tm, D)
    return pl.pallas_call(
        _peer_route_kernel,
        out_shape=[jax.ShapeDtypeStruct((nb, S, nsel), F32)] * 3,
        grid=(nb, S // tm),
        in_specs=[pl.BlockSpec((1, tm, D), lambda b, i: (b, i, 0)), mspec, mspec,
                  pl.BlockSpec(wq.shape, lambda b, i: (0, 0)),
                  pl.BlockSpec(keys.shape, lambda b, i: (0, 0, 0))],
        out_specs=[pl.BlockSpec((1, tm, nsel), lambda b, i: (b, i, 0))] * 3,
        compiler_params=_cparams("parallel", "parallel"),
        name="peer_route",
    )(x, shift, scale, wq, keys)


PEER_TOKEN_GROUP = 16


def _peer_expert_kernel(x_ref, sh_ref, sc_ref, gt_ref, i1_ref, i2_ref, g_ref, ut_ref, v_ref, lg_ref, lb_ref,
                        y_ref, w3_ref, acc_ref, h_ref, *, ac):
    tm = x_ref.shape[1]
    j = pl.program_id(2)

    @pl.when(j == 0)
    def _():
        h_ref[...] = (x_ref[0] * (1.0 + sc_ref[0]) + sh_ref[0]).astype(BF16)
        acc_ref[...] = jnp.zeros_like(acc_ref)
        iota_s = lax.broadcasted_iota(jnp.int32, (PEER_KEYS, LANES), 0).astype(F32)

        def build(gi, carry):
            t0 = pl.multiple_of(gi * PEER_TOKEN_GROUP, PEER_TOKEN_GROUP)
            i1g = i1_ref[0, pl.ds(t0, PEER_TOKEN_GROUP), :]
            i2g = i2_ref[0, pl.ds(t0, PEER_TOKEN_GROUP), :]
            gg = g_ref[0, pl.ds(t0, PEER_TOKEN_GROUP), :]
            tiles = []
            for t in range(PEER_TOKEN_GROUP):
                g1t = jnp.where(iota_s == i1g[t:t + 1], gg[t:t + 1], 0.0).astype(BF16)
                o2t = jnp.where(iota_s == i2g[t:t + 1], 1.0, 0.0).astype(BF16)
                tiles.append(lax.dot_general(g1t, o2t, _NT, preferred_element_type=F32))
            w3_ref[:, pl.ds(t0, PEER_TOKEN_GROUP), :] = jnp.swapaxes(jnp.stack(tiles, axis=0), 0, 1).astype(BF16)
            return carry

        lax.fori_loop(0, tm // PEER_TOKEN_GROUP, build, 0)

    act = _gelu(jnp.dot(h_ref[...], ut_ref[...], preferred_element_type=F32))
    wc = jnp.concatenate([w3_ref[j * ac + a] for a in range(ac)], axis=1).astype(F32)
    coef = (wc * act).astype(BF16)
    acc_ref[...] += jnp.dot(coef, v_ref[...], preferred_element_type=F32)

    @pl.when(j == pl.num_programs(2) - 1)
    def _():
        y_ref[0] = _post_norm_math(x_ref[0], acc_ref[...], gt_ref[0], lg_ref[...], lb_ref[...])


def peer_experts(x, shift, scale, gate, i1, i2, g, ut, v, ln_g, ln_b, tm, ac):
    nb, S, D = x.shape
    assert tm % PEER_TOKEN_GROUP == 0 and S % tm == 0, (S, tm)
    nsel = PEER_HEADS * PEER_TOPK
    ce = ac * PEER_KEYS
    mspec = _mod_specs(shift.shape[1], S, tm, D)
    pick =pl.BlockSpec((1, tm, nsel), lambda b, i, j: (b, i, 0))
    return pl.pallas_call(
        functools.partial(_peer_expert_kernel, ac=ac),
        out_shape=jax.ShapeDtypeStruct(x.shape, F32),
        grid=(nb, S // tm, PEER_KEYS // ac),
        in_specs=[pl.BlockSpec((1, tm, D), lambda b, i, j: (b, i, 0)), mspec, mspec, mspec, pick, pick, pick,
                  pl.BlockSpec((D, ce), lambda b, i, j: (0, j)),
                  pl.BlockSpec((ce, D), lambda b, i, j: (j, 0)),
                  pl.BlockSpec((1, D), lambda b, i, j: (0, 0)),
                  pl.BlockSpec((1, D), lambda b, i, j: (0, 0))],
        out_specs=pl.BlockSpec((1, tm, D), lambda b, i, j: (b, i, 0)),
        scratch_shapes=[pltpu.VMEM((PEER_KEYS, tm, PEER_KEYS), BF16),
                        pltpu.VMEM((tm, D), F32),
                        pltpu.VMEM((tm, D), BF16)],
        compiler_params=_cparams("parallel", "parallel", "arbitrary"),
        name="peer_experts",
    )(x, shift, scale, gate, i1, i2, g, ut, v, ln_g.reshape(1, D), ln_b.reshape(1, D))


def peer_layer(x, shift, scale, gate, wq, keys, ut, v, ln_g, ln_b, tm_route, tm):
    i1, i2, g = peer_route(x, shift, scale, wq, keys, tm_route)
    return peer_experts(x, shift, scale, gate, i1, i2, g, ut, v, ln_g, ln_b, tm, ac=8)


INT_MIN = -2 ** 31
NEG_INF = float("-inf")


def _to_key(x):
    b = lax.bitcast_convert_type(x, jnp.int32)
    return b ^ ((b >> 31) & 0x7FFFFFFF)


COUNT_ROWS = 64


def _key_tile(key_ref, c0, kc, axis):
    return key_ref[:, pl.ds(c0, kc)] if axis == 1 else key_ref[pl.ds(c0, kc), :]


def _count_keys(key_ref, nck, kc, pred, axis):
    other = key_ref.shape[1 - axis]

    def body(c, acc):
        c0 = pl.multiple_of(c * kc, kc)
        hit = jnp.where(pred(_key_tile(key_ref, c0, kc, axis), c0), 1.0, 0.0)
        if axis == 1:
            part = hit[:, 0:LANES]
            for u in range(1, kc // LANES):
                part = part + hit[:, u * LANES:(u + 1) * LANES]
        else:
            part = jnp.sum(hit.reshape(kc // COUNT_ROWS, COUNT_ROWS, other), axis=0)
        return acc + part

    acc = lax.fori_loop(0, nck, body, jnp.zeros((other, LANES) if axis == 1 else (COUNT_ROWS, other), F32))
    return jnp.sum(acc, axis=axis, keepdims=True)


def _topk_threshold(key_ref, nck, kc, k, idx_bits, axis=1):
    other = key_ref.shape[1 - axis]
    vec = (other, 1) if axis == 1 else (1, other)
    kf = float(k)

    def bit_step(p, t_u):
        cand_u = t_u | jnp.left_shift(jnp.int32(1), 31 - p)
        cand_s = cand_u ^ INT_MIN
        cnt = _count_keys(key_ref, nck, kc, lambda tile, c0: tile >= cand_s, axis)
        return jnp.where(cnt >= kf, cand_u, t_u)

    t_u = lax.fori_loop(0, 32, bit_step, jnp.zeros(vec, jnp.int32))
    thr = jnp.maximum(t_u ^ INT_MIN, INT_MIN + 1)
    n_ge = _count_keys(key_ref, nck, kc, lambda tile, c0: tile >= thr, axis)

    @pl.when(jnp.max(n_ge) > kf)
    def _():
        need = kf - _count_keys(key_ref, nck, kc, lambda tile, c0: tile > thr, axis)
        iota = lax.broadcasted_iota(jnp.int32, (other, kc) if axis == 1 else (kc, other), axis)

        def idx_step(p, j_hi):
            cand = j_hi | jnp.left_shift(jnp.int32(1), idx_bits - 1 - p)
            cnt = _count_keys(key_ref, nck, kc, lambda tile, c0: (tile == thr) & (c0 + iota < cand), axis)
            return jnp.where(cnt <= need, cand, j_hi)

        j_hi = lax.fori_loop(0, idx_bits, idx_step, jnp.zeros(vec, jnp.int32))
        surplus = n_ge > kf

        def lower(c, carry):
            c0 = pl.multiple_of(c * kc, kc)
            tile = _key_tile(key_ref, c0, kc, axis)
            drop = (tile == thr) & (c0 + iota >= j_hi) & surplus
            tile = jnp.where(drop, thr - 1, tile)
            if axis == 1:
                key_ref[:, pl.ds(c0, kc)] = tile
            else:
                key_ref[pl.ds(c0, kc), :] = tile
            return carry

        lax.fori_loop(0, nck, lower, 0)

    return thr


def _stack_heads(q_ref, qs_ref, tq):
    for hd in range(N_HEADS):
        qs_ref[hd * tq:(hd + 1) * tq, :] = (q_ref[0, :, hd * HEAD_DIM:(hd + 1) * HEAD_DIM] * ATTN_SCALE).astype(BF16)


def _flash_init(m_ref, l_ref, acc_ref):
    m_ref[...] = jnp.full(m_ref.shape, NEG_INF, F32)
    l_ref[...] = jnp.zeros(l_ref.shape, F32)
    acc_ref[...] = jnp.zeros(acc_ref.shape, F32)


def _flash_result(rows, l_ref, acc_ref):
    return acc_ref[rows, :] / jnp.maximum(l_ref[rows, :HEAD_DIM], 1e-30)


def _grouped_flash_step(mask_of, kch, vch, qs_ref, m_ref, l_ref, acc_ref, tq):
    rows_h = GROUP * tq
    kc = kch.shape[0]
    parts = []
    for h in range(N_KV_HEADS):
        logits = lax.dot_general(qs_ref[h * rows_h:(h + 1) * rows_h, :], kch[:, h * HEAD_DIM:(h + 1) * HEAD_DIM],
                                 _NT, preferred_element_type=F32)
        parts.append(jnp.where(mask_of(h)[None], logits.reshape(GROUP, tq, kc), NEG_INF).reshape(rows_h, kc))
    s = jnp.concatenate(parts, axis=0)
    m_prev = m_ref[...]
    m_new = jnp.maximum(m_prev, jnp.max(s, axis=1, keepdims=True))
    m_safe = jnp.where(m_new == NEG_INF, 0.0, m_new)
    p = jnp.exp(s - jnp.concatenate([m_safe] * (kc // LANES), axis=1))
    alpha = jnp.exp(m_prev - m_safe)
    l_ref[...] = alpha * l_ref[...] + jnp.sum(p, axis=1, keepdims=True)
    pb = p.astype(BF16)
    pv = [jnp.dot(pb[h * rows_h:(h + 1) * rows_h], vch[:, h * HEAD_DIM:(h + 1) * HEAD_DIM],
                  preferred_element_type=F32) for h in range(N_KV_HEADS)]
    acc_ref[...] = alpha[:, :HEAD_DIM] * acc_ref[...] + jnp.concatenate(pv, axis=0)
    m_ref[...] = m_new


def _index_scores(qi, wi, kic):
    sc = None
    for hh in range(IDX_HEADS):
        s = lax.dot_general(qi[:, hh * IDX_DIM:(hh + 1) * IDX_DIM], kic, _NT, preferred_element_type=F32)
        term = jnp.maximum(s, 0.0) * wi[:, hh:hh + 1]
        sc = term if sc is None else sc + term
    return sc


def _dsa_prompt_kernel(q_ref, qi_ref, wi_ref, k_ref, vt_ref, ki_ref, o_ref, key_ref, *fl, kc, topk, idx_bits):
    qs_ref = fl[0]
    tq = q_ref.shape[1]
    t0 = pl.program_id(1) * tq
    nck = (t0 + tq + kc - 1) // kc
    _stack_heads(q_ref, qs_ref, tq)
    qpos = t0 + lax.broadcasted_iota(jnp.int32, (1, tq), 1)
    kiota = lax.broadcasted_iota(jnp.int32, (kc, 1), 0)
    qi = qi_ref[0]
    wi = wi_ref[0] * IDX_SCALE

    def score(c, carry):
        c0 = pl.multiple_of(c * kc, kc)
        sc = _index_scores(qi, wi, ki_ref[0, pl.ds(c0, kc), :])
        key_ref[pl.ds(c0, kc), :] = jnp.where(c0 + kiota <= qpos, _to_key(sc.T), INT_MIN)
        return carry

    lax.fori_loop(0, nck, score, 0)
    thr = _topk_threshold(key_ref, nck, kc, topk, idx_bits, axis=0)
    _flash_init_t(*fl)

    def attend(c, carry):
        c0 = pl.multiple_of(c * kc, kc)
        bias = jnp.where(key_ref[pl.ds(c0, kc), :] >= thr, 0.0, NEG_INF)
        _flash_step_t(lambda h: bias, k_ref[0, pl.ds(c0, kc), :], vt_ref[0, :, pl.ds(c0, kc)], *fl, tq=tq)
        return carry

    lax.fori_loop(0, nck, attend, 0)
    for h in range(N_KV_HEADS):
        res = _flash_result_t(h, *fl)
        for g in range(GROUP):
            hd = h * GROUP + g
            o_ref[0, :, hd * HEAD_DIM:(hd + 1) * HEAD_DIM] = res[g * tq:(g + 1) * tq].astype(o_ref.dtype)


ONES_ROWS = 16
FLASH_MAX_ROWS = 16


def _flash_scratch_t(tq, kmax):
    cols = GROUP * tq
    return [pltpu.VMEM((N_HEADS * tq, HEAD_DIM), BF16),
            pltpu.VMEM((N_KV_HEADS, 1, cols), F32),
            pltpu.VMEM((N_KV_HEADS, HEAD_DIM + ONES_ROWS, cols), F32),
            pltpu.VMEM((N_KV_HEADS, kmax, cols), F32),
            pltpu.VMEM((N_KV_HEADS, kmax, cols), BF16)]


def _flash_init_t(qs_ref, m_ref, acc_ref, s_ref, p_ref):
    m_ref[...] = jnp.full(m_ref.shape, NEG_INF, F32)
    acc_ref[...] = jnp.zeros(acc_ref.shape, F32)


def _flash_step_t(bias_of, kch, vtch, qs_ref, m_ref, acc_ref, s_ref, p_ref, *, tq):
    cols_h = GROUP * tq
    kc = kch.shape[0]
    ones = jnp.ones((ONES_ROWS, kc), BF16)
    hcols = lambda h: slice(h * HEAD_DIM, (h + 1) * HEAD_DIM)

    def logits(h):
        st = lax.dot_general(kch[:, hcols(h)], qs_ref[h * cols_h:(h + 1) * cols_h, :], _NT,
                             preferred_element_type=F32)
        s_ref[h, 0:kc, :] = st + jnp.concatenate([bias_of(h)] * GROUP, axis=1)
        mx = s_ref[h, 0:FLASH_MAX_ROWS, :]
        for r in range(FLASH_MAX_ROWS, kc, FLASH_MAX_ROWS):
            mx = jnp.maximum(mx, s_ref[h, r:r + FLASH_MAX_ROWS, :])
        return jnp.max(mx, axis=0, keepdims=True)

    m_cur = logits(0)
    for h in range(N_KV_HEADS):
        m_next = logits(h + 1) if h + 1 < N_KV_HEADS else None
        m_prev = m_ref[h]
        m_new = jnp.maximum(m_prev, m_cur)
        m_safe = jnp.where(m_new == NEG_INF, 0.0, m_new)
        p_ref[h, 0:kc, :] = jnp.exp(s_ref[h, 0:kc, :] - m_safe).astype(BF16)
        va = jnp.concatenate([vtch[hcols(h), :], ones], axis=0)
        acc_ref[h] = jnp.exp(m_prev - m_safe) * acc_ref[h] + jnp.dot(va, p_ref[h, 0:kc, :],
                                                                     preferred_element_type=F32)
        m_ref[h] = m_new
        m_cur = m_next


def _flash_result_t(h, qs_ref, m_ref, acc_ref, s_ref, p_ref):
    acc = acc_ref[h]
    return (acc[0:HEAD_DIM] / jnp.maximum(acc[HEAD_DIM:HEAD_DIM + 1], 1e-30)).T


def _flash_scratch(tq):
    return [pltpu.VMEM((N_HEADS * tq, HEAD_DIM), BF16),
            pltpu.VMEM((N_HEADS * tq, LANES), F32),
            pltpu.VMEM((N_HEADS * tq, LANES), F32),
            pltpu.VMEM((N_HEADS * tq, HEAD_DIM), F32)]


def dsa_prompt_attend(q, qi, wi, k, vt, ki, tq=Q_BLOCK, kc=512):
    B, S, D = q.shape
    kc = min(kc, S)
    topk = min(DSA_TOPK, S // 4)
    blk = lambda w: pl.BlockSpec((1, tq, w), lambda b, i: (b, i, 0))
    full = lambda w: pl.BlockSpec((1, S, w), lambda b, i: (b, 0, 0))
    return pl.pallas_call(
        functools.partial(_dsa_prompt_kernel, kc=kc, topk=topk, idx_bits=S.bit_length()),
        out_shape=jax.ShapeDtypeStruct((B, S, D), BF16),
        grid=(B, S // tq),
        in_specs=[blk(D), blk(qi.shape[-1]), blk(wi.shape[-1]), full(KV_WIDTH),
                  pl.BlockSpec((1, KV_WIDTH, S), lambda b, i: (b, 0, 0)), full(IDX_DIM)],
        out_specs=blk(D),
        scratch_shapes=[pltpu.VMEM((S, tq), jnp.int32)] + _flash_scratch_t(tq, kc),
        compiler_params=_cparams("parallel", "arbitrary"),
        name="dsa_prompt_attend",
    )(q, qi, wi, k, vt, ki)


def _dsa_sample_kernel(pt_ref, q_ref, qi_ref, wi_ref, kvn_ref, kin_ref, *refs, pg, kc, topk, idx_bits, n_new):
    ckv_refs, cki_refs = refs[:pg], refs[pg:2 * pg]
    o_ref, key_ref, kv_ref, qs_ref, m_ref, l_ref, acc_ref = refs[2 * pg:]
    tq = q_ref.shape[1]
    j = pl.program_id(1)
    n_steps = pl.num_programs(1)
    past = kv_ref.shape[0] - PAGE_SIZE
    qi = qi_ref[0].astype(BF16)
    wi = wi_ref[0] * IDX_SCALE
    for k in range(pg):
        p0 = pl.multiple_of((j * pg + k) * PAGE_SIZE, PAGE_SIZE)
        kv_ref[pl.ds(p0, PAGE_SIZE), :] = ckv_refs[k][0, 0].astype(BF16)
        key_ref[:, pl.ds(p0, PAGE_SIZE)] = _to_key(_index_scores(qi, wi, cki_refs[k][0, 0].astype(BF16)))

    @pl.when(j == n_steps - 1)
    def _():
        nck = (past + PAGE_SIZE) // kc
        kv_ref[past:past + PAGE_SIZE, :] = kvn_ref[0]
        trow = lax.broadcasted_iota(jnp.int32, (tq, PAGE_SIZE), 0)
        ncol = lax.broadcasted_iota(jnp.int32, (tq, PAGE_SIZE), 1)
        visible = (ncol <= trow) & (ncol < n_new)
        key_ref[:, past:past + PAGE_SIZE] = jnp.where(visible, _to_key(_index_scores(qi, wi, kin_ref[0])), INT_MIN)
        _stack_heads(q_ref, qs_ref, tq)
        thr = _topk_threshold(key_ref, nck, kc, topk, idx_bits)
        _flash_init(m_ref, l_ref, acc_ref)

        def attend(c, carry):
            c0 = pl.multiple_of(c * kc, kc)
            sel = key_ref[:, pl.ds(c0, kc)] >= thr
            _grouped_flash_step(lambda h: sel, kv_ref[pl.ds(c0, kc), 0:KV_WIDTH],
                                kv_ref[pl.ds(c0, kc), KV_WIDTH:2 * KV_WIDTH], qs_ref, m_ref, l_ref, acc_ref, tq)
            return carry

        lax.fori_loop(0, nck, attend, 0)
        for hd in range(N_HEADS):
            o_ref[0, :, hd * HEAD_DIM:(hd + 1) * HEAD_DIM] = _flash_result(
                slice(hd * tq, (hd + 1) * tq), l_ref, acc_ref).astype(o_ref.dtype)


PAGES_PER_STEP = 8


def _pages_per_step(n_pages):
    return max(d for d in range(1, PAGES_PER_STEP + 1) if n_pages % d == 0)


def _page_specs(width, col, layer, pg):
    return [pl.BlockSpec((1, 1, PAGE_SIZE, width), lambda b, j, pt, k=k: (layer, pt[b, j * pg + k], 0, col))
            for k in range(pg)]


def _key_chunk(total):
    n = total // LANES
    return LANES * max(d for d in range(1, 9) if n % d == 0)


def dsa_sample_attend(q, qi, wi, kv_new, ki_new, cache_kv, cache_ki, layer, page_table, n_new):
    DB, tq, D = q.shape
    n_pages = page_table.shape[1]
    past = n_pages * PAGE_SIZE
    total = past + PAGE_SIZE
    kc = _key_chunk(total)
    topk = min(DSA_TOPK, (past + n_new) // 4)
    per_b = lambda r, w: pl.BlockSpec((1, r, w), lambda b, j, pt: (b, 0, 0))
    pg = _pages_per_step(n_pages)
    grid_spec = pltpu.PrefetchScalarGridSpec(
        num_scalar_prefetch=1,
        grid=(DB, n_pages // pg),
        in_specs=[per_b(tq, D), per_b(tq, qi.shape[-1]), per_b(tq, wi.shape[-1]),
                  per_b(PAGE_SIZE, 2 * KV_WIDTH), per_b(PAGE_SIZE, IDX_DIM)]
        + _page_specs(2 * KV_WIDTH, 0, layer, pg) + _page_specs(IDX_DIM, 0, layer, pg),
        out_specs=per_b(tq, D),
        scratch_shapes=[pltpu.VMEM((tq, total), jnp.int32),
                        pltpu.VMEM((total, 2 * KV_WIDTH), BF16)] + _flash_scratch(tq),
    )
    return pl.pallas_call(
        functools.partial(_dsa_sample_kernel, pg=pg, kc=kc, topk=topk, idx_bits=total.bit_length(), n_new=n_new),
        out_shape=jax.ShapeDtypeStruct((DB, tq, D), BF16),
        grid_spec=grid_spec,
        compiler_params=_cparams("parallel", "arbitrary"),
        name="dsa_sample_attend",
    )(page_table, q, qi, wi, kv_new, ki_new, *([cache_kv] * pg), *([cache_ki] * pg))


def _nsa_compress_kernel(pt_ref, *refs, pg):
    x_refs = refs[:pg]
    pe_ref, w1_ref, w2_ref, o_ref, rows_ref = refs[pg:]
    j = pl.program_id(2)
    pair = 2 * CMP_LEN
    per_step = pg * PAGE_SIZE // pair
    x = jnp.concatenate([r[0, 0] for r in x_refs], axis=0) if pg > 1 else x_refs[0][0, 0]
    rows_ref[:, pl.ds(pl.multiple_of(j * per_step, per_step), per_step), :] = jnp.swapaxes(
        x.reshape(per_step, pair, x.shape[-1]), 0, 1)

    @pl.when(j == pl.num_programs(2) - 1)
    def _():
        acc = None
        for l in range(CMP_LEN):
            xl = (jnp.concatenate([rows_ref[l], rows_ref[CMP_LEN + l]], axis=0) + pe_ref[0, l:l + 1, :]).astype(BF16)
            part = jnp.dot(xl, w1_ref[0, l], preferred_element_type=F32)
            acc = part if acc is None else acc + part
        hid = _gelu(acc).astype(BF16)
        o_ref[0] = jnp.dot(hid, w2_ref[0], preferred_element_type=F32).astype(o_ref.dtype)


def nsa_compress(rows, layer, page_table, pe_t, w1_bd, w2_bd):
    NB, n_pages = page_table.shape
    nc = n_pages * PAGE_SIZE // CMP_LEN
    pg = _pages_per_step(n_pages)
    grid_spec = pltpu.PrefetchScalarGridSpec(
        num_scalar_prefetch=1,
        grid=(NB, 2, n_pages // pg),
        in_specs=[pl.BlockSpec((1, 1, PAGE_SIZE, KV_WIDTH), lambda b, s, j, pt, k=k: (layer, pt[b, j * pg + k], 0, s))
                  for k in range(pg)]
        + [pl.BlockSpec((1, CMP_LEN, KV_WIDTH), lambda b, s, j, pt: (s, 0, 0)),
           pl.BlockSpec((1, CMP_LEN, KV_WIDTH, N_KV_HEADS * CMP_HIDDEN), lambda b, s, j, pt: (s, 0, 0, 0)),
           pl.BlockSpec((1, N_KV_HEADS * CMP_HIDDEN, KV_WIDTH), lambda b, s, j, pt: (s, 0, 0))],
        out_specs=pl.BlockSpec((1, nc, KV_WIDTH), lambda b, s, j, pt: (b, 0, s)),
        scratch_shapes=[pltpu.VMEM((2 * CMP_LEN, nc // 2, KV_WIDTH), F32)],
    )
    return pl.pallas_call(
        functools.partial(_nsa_compress_kernel, pg=pg),
        out_shape=jax.ShapeDtypeStruct((NB, nc, 2 * KV_WIDTH), BF16),
        grid_spec=grid_spec,
        compiler_params=_cparams("parallel", "arbitrary", "arbitrary"),
        name="nsa_compress",
    )(page_table, *([rows] * pg), pe_t, w1_bd, w2_bd)


def nsa_compress_weights(cmp_pe, cmp_w1, cmp_w2):
    eye = jnp.eye(N_KV_HEADS, dtype=F32)
    w1 = cmp_w1.reshape(2, CMP_LEN, HEAD_DIM, CMP_HIDDEN)
    w1_bd = jnp.einsum('hg,kldj->klhdgj', eye, w1).reshape(2, CMP_LEN, KV_WIDTH, N_KV_HEADS * CMP_HIDDEN)
    w2_bd = jnp.einsum('hg,kjd->khjgd', eye, cmp_w2).reshape(2, N_KV_HEADS * CMP_HIDDEN, KV_WIDTH)
    pe_t = jnp.tile(cmp_pe, (1, 1, N_KV_HEADS))
    return pe_t, w1_bd.astype(BF16), w2_bd.astype(BF16)


def _nsa_compressed_branch(qs_ref, kvc, qpos, gate_of, out_ref, tq):
    nc = kvc.shape[0]
    half = nc // 2
    pcol = lax.broadcasted_iota(jnp.int32, (1, nc), 1)
    cidx = jnp.where(pcol < half, 2 * pcol, 2 * (pcol - half) + 1)
    visible = (cidx + 1) * CMP_LEN - 1 <= qpos
    imps = []
    for h in range(N_KV_HEADS):
        cols = slice(h * HEAD_DIM, (h + 1) * HEAD_DIM)
        logits = lax.dot_general(qs_ref[h * GROUP * tq:(h + 1) * GROUP * tq, :], kvc[:, cols], _NT,
                                 preferred_element_type=F32)
        vcols = slice(KV_WIDTH + h * HEAD_DIM, KV_WIDTH + (h + 1) * HEAD_DIM)
        imp = None
        for g in range(GROUP):
            hd = h * GROUP + g
            s = jnp.where(visible, logits[g * tq:(g + 1) * tq], NEG_INF)
            m = jnp.max(s, axis=1, keepdims=True)
            e = jnp.exp(s - jnp.where(m == NEG_INF, 0.0, m))
            pc = e / jnp.maximum(jnp.sum(e, axis=1, keepdims=True), 1e-30)
            o = jnp.dot(pc.astype(BF16), kvc[:, vcols], preferred_element_type=F32)
            out_ref[hd * tq:(hd + 1) * tq, :] = gate_of(hd, 0) * o
            pair = pc[:, :half] + pc[:, half:]
            imp = pair if imp is None else imp + pair
        imps.append(imp)
    return imps


def _select_blocks(imp, cur, n_blocks, n_pick, axis):
    blk = lax.broadcasted_iota(jnp.int32, imp.shape, axis)
    forced = (blk == 0) | (blk == cur) | (blk == cur - 1)
    x = jnp.where(forced, 16.0, imp)
    x = jnp.where(blk <= cur, x, -1.0)
    x = jnp.where(blk < n_blocks, x, -2.0)
    blk_f = blk.astype(F32)
    sel = jnp.zeros(imp.shape, F32)
    for _ in range(n_pick):
        m = jnp.max(x, axis=axis, keepdims=True)
        first = jnp.min(jnp.where(x == m, blk_f, float(imp.shape[axis])), axis=axis, keepdims=True)
        hit = blk_f == first
        sel = jnp.where(hit, 1.0, sel)
        x = jnp.where(hit, -2.0, x)
    return sel


def _nsa_finish_branch(br, gate_of, out_ref, l_ref, acc_ref, tq):
    for hd in range(N_HEADS):
        rows = slice(hd * tq, (hd + 1) * tq)
        out_ref[rows, :] = out_ref[rows, :] + gate_of(hd, br) * _flash_result(rows, l_ref, acc_ref)


def _nsa_selected_branch(sel_ref, kv_chunk, nck, kc, qpos, qs_ref, m_ref, l_ref, acc_ref, tq):
    ns_pad = sel_ref.shape[2]
    _flash_init(m_ref, l_ref, acc_ref)
    brow = lax.broadcasted_iota(jnp.int32, (ns_pad, kc), 0)
    kcol = lax.broadcasted_iota(jnp.int32, (ns_pad, kc), 1)
    kiota = lax.broadcasted_iota(jnp.int32, (1, kc), 1)

    def attend(c, carry):
        c0 = pl.multiple_of(c * kc, kc)
        expand = jnp.where((c0 + kcol) // SEL_BLOCK == brow, 1.0, 0.0).astype(BF16)
        causal = c0 + kiota <= qpos
        kch, vch = kv_chunk(c0)

        def mask_of(h):
            picked = jnp.dot(sel_ref[h], expand, preferred_element_type=F32)
            return jnp.where(causal, picked, 0.0) > 0.5

        _grouped_flash_step(mask_of, kch, vch, qs_ref, m_ref, l_ref, acc_ref, tq)
        return carry

    lax.fori_loop(0, nck, attend, 0)


def _nsa_window_branch(kw, vw, kwpos, qpos, qs_ref, m_ref, l_ref, acc_ref, tq):
    _flash_init(m_ref, l_ref, acc_ref)
    dist = qpos - kwpos
    wmask = (dist >= 0) & (dist < WINDOW) & (kwpos >= 0)
    _grouped_flash_step(lambda h: wmask, kw, vw, qs_ref, m_ref, l_ref, acc_ref, tq)


def _gate_fn(gates_ref):
    sig = jax.nn.sigmoid(gates_ref[0])
    return lambda hd, br: sig[:, hd * 3 + br:hd * 3 + br + 1]


def _nsa_write_out(o_ref, out_ref, tq):
    for hd in range(N_HEADS):
        o_ref[0, :, hd * HEAD_DIM:(hd + 1) * HEAD_DIM] = out_ref[hd * tq:(hd + 1) * tq, :].astype(o_ref.dtype)


def _nsa_finish_branch_t(br, gate_of, out_ref, fl, tq):
    for h in range(N_KV_HEADS):
        res = _flash_result_t(h, *fl)
        for g in range(GROUP):
            hd = h * GROUP + g
            rows = slice(hd * tq, (hd + 1) * tq)
            out_ref[rows, :] = out_ref[rows, :] + gate_of(hd, br) * res[g * tq:(g + 1) * tq]


def _nsa_prompt_kernel(q_ref, gates_ref, kvc_ref, kk_ref, vt_ref, o_ref, selt_ref, out_ref, *fl, kc, wlen):
    qs_ref = fl[0]
    tq = q_ref.shape[1]
    S = kk_ref.shape[1]
    t0 = pl.program_id(1) * tq
    _stack_heads(q_ref, qs_ref, tq)
    qpos = t0 + lax.broadcasted_iota(jnp.int32, (tq, 1), 0)
    qpos_row = t0 + lax.broadcasted_iota(jnp.int32, (1, tq), 1)
    gate_of = _gate_fn(gates_ref)
    imps = _nsa_compressed_branch(qs_ref, kvc_ref[0], qpos, gate_of, out_ref, tq)
    n_blocks = S // SEL_BLOCK
    for h in range(N_KV_HEADS):
        selt_ref[h] = _select_blocks(imps[h].T, qpos_row // SEL_BLOCK, n_blocks, min(N_SEL, n_blocks), 0).astype(BF16)
    nck = (t0 + tq + kc - 1) // kc
    _flash_init_t(*fl)
    krow = lax.broadcasted_iota(jnp.int32, (kc, n_blocks), 0)
    bcol = lax.broadcasted_iota(jnp.int32, (kc, n_blocks), 1)
    kiota = lax.broadcasted_iota(jnp.int32, (kc, 1), 0)

    def attend(c, carry):
        c0 = pl.multiple_of(c * kc, kc)
        expand = jnp.where((c0 + krow) // SEL_BLOCK == bcol, 1.0, 0.0).astype(BF16)
        causal = c0 + kiota <= qpos_row

        def bias_of(h):
            picked = jnp.dot(expand, selt_ref[h], preferred_element_type=F32)
            return jnp.where(causal, jnp.where(picked > 0.5, 0.0, NEG_INF), NEG_INF)

        _flash_step_t(bias_of, kk_ref[0, pl.ds(c0, kc), 0:KV_WIDTH], vt_ref[0, 0:KV_WIDTH, pl.ds(c0, kc)],
                      *fl, tq=tq)
        return carry

    lax.fori_loop(0, nck, attend, 0)
    _nsa_finish_branch_t(1, gate_of, out_ref, fl, tq)
    start = pl.multiple_of(jnp.clip(t0 + tq - wlen, 0, S - wlen), tq)
    dist = qpos_row - (start + lax.broadcasted_iota(jnp.int32, (wlen, 1), 0))
    wbias = jnp.where(dist >= 0, jnp.where(dist < WINDOW, 0.0, NEG_INF), NEG_INF)
    _flash_init_t(*fl)
    _flash_step_t(lambda h: wbias, kk_ref[0, pl.ds(start, wlen), KV_WIDTH:2 * KV_WIDTH],
                  vt_ref[0, KV_WIDTH:2 * KV_WIDTH, pl.ds(start, wlen)], *fl, tq=tq)
    _nsa_finish_branch_t(2, gate_of, out_ref, fl, tq)
    _nsa_write_out(o_ref, out_ref, tq)


def nsa_prompt_attend(q, gates, kvc, kk, vt, tq=Q_BLOCK, kc=512):
    B, S, D = q.shape
    kc = min(kc, S)
    wlen = min(WINDOW + tq, S)
    n_blocks = S // SEL_BLOCK
    blk = lambda w: pl.BlockSpec((1, tq, w), lambda b, i: (b, i, 0))
    return pl.pallas_call(
        functools.partial(_nsa_prompt_kernel, kc=kc, wlen=wlen),
        out_shape=jax.ShapeDtypeStruct((B, S, D), BF16),
        grid=(B, S // tq),
        in_specs=[blk(D), blk(gates.shape[-1]),
                  pl.BlockSpec((1,) + kvc.shape[1:], lambda b, i: (b, 0, 0)),
                  pl.BlockSpec((1, S, 2 * KV_WIDTH), lambda b, i: (b, 0, 0), pipeline_mode=pl.Buffered(1)),
                  pl.BlockSpec((1, 2 * KV_WIDTH, S), lambda b, i: (b, 0, 0), pipeline_mode=pl.Buffered(1))],
        out_specs=blk(D),
        scratch_shapes=[pltpu.VMEM((N_KV_HEADS, n_blocks, tq), BF16),
                        pltpu.VMEM((N_HEADS * tq, HEAD_DIM), F32)] + _flash_scratch_t(tq, max(kc, wlen)),
        compiler_params=_cparams("parallel", "arbitrary"),
        name="nsa_prompt_attend",
    )(q, gates, kvc, kk, vt)


def _nsa_sample_kernel(pt_ref, q_ref, gates_ref, kvc_ref, kvn_ref, win_ref, winn_ref, *refs, pg, kc, n_new):
    csel_refs = refs[:pg]
    o_ref, kv_ref, wkv_ref, sel_ref, out_ref, qs_ref, m_ref, l_ref, acc_ref = refs[pg:]
    tq = q_ref.shape[1]
    j = pl.program_id(1)
    past = kv_ref.shape[0] - PAGE_SIZE
    for k in range(pg):
        p0 = pl.multiple_of((j * pg + k) * PAGE_SIZE, PAGE_SIZE)
        kv_ref[pl.ds(p0, PAGE_SIZE), :] = csel_refs[k][0, 0].astype(BF16)

    @pl.when(j == pl.num_programs(1) - 1)
    def _():
        kv_ref[past:past + PAGE_SIZE, :] = kvn_ref[0]
        wb = win_ref.shape[1]
        wkv_ref[0:wb, :] = win_ref[0].astype(BF16)
        wkv_ref[wb:wb + PAGE_SIZE, :] = winn_ref[0]
        _stack_heads(q_ref, qs_ref, tq)
        qpos = past + jnp.minimum(lax.broadcasted_iota(jnp.int32, (tq, 1), 0), n_new - 1)
        gate_of = _gate_fn(gates_ref)
        imps = _nsa_compressed_branch(qs_ref, kvc_ref[0], qpos, gate_of, out_ref, tq)
        n_blocks = -(-(past + n_new) // SEL_BLOCK)
        ns_pad = sel_ref.shape[2]
        for h in range(N_KV_HEADS):
            imp = imps[h]
            imp = jnp.concatenate([imp, jnp.zeros((tq, ns_pad - imp.shape[1]), F32)], axis=1)
            sel_ref[h] = _select_blocks(imp, qpos // SEL_BLOCK, n_blocks, min(N_SEL, n_blocks), 1).astype(BF16)
        nck = (past + PAGE_SIZE) // kc
        kv_chunk = lambda c0: (kv_ref[pl.ds(c0, kc), 0:KV_WIDTH], kv_ref[pl.ds(c0, kc), KV_WIDTH:2 * KV_WIDTH])
        _nsa_selected_branch(sel_ref, kv_chunk, nck, kc, qpos, qs_ref, m_ref, l_ref, acc_ref, tq)
        _nsa_finish_branch(1, gate_of, out_ref, l_ref, acc_ref, tq)
        kwpos = past - wb + lax.broadcasted_iota(jnp.int32, (1, wb + PAGE_SIZE), 1)
        _nsa_window_branch(wkv_ref[:, 0:KV_WIDTH], wkv_ref[:, KV_WIDTH:2 * KV_WIDTH], kwpos, qpos,
                           qs_ref, m_ref, l_ref, acc_ref, tq)
        _nsa_finish_branch(2, gate_of, out_ref, l_ref, acc_ref, tq)
        _nsa_write_out(o_ref, out_ref, tq)


def nsa_sample_attend(q, gates, kvc, sel_new, win_buf, win_new, cache_kv, layer, page_table, n_new):
    DB, tq, D = q.shape
    n_pages = page_table.shape[1]
    past = n_pages * PAGE_SIZE
    total = past + PAGE_SIZE
    kc = _key_chunk(total)
    wb = win_buf.shape[1]
    ns_pad = -(-(total // SEL_BLOCK) // LANES) * LANES
    per_b = lambda r, w: pl.BlockSpec((1, r, w), lambda b, j, pt: (b, 0, 0))
    pg = _pages_per_step(n_pages)
    grid_spec = pltpu.PrefetchScalarGridSpec(
        num_scalar_prefetch=1,
        grid=(DB, n_pages // pg),
        in_specs=[per_b(tq, D), per_b(tq, gates.shape[-1]), per_b(kvc.shape[1], kvc.shape[2]),
                  per_b(PAGE_SIZE, 2 * KV_WIDTH), per_b(wb, 2 * KV_WIDTH), per_b(PAGE_SIZE, 2 * KV_WIDTH)]
        + _page_specs(2 * KV_WIDTH, 1, layer, pg),
        out_specs=per_b(tq, D),
        scratch_shapes=[pltpu.VMEM((total, 2 * KV_WIDTH), BF16),
                        pltpu.VMEM((wb + PAGE_SIZE, 2 * KV_WIDTH), BF16),
                        pltpu.VMEM((N_KV_HEADS, tq, ns_pad), BF16),
                        pltpu.VMEM((N_HEADS * tq, HEAD_DIM), F32)] + _flash_scratch(tq),
    )
    return pl.pallas_call(
        functools.partial(_nsa_sample_kernel, pg=pg, kc=kc, n_new=n_new),
        out_shape=jax.ShapeDtypeStruct((DB, tq, D), BF16),
        grid_spec=grid_spec,
        compiler_params=_cparams("parallel", "arbitrary"),
        name="nsa_sample_attend",
    )(page_table, q, gates, kvc, sel_new, win_buf, win_new, *([cache_kv] * pg))


HALO = 8
G_IG, G_LF, G_CUM = 0, MLSTM_HEADS, 2 * MLSTM_HEADS
MASKED_GATE = -1e30


def _split3(x):
    hi = x.astype(BF16)
    r1 = x - hi.astype(F32)
    mid = r1.astype(BF16)
    lo = (r1 - mid.astype(F32)).astype(BF16)
    return hi, mid, lo


def _mlstm_pre_kernel(xm_ref, halo_ref, cw_ref, cb_ref, wq_ref, wk_ref, wv_ref, wvt_ref, wg_ref, bg_ref,
                      q_ref, k_ref, v_ref, vt_ref, xc_ref, g_ref, gt_ref, xp_ref, *, chunk, n_valid):
    tm = xm_ref.shape[1]
    hd = MLSTM_HEAD_DIM
    xp_ref[0:HALO, :] = halo_ref[0, 0]
    xp_ref[HALO:HALO + tm, :] = xm_ref[0]
    y = cb_ref[...]
    for j in range(CONV_WIDTH):
        off = HALO - (CONV_WIDTH - 1) + j
        y = y + cw_ref[j:j + 1, :] * xp_ref[off:off + tm, :]
    xc = y * jax.nn.sigmoid(y)
    xc_ref[0] = xc
    xcb = xc.astype(BF16)
    xmb = xm_ref[0].astype(BF16)
    g = bg_ref[...]
    for h in range(MLSTM_HEADS):
        cols = slice(h * hd, (h + 1) * hd)
        q = jnp.dot(xcb[:, cols], wq_ref[h], preferred_element_type=F32).astype(BF16)
        k = (jnp.dot(xcb[:, cols], wk_ref[h], preferred_element_type=F32) * hd ** -0.5).astype(BF16)
        v = jnp.dot(xmb[:, cols], wv_ref[h], preferred_element_type=F32).astype(BF16)
        q_ref[0, :, cols] = q
        k_ref[0, :, cols] = k
        v_ref[0, :, cols] = v
        vt_ref[0, cols, :] = lax.dot_general(wvt_ref[h], xmb[:, cols], _NT, preferred_element_type=F32).astype(BF16)
        for i, a in enumerate((q, k, v)):
            g = g + jnp.dot(a, wg_ref[i * MLSTM_INNER + h * hd:i * MLSTM_INNER + (h + 1) * hd, :],
                            preferred_element_type=F32)
    col = lax.broadcasted_iota(jnp.int32, (tm, LANES), 1)
    row = pl.program_id(1) * tm + lax.broadcasted_iota(jnp.int32, (tm, LANES), 0)
    log_f = jnp.minimum(g, 0.0) - jnp.log1p(jnp.exp(-jnp.abs(g)))
    g = jnp.where(col < G_LF, g, log_f)
    live = row < n_valid
    g = jnp.where(live, g, jnp.where(col < G_LF, MASKED_GATE, 0.0))
    tri = (lax.broadcasted_iota(jnp.int32, (chunk, chunk), 0) >= lax.broadcasted_iota(jnp.int32, (chunk, chunk), 1))
    tri = jnp.where(tri, 1.0, 0.0).astype(BF16)
    for c in range(tm // chunk):
        rows = slice(c * chunk, (c + 1) * chunk)
        cum = None
        for piece in _split3(g[rows]):
            part = jnp.dot(tri, piece, preferred_element_type=F32)
            cum = part if cum is None else cum + part
        out = jnp.where(col[rows] < G_CUM, g[rows], pltpu.roll(cum, G_CUM - G_LF, 1))
        g_ref[0, rows, :] = out
        gt_ref[0, :, rows] = out.T[0:gt_ref.shape[1], :]


def mlstm_pre(xm, halo, conv_w, conv_b, wq, wk, wv, wvt, wg, bg, tm, chunk, n_valid):
    NB, S, W = xm.shape
    hd = MLSTM_HEAD_DIM
    const = lambda a: pl.BlockSpec(a.shape, lambda b, i: (0,) * a.ndim)
    row_blk = lambda w: pl.BlockSpec((1, tm, w), lambda b, i: (b, i, 0))
    sds = jax.ShapeDtypeStruct
    return pl.pallas_call(
        functools.partial(_mlstm_pre_kernel, chunk=chunk, n_valid=n_valid),
        out_shape=[sds((NB, S, W), BF16)] * 3 + [sds((NB, W, S), BF16), sds((NB, S, W), F32),
                                                 sds((NB, S, LANES), F32), sds((NB, 2 * HALO, S), F32)],
        grid=(NB, S // tm),
        in_specs=[row_blk(W), pl.BlockSpec((1, 1, HALO, W), lambda b, i: (b, i, 0, 0)),
                  const(conv_w), const(conv_b), const(wq), const(wk), const(wv), const(wvt), const(wg), const(bg)],
        out_specs=[row_blk(W)] * 3 + [pl.BlockSpec((1, W, tm), lambda b, i: (b, 0, i)), row_blk(W), row_blk(LANES),
                                      pl.BlockSpec((1, 2 * HALO, tm), lambda b, i: (b, 0, i))],
        scratch_shapes=[pltpu.VMEM((HALO + tm, W), F32)],
        compiler_params=_cparams("parallel", "parallel"),
        name="mlstm_pre",
    )(xm, halo, conv_w, conv_b, wq, wk, wv, wvt, wg, bg)


def _mlstm_scan_kernel(q_ref, k_ref, v_ref, vt_ref, g_ref, gt_ref, xc_ref, z_ref, ng_ref, sk_ref, c0_ref, n0_ref, m0_ref,
                       o_ref, c_out, n_out, m_out, c_ref, n_ref, m_ref):
    L = q_ref.shape[1]
    h = pl.program_id(1)
    ci = pl.program_id(2)

    @pl.when(ci == 0)
    def _():
        c_ref[...] = c0_ref[0, 0]
        n_ref[...] = n0_ref[0, 0]
        m_ref[...] = m0_ref[0, 0]

    q, k, v = q_ref[0], k_ref[0], v_ref[0]
    col = lax.broadcasted_iota(jnp.int32, (L, LANES), 1)
    b_col = jnp.sum(jnp.where(col == G_CUM + h, g_ref[0], 0.0), axis=1, keepdims=True)
    i_row = gt_ref[0, pl.ds(G_IG + h, 1), :]
    b_row = gt_ref[0, pl.ds(G_CUM + h, 1), :]
    m = m_ref[0:1, 0:1]
    inter = b_col + m
    dmat = b_col - b_row + i_row
    tril = lax.broadcasted_iota(jnp.int32, (L, L), 0) >= lax.broadcasted_iota(jnp.int32, (L, L), 1)
    dmat = jnp.where(tril, dmat, NEG_INF)
    m_loc = jnp.maximum(inter, jnp.max(dmat, axis=1, keepdims=True))
    a = lax.dot_general(q, k, _NT, preferred_element_type=F32) * jnp.exp(dmat - m_loc)
    w_inter = jnp.exp(inter - m_loc)
    c_prev = c_ref[...]
    n_prev = n_ref[...]
    num = (jnp.dot(a.astype(BF16), v, preferred_element_type=F32)
           + w_inter * lax.dot_general(q, c_prev.astype(BF16), _NT, preferred_element_type=F32))
    qn = lax.dot_general(q, n_prev.astype(BF16), _NT, preferred_element_type=F32)[:, 0:1]
    den = jnp.sum(a, axis=1, keepdims=True) + w_inter * qn
    hc = num / jnp.maximum(jnp.abs(den), jnp.exp(-m_loc))
    b_end = b_row[:, L - 1:L]
    lg = b_end - b_row + i_row
    m_new = jnp.maximum(b_end + m, jnp.max(lg, axis=1, keepdims=True))
    wg = jnp.exp(lg - m_new)
    decay = jnp.exp(b_end + m - m_new)
    c_ref[...] = decay * c_prev + jnp.dot((vt_ref[0] * wg).astype(BF16), k, preferred_element_type=F32)
    n_ref[...] = decay * n_prev + jnp.dot(jnp.broadcast_to(wg, (n_ref.shape[0], L)).astype(BF16), k,
                                          preferred_element_type=F32)
    m_ref[...] = jnp.broadcast_to(m_new, m_ref.shape)
    mu = jnp.mean(hc, axis=1, keepdims=True)
    hz = hc - mu
    var = jnp.mean(hz * hz, axis=1, keepdims=True)
    hn = hz * lax.rsqrt(var + LN_EPS) * ng_ref[...]
    z = z_ref[0]
    o_ref[0] = ((hn + sk_ref[...] * xc_ref[0]) * (z * jax.nn.sigmoid(z))).astype(o_ref.dtype)

    @pl.when(ci == pl.num_programs(2) - 1)
    def _():
        c_out[0, 0] = c_ref[...]
        n_out[0, 0] = n_ref[...]
        m_out[0, 0] = m_ref[...]


def mlstm_scan(q, k, v, vt, g, gt, xc, z, norm_g, skip, c0, n0, m0, chunk):
    NB, S, W = q.shape
    hd = MLSTM_HEAD_DIM
    H = W // hd
    seq = lambda: pl.BlockSpec((1, chunk, hd), lambda b, h, c: (b, c, h))
    vec = pl.BlockSpec((1, hd), lambda b, h, c: (0, h))
    st = lambda r, w: pl.BlockSpec((1, 1, r, w), lambda b, h, c: (b, h, 0, 0))
    sds = jax.ShapeDtypeStruct
    return pl.pallas_call(
        _mlstm_scan_kernel,
        out_shape=[sds((NB, S, W), BF16), sds((NB, H, hd, hd), F32), sds((NB, H, HALO, hd), F32), sds((NB, H, HALO, LANES), F32)],
        grid=(NB, H, S // chunk),
        in_specs=[seq(), seq(), seq(), pl.BlockSpec((1, hd, chunk), lambda b, h, c: (b, h, c)),
                  pl.BlockSpec((1, chunk, LANES), lambda b, h, c: (b, c, 0)),
                  pl.BlockSpec((1, 2 * HALO, chunk), lambda b, h, c: (b, 0, c)),
                  seq(), seq(), vec, vec, st(hd, hd), st(HALO, hd), st(HALO, LANES)],
        out_specs=[seq(), st(hd, hd), st(HALO, hd), st(HALO, LANES)],
        scratch_shapes=[pltpu.VMEM((hd, hd), F32), pltpu.VMEM((HALO, hd), F32), pltpu.VMEM((HALO, LANES), F32)],
        compiler_params=_cparams("parallel", "parallel", "arbitrary"),
        name="mlstm_scan",
    )(q, k, v, vt, g, gt, xc, z, norm_g.reshape(1, W), skip.reshape(1, W), c0, n0, m0)


SAMPLE_Q_ROWS = 16
TM_PROMPT = 512
TM_PEER_ROUTE = 256
TM_PEER = 512
TM_MLSTM = 256
CHUNK_MLSTM = 256


def _pad_rows(a, rows):
    return jnp.pad(a, ((0, 0), (0, rows - a.shape[1]), (0, 0)))


def _pad_cols(a, cols):
    return jnp.pad(a, ((0, 0),) * (a.ndim - 1) + ((0, cols - a.shape[-1]),))


def _dsa_layer(xp, xs, mp, ms, w_in, w_o, ln_g, ln_b, cache_kv, cache_ki, layer, page_table, DB, T):
    B, S, D = xp.shape
    wb = w_in.astype(BF16)
    o1, o2, o3, o4 = D, D + KV_WIDTH, D + 2 * KV_WIDTH, D + 2 * KV_WIDTH + IDX_HEADS * IDX_DIM
    weights = [wb[:, :o1], wb[:, o1:o3], wb[:, o1:o2], wb[:, o2:o3], wb[:, o3:o4], wb[:, o4:o4 + IDX_DIM],
               _pad_cols(wb[:, o4 + IDX_DIM:], LANES)]
    outs = [(0, BF16), (1, F32), (2, BF16), (3, BF16), (4, BF16), (5, F32), (5, BF16), (6, F32)]
    w_ob = w_o.astype(BF16)
    q, kv32, kb, _, qi, ki32, kib, wi, vt = proj(xp, mp(0, 0), mp(0, 1), weights, outs, TM_PROMPT,
                                                 weights_t=[wb[:, o2:o3].T])
    o = dsa_prompt_attend(q, qi, wi, kb, vt, kib)
    xp = outproj_postnorm(o, w_ob, xp, mp(0, 2), ln_g, ln_b, TM_PROMPT)
    kv_p = kv32.reshape(B, S, 2, N_KV_HEADS, HEAD_DIM)
    n = DB * T
    q, kv32s, kb, vb, qi, ki32s, kib, wi = proj(xs, ms(0, 0), ms(0, 1), weights, outs, n)
    per_b = lambda a: a.reshape(DB, T, a.shape[-1])
    kv_new = _pad_rows(jnp.concatenate([per_b(kb), per_b(vb)], axis=-1), PAGE_SIZE)
    o = dsa_sample_attend(_pad_rows(per_b(q), SAMPLE_Q_ROWS), _pad_rows(per_b(qi), SAMPLE_Q_ROWS),
                          _pad_rows(per_b(wi), SAMPLE_Q_ROWS), kv_new, _pad_rows(per_b(kib), PAGE_SIZE),
                          cache_kv.reshape(cache_kv.shape[:3] + (2 * KV_WIDTH,)), cache_ki, layer, page_table, T)
    xs = outproj_postnorm(o[:, :T].reshape(1, n, D), w_ob, xs, ms(0, 2), ln_g, ln_b, n)
    kv_s = kv32s.reshape(DB, T, 2, N_KV_HEADS, HEAD_DIM)
    return xp, xs, kv_p, kv_s, ki32, ki32s.reshape(DB, T, IDX_DIM)


def _nsa_layer(xp, xs, mp, ms, w_in, cmp_pe, cmp_w1, cmp_w2, w_o, ln_g, ln_b, cache_kv, win_state, layer,
               page_table, DB, T):
    B, S, D = xp.shape
    wb = w_in.astype(BF16)
    c1, c2, c3 = D + 4 * KV_WIDTH, D + 6 * KV_WIDTH, D + 2 * KV_WIDTH
    sel_k, sel_v = wb[:, c3:c3 + KV_WIDTH], wb[:, c3 + KV_WIDTH:c1]
    win_k, win_v = wb[:, c1:c1 + KV_WIDTH], wb[:, c1 + KV_WIDTH:c2]
    weights = [wb[:, :D], wb[:, D:c1], wb[:, c1:c2], wb[:, c3:c2], _pad_cols(wb[:, c2:], LANES),
               jnp.concatenate([sel_k, win_k], axis=1)]
    outs = [(0, BF16), (1, F32), (2, F32), (3, BF16), (4, F32)]
    w_ob = w_o.astype(BF16)
    cmp_w = nsa_compress_weights(cmp_pe, cmp_w1, cmp_w2)
    q, cs32, win32, kk, gates, vt = proj(xp, mp(0, 0), mp(0, 1), weights,
                                         [(0, BF16), (1, F32), (2, F32), (5, BF16), (4, F32)], TM_PROMPT,
                                         weights_t=[jnp.concatenate([sel_v, win_v], axis=1).T])
    pages = S // PAGE_SIZE
    ident = jnp.arange(B * pages, dtype=jnp.int32).reshape(B, pages)
    kvc = nsa_compress(cs32.reshape(1, B * pages, PAGE_SIZE, 4 * KV_WIDTH), 0, ident, *cmp_w)
    o = nsa_prompt_attend(q, gates, kvc, kk, vt)
    xp = outproj_postnorm(o, w_ob, xp, mp(0, 2), ln_g, ln_b, TM_PROMPT)
    kv_p = cs32.reshape(B, S, 2, 2, N_KV_HEADS, HEAD_DIM)
    keep = min(WINDOW, S)
    win_p = win32[:, S - keep:].reshape(B, keep, 2, N_KV_HEADS, HEAD_DIM)
    n = DB * T
    q, cs32s, win32s, selwin, gates = proj(xs, ms(0, 0), ms(0, 1), weights, outs, n)
    per_b = lambda a: a.reshape(DB, T, a.shape[-1])
    kvc = nsa_compress(cache_kv.reshape(cache_kv.shape[:3] + (4 * KV_WIDTH,)), layer, page_table, *cmp_w)
    selwin = per_b(selwin)
    wbuf = win_state[layer]
    o = nsa_sample_attend(_pad_rows(per_b(q), SAMPLE_Q_ROWS), _pad_rows(per_b(gates), SAMPLE_Q_ROWS), kvc,
                          _pad_rows(selwin[..., :2 * KV_WIDTH], PAGE_SIZE), wbuf.reshape(DB, wbuf.shape[1], 2 * KV_WIDTH),
                          _pad_rows(selwin[..., 2 * KV_WIDTH:], PAGE_SIZE),
                          cache_kv.reshape(cache_kv.shape[:3] + (4 * KV_WIDTH,)), layer, page_table, T)
    xs = outproj_postnorm(o[:, :T].reshape(1, n, D), w_ob, xs, ms(0, 2), ln_g, ln_b, n)
    kv_s = cs32s.reshape(DB, T, 2, 2, N_KV_HEADS, HEAD_DIM)
    win_s = jnp.concatenate([wbuf, win32s.reshape(DB, T, 2, N_KV_HEADS, HEAD_DIM)], axis=1)[:, T:]
    return xp, xs, kv_p, kv_s, win_p, win_s


def _mlstm_layer(xp, xs, mp, ms, w_in, conv_w, conv_b, w_qkv, w_gate, b_gate, norm_g, skip, w_o, ln_g, ln_b,
                 conv_state, c_state, n_state, m_state, DB, T):
    B, S, D = xp.shape
    W, H, hd = MLSTM_INNER, MLSTM_HEADS, MLSTM_HEAD_DIM
    wb = w_in.astype(BF16)
    weights = [wb[:, :W], wb[:, W:]]
    outs = [(0, F32), (1, F32)]
    wq, wk, wv = (w_qkv[i].astype(BF16) for i in range(3))
    pre_w = (conv_w, conv_b.reshape(1, W), wq, wk, wv, jnp.swapaxes(wv, 1, 2), _pad_cols(w_gate, LANES).astype(BF16),
             _pad_cols(b_gate.reshape(1, -1), LANES))
    w_ob = w_o.astype(BF16)
    keep = CONV_WIDTH - 1
    rep = lambda a, r: jnp.broadcast_to(a[..., None, :], a.shape[:-1] + (r, a.shape[-1]))
    xm, z = proj(xp, mp(0, 0), mp(0, 1), weights, outs, TM_PROMPT)
    tm = min(TM_MLSTM, S)
    tiles = xm.reshape(B, S // tm, tm, W)
    halo = jnp.concatenate([jnp.zeros((B, 1, HALO, W), F32), tiles[:, :-1, tm - HALO:]], axis=1)
    chunk = min(CHUNK_MLSTM, S)
    q, k, v, vt, xc, g, gt = mlstm_pre(xm, halo, *pre_w, tm=tm, chunk=chunk, n_valid=S)
    zeros = lambda *s: jnp.zeros(s, F32)
    o, c_p, n_p, m_p = mlstm_scan(q, k, v, vt, g, gt, xc, z, norm_g, skip, zeros(B, H, hd, hd), zeros(B, H, HALO, hd),
                                  zeros(B, H, HALO, LANES), chunk)
    xp = outproj_postnorm(o, w_ob, xp, mp(0, 2), ln_g, ln_b, TM_PROMPT)
    conv_p = jnp.concatenate([zeros(B, keep, W), xm], axis=1)[:, -keep:]
    n = DB * T
    xm_s, z_s = proj(xs, ms(0, 0), ms(0, 1), weights, outs, n)
    xm_s = xm_s.reshape(DB, T, W)
    rows = LANES
    halo = jnp.concatenate([zeros(DB, HALO - keep, W), conv_state], axis=1)[:, None]
    q, k, v, vt, xc, g, gt = mlstm_pre(_pad_rows(xm_s, rows), halo, *pre_w, tm=rows, chunk=rows, n_valid=T)
    m0 = jnp.broadcast_to(m_state[..., None, None], (DB, H, HALO, LANES))
    o, c_s, n_s, m_s = mlstm_scan(q, k, v, vt, g, gt, xc, _pad_rows(z_s.reshape(DB, T, W), rows), norm_g, skip,
                                  c_state, rep(n_state, HALO), m0, rows)
    xs = outproj_postnorm(o[:, :T].reshape(1, n, W), w_ob, xs, ms(0, 2), ln_g, ln_b, n)
    conv_s = jnp.concatenate([conv_state, xm_s], axis=1)[:, -keep:]
    return (xp, xs, conv_p, conv_s, c_p, c_s, n_p[:, :, 0], n_s[:, :, 0], m_p[:, :, 0, 0], m_s[:, :, 0, 0])


def kernel(x_prompt, x_sample, cache_a_kv, cache_a_kidx, cache_b_kv, state_b_win, state_c_conv, state_c_C, state_c_n,
           state_c_m, page_table, c_prompt, c_sample, a_w_in, a_w_o, b_w_in, b_cmp_pe, b_cmp_w1, b_cmp_w2, b_w_o,
           c_w_in, c_conv_w, c_conv_b, c_w_qkv, c_w_gate, c_b_gate, c_norm_g, c_skip, c_w_o,
           ada_w, ada_b, ln_g, ln_b, peer_w_q, peer_sub_keys, peer_u, peer_v):
    B, S, D = x_prompt.shape
    DB, T, _ = x_sample.shape
    n = DB * T
    n_cond = B + DB
    cond = _pad_rows(jnp.concatenate([c_prompt, c_sample], axis=0)[None], -(-n_cond // 8) * 8)[0]
    mods = adaln_all(cond, ada_w, ada_b).reshape(DEPTH, cond.shape[0], 2, 3, D)
    xp, xs = x_prompt, x_sample.reshape(1, n, D)
    outs = {name: [] for name in ("a_kv_p", "a_kv_s", "a_ki_p", "a_ki_s", "b_kv_p", "b_kv_s", "b_win_p", "b_win_s",
                                  "conv_p", "conv_s", "C_p", "C_s", "n_p", "n_s", "m_p", "m_s")}
    for i in range(DEPTH):
        kind, j = i % N_MIXERS, i // N_MIXERS
        mod_p = mods[i, :B]
        mod_s = jnp.repeat(mods[i, B:n_cond], T, axis=0)
        mp = lambda s, r, mod_p=mod_p: mod_p[:, s, r][:, None, :]
        ms = lambda s, r, mod_s=mod_s: mod_s[:, s, r][None]
        if kind == 0:
            xp, xs, kvp, kvs, kip, kis = _dsa_layer(xp, xs, mp, ms, a_w_in[j], a_w_o[j], ln_g[i, 0], ln_b[i, 0],
                                                    cache_a_kv, cache_a_kidx, j, page_table, DB, T)
            for name, val in zip(("a_kv_p", "a_kv_s", "a_ki_p", "a_ki_s"), (kvp, kvs, kip, kis)):
                outs[name].append(val)
        elif kind == 1:
            xp, xs, kvp, kvs, wp, ws = _nsa_layer(xp, xs, mp, ms, b_w_in[j], b_cmp_pe[j], b_cmp_w1[j], b_cmp_w2[j],
                                                  b_w_o[j], ln_g[i, 0], ln_b[i, 0], cache_b_kv, state_b_win, j,
                                                  page_table, DB, T)
            for name, val in zip(("b_kv_p", "b_kv_s", "b_win_p", "b_win_s"), (kvp, kvs, wp, ws)):
                outs[name].append(val)
        else:
            res = _mlstm_layer(xp, xs, mp, ms, c_w_in[j], c_conv_w[j], c_conv_b[j], c_w_qkv[j], c_w_gate[j], c_b_gate[j],
                               c_norm_g[j], c_skip[j], c_w_o[j], ln_g[i, 0], ln_b[i, 0],
                               state_c_conv[j], state_c_C[j], state_c_n[j], state_c_m[j], DB, T)
            xp, xs = res[0], res[1]
            for name, val in zip(("conv_p", "conv_s", "C_p", "C_s", "n_p", "n_s", "m_p", "m_s"), res[2:]):
                outs[name].append(val)
        peer_w = (peer_w_q[i].astype(BF16), peer_sub_keys[i].astype(BF16), peer_u[i].astype(BF16).T,
                  peer_v[i].astype(BF16), ln_g[i, 1], ln_b[i, 1])
        xp = peer_layer(xp, mp(1, 0), mp(1, 1), mp(1, 2), *peer_w, tm_route=min(TM_PEER_ROUTE, S), tm=min(TM_PEER, S))
        xs = peer_layer(xs, ms(1, 0), ms(1, 1), ms(1, 2), *peer_w, tm_route=n, tm=n)
    return (xp, xs.reshape(DB, T, D)) + tuple(jnp.stack(outs[name]) for name in outs)
```

```python
import functools
import math

import jax
import jax.numpy as jnp
from jax import lax
from jax.experimental import pallas as pl
from jax.experimental.pallas import tpu as pltpu

F32 = jnp.float32
BF16 = jnp.bfloat16

D_MODEL = 1024
DEPTH = 4
PAGE_SIZE = 128
N_MIXERS = 3

N_HEADS = 16
HEAD_DIM = D_MODEL // N_HEADS
N_KV_HEADS = 4
GROUP = N_HEADS // N_KV_HEADS
KV_WIDTH = N_KV_HEADS * HEAD_DIM
Q_BLOCK = 128
ATTN_SCALE = HEAD_DIM ** -0.5

IDX_HEADS = 8
IDX_DIM = 64
IDX_SCALE = (IDX_HEADS * IDX_DIM) ** -0.5
DSA_TOPK = 256

CMP_LEN = 32
CMP_HIDDEN = 2 * HEAD_DIM
SEL_BLOCK = 64
N_SEL = 16
WINDOW = 512

MLSTM_INNER = 2 * D_MODEL
MLSTM_HEADS = 4
MLSTM_HEAD_DIM = MLSTM_INNER // MLSTM_HEADS
CONV_WIDTH = 4
MLSTM_CHUNK = 64

PEER_HEADS = 8
PEER_KEYS = 128
N_EXPERTS = PEER_KEYS * PEER_KEYS
PEER_KEY_DIM = 256
PEER_TOPK = 16

ALPHA = (2 * DEPTH) ** 0.25
LN_EPS = 1e-5

LANES = 128
VMEM_LIMIT = 56 * 1024 * 1024

_NT = (((1,), (1,)), ((), ()))


def _cparams(*sem):
    return pltpu.CompilerParams(dimension_semantics=sem, vmem_limit_bytes=VMEM_LIMIT)


def _gelu(x):
    return 0.5 * x * (1.0 + jnp.tanh(math.sqrt(2.0 / math.pi) * (x + 0.044715 * (x * x * x))))


def _post_norm_math(x, y, gate, g, b):
    z = ALPHA * x + gate * y
    mu = jnp.mean(z, axis=-1, keepdims=True)
    zc = z - mu
    var = jnp.mean(zc * zc, axis=-1, keepdims=True)
    return zc * lax.rsqrt(var + LN_EPS) * g + b


def _adaln_kernel(c_ref, w_ref, b_ref, o_ref):
    c = c_ref[...]
    s = (c * jax.nn.sigmoid(c)).astype(BF16)
    o_ref[0] = jnp.dot(s, w_ref[0].astype(BF16), preferred_element_type=F32) + b_ref[0]


def adaln_all(c, ada_w, ada_b):
    M = c.shape[0]
    n = ada_w.shape[-1] // D_MODEL
    return pl.pallas_call(
        _adaln_kernel,
        out_shape=jax.ShapeDtypeStruct((DEPTH, M, n * D_MODEL), F32),
        grid=(DEPTH, n),
        in_specs=[pl.BlockSpec((M, D_MODEL), lambda i, j: (0, 0)),
                  pl.BlockSpec((1, D_MODEL, D_MODEL), lambda i, j: (i, 0, j)),
                  pl.BlockSpec((1, 1, D_MODEL), lambda i, j: (i, 0, j))],
        out_specs=pl.BlockSpec((1, M, D_MODEL), lambda i, j: (i, 0, j)),
        compiler_params=_cparams("arbitrary", "arbitrary"),
        name="adaln",
    )(c, ada_w, ada_b.reshape(DEPTH, 1, -1))


def _proj_kernel(x_ref, sh_ref, sc_ref, *refs, n_w, n_t, out_map):
    w_refs, wt_refs, o_refs = refs[:n_w], refs[n_w:n_w + n_t], refs[n_w + n_t:]
    h = (x_ref[0] * (1.0 + sc_ref[0]) + sh_ref[0]).astype(BF16)
    done = {}
    for o_ref, wi in zip(o_refs, out_map):
        if wi not in done:
            done[wi] = jnp.dot(h, w_refs[wi][...], preferred_element_type=F32)
        o_ref[0] = done[wi].astype(o_ref.dtype)
    for o_ref, wt_ref in zip(o_refs[len(out_map):], wt_refs):
        o_ref[0] = lax.dot_general(wt_ref[...], h, _NT, preferred_element_type=F32).astype(o_ref.dtype)


def proj(x, shift, scale, weights, outs, tm, weights_t=()):
    nb, S, D = x.shape
    rows = shift.shape[1]
    mblk = (1, tm, D) if rows == S else (1, 1, D)
    mmap = (lambda b, i: (b, i, 0)) if rows == S else (lambda b, i: (b, 0, 0))
    in_specs = [pl.BlockSpec((1, tm, D), lambda b, i: (b, i, 0)),
                pl.BlockSpec(mblk, mmap), pl.BlockSpec(mblk, mmap)]
    in_specs += [pl.BlockSpec(w.shape, lambda b, i: (0, 0)) for w in (*weights, *weights_t)]
    out_shape = [jax.ShapeDtypeStruct((nb, S, weights[wi].shape[1]), dt) for wi, dt in outs]
    out_specs = [pl.BlockSpec((1, tm, weights[wi].shape[1]), lambda b, i: (b, i, 0)) for wi, _ in outs]
    out_shape += [jax.ShapeDtypeStruct((nb, w.shape[0], S), BF16) for w in weights_t]
    out_specs += [pl.BlockSpec((1, w.shape[0], tm), lambda b, i: (b, 0, i)) for w in weights_t]
    return pl.pallas_call(
        functools.partial(_proj_kernel, n_w=len(weights), n_t=len(weights_t), out_map=tuple(wi for wi, _ in outs)),
        out_shape=out_shape, grid=(nb, S // tm), in_specs=in_specs, out_specs=out_specs,
        compiler_params=_cparams("parallel", "parallel"),
        name="proj",
    )(x, shift, scale, *weights, *weights_t)


def _outproj_kernel(o_ref, w_ref, x_ref, gate_ref, g_ref, b_ref, y_ref):
    y = jnp.dot(o_ref[0], w_ref[...], preferred_element_type=F32)
    y_ref[0] = _post_norm_math(x_ref[0], y, gate_ref[0], g_ref[...], b_ref[...])


def outproj_postnorm(o, w, x, gate, g, b, tm):
    nb, S, K = o.shape
    D = x.shape[-1]
    rows = gate.shape[1]
    mblk = (1, tm, D) if rows == S else (1, 1, D)
    mmap = (lambda bi, i: (bi, i, 0)) if rows == S else (lambda bi, i: (bi, 0, 0))
    return pl.pallas_call(
        _outproj_kernel,
        out_shape=jax.ShapeDtypeStruct(x.shape, F32),
        grid=(nb, S // tm),
        in_specs=[pl.BlockSpec((1, tm, K), lambda bi, i: (bi, i, 0)),
                  pl.BlockSpec((K, D), lambda bi, i: (0, 0)),
                  pl.BlockSpec((1, tm, D), lambda bi, i: (bi, i, 0)),
                  pl.BlockSpec(mblk, mmap),
                  pl.BlockSpec((1, D), lambda bi, i: (0, 0)),
                  pl.BlockSpec((1, D), lambda bi, i: (0, 0))],
        out_specs=pl.BlockSpec((1, tm, D), lambda bi, i: (bi, i, 0)),
        compiler_params=_cparams("parallel", "parallel"),
        name="outproj_postnorm",
    )(o, w, x, gate, g.reshape(1, D), b.reshape(1, D))


def _top_rows(x, iota, n):
    vals, idxs = [], []
    for _ in range(n):
        m = jnp.max(x, axis=0, keepdims=True)
        idx = jnp.min(jnp.where(x == m, iota, jnp.inf), axis=0, keepdims=True)
        vals.append(m)
        idxs.append(idx)
        x = jnp.where(iota == idx, -jnp.inf, x)
    return vals, idxs


def _peer_route_kernel(x_ref, sh_ref, sc_ref, wq_ref, keys_ref, i1_ref, i2_ref, g_ref):
    tm = x_ref.shape[1]
    h = (x_ref[0] * (1.0 + sc_ref[0]) + sh_ref[0]).astype(BF16)
    q = jnp.dot(h, wq_ref[...], preferred_element_type=F32).astype(BF16)
    iota_k = lax.broadcasted_iota(jnp.int32, (PEER_KEYS, tm), 0).astype(F32)
    half = PEER_KEY_DIM // 2
    n = PEER_TOPK
    row16 = lax.broadcasted_iota(jnp.int32, (n, tm), 0).astype(F32)
    row8 = lax.broadcasted_iota(jnp.int32, (8, tm), 0).astype(F32)
    pieces = [(0, 1, 16, 0, 15), (1, 1, 8, 0, 7), (2, 1, 8, 0, 4), (3, 1, 8, 0, 3),
              (0, 2, 16, 4, 15), (1, 2, 8, 4, 7), (2, 2, 8, 4, 4)]
    ids = []
    for fixed, which, rows, lo, hi in pieces:
        r = row16 if rows == 16 else row8
        pair = fixed * n + r if which == 1 else r * n + fixed
        ids.append(jnp.where(r < lo, float(n * n), jnp.where(r > hi, float(n * n), pair)))
    cand_id = jnp.concatenate(ids, axis=0)
    cand_ok = cand_id < float(n * n)
    i1_rows, i2_rows, g_rows = [], [], []
    for hd in range(PEER_HEADS):
        tops = []
        for c in range(2):
            col = (hd * 2 + c) * half
            s_t = lax.dot_general(keys_ref[c], q[:, col:col + half], _NT, preferred_element_type=F32)
            tops.append(_top_rows(s_t, iota_k, n))
        (v1, id1), (v2, id2) = tops
        stacked = {(w, rows): jnp.concatenate((v1, v2)[w - 1][:rows], axis=0) for w in (1, 2) for rows in (8, n)}
        sums = [(v1[fixed] + stacked[2, rows]) if which == 1 else (stacked[1, rows] + v2[fixed])
                for fixed, which, rows, _, _ in pieces]
        cand = jnp.where(cand_ok, jnp.concatenate(sums, axis=0), NEG_INF)
        cvals, cidx = _top_rows(cand, cand_id, n)
        cv = jnp.concatenate(cvals, axis=0)
        ci = jnp.concatenate(cidx, axis=0)
        r1 = jnp.floor(ci * (1.0 / PEER_TOPK))
        r2 = ci - r1 * PEER_TOPK
        i1 = jnp.zeros_like(ci)
        i2 = jnp.zeros_like(ci)
        for r in range(PEER_TOPK):
            i1 = i1 + jnp.where(r1 == float(r), id1[r], 0.0)
            i2 = i2 + jnp.where(r2 == float(r), id2[r], 0.0)
        e = jnp.exp(cv - cvals[0])
        gate = e / jnp.sum(e, axis=0, keepdims=True)
        i1_rows.append(i1)
        i2_rows.append(i2)
        g_rows.append(gate)
    i1_ref[0] = jnp.concatenate(i1_rows, axis=0).T
    i2_ref[0] = jnp.concatenate(i2_rows, axis=0).T
    g_ref[0] = jnp.concatenate(g_rows, axis=0).T


def _mod_specs(rows, S, tm, D):
    if rows == S:
        return pl.BlockSpec((1, tm, D), lambda b, i, *_: (b, i, 0))
    return pl.BlockSpec((1, 1, D), lambda b, i, *_: (b, 0, 0))


def peer_route(x, shift, scale, wq, keys, tm):
    nb, S, D = x.shape
    nsel = PEER_HEADS * PEER_TOPK
    mspec = _mod_specs(shift.shape[1], S, tm, D)
    return pl.pallas_call(
        _peer_route_kernel,
        out_shape=[jax.ShapeDtypeStruct((nb, S, nsel), F32)] * 3,
        grid=(nb, S // tm),
        in_specs=[pl.BlockSpec((1, tm, D), lambda b, i: (b, i, 0)), mspec, mspec,
                  pl.BlockSpec(wq.shape, lambda b, i: (0, 0)),
                  pl.BlockSpec(keys.shape, lambda b, i: (0, 0, 0))],
        out_specs=[pl.BlockSpec((1, tm, nsel), lambda b, i: (b, i, 0))] * 3,
        compiler_params=_cparams("parallel", "parallel"),
        name="peer_route",
    )(x, shift, scale, wq, keys)


PEER_TOKEN_GROUP = 16


def _peer_expert_kernel(x_ref, sh_ref, sc_ref, gt_ref, i1_ref, i2_ref, g_ref, ut_ref, v_ref, lg_ref, lb_ref,
                        y_ref, w3_ref, acc_ref, h_ref, *, ac):
    tm = x_ref.shape[1]
    j = pl.program_id(2)

    @pl.when(j == 0)
    def _():
        h_ref[...] = (x_ref[0] * (1.0 + sc_ref[0]) + sh_ref[0]).astype(BF16)
        acc_ref[...] = jnp.zeros_like(acc_ref)
        iota_s = lax.broadcasted_iota(jnp.int32, (PEER_KEYS, LANES), 0).astype(F32)

        def build(gi, carry):
            t0 = pl.multiple_of(gi * PEER_TOKEN_GROUP, PEER_TOKEN_GROUP)
            i1g = i1_ref[0, pl.ds(t0, PEER_TOKEN_GROUP), :]
            i2g = i2_ref[0, pl.ds(t0, PEER_TOKEN_GROUP), :]
            gg = g_ref[0, pl.ds(t0, PEER_TOKEN_GROUP), :]
            tiles = []
            for t in range(PEER_TOKEN_GROUP):
                g1t = jnp.where(iota_s == i1g[t:t + 1], gg[t:t + 1], 0.0).astype(BF16)
                o2t = jnp.where(iota_s == i2g[t:t + 1], 1.0, 0.0).astype(BF16)
                tiles.append(lax.dot_general(g1t, o2t, _NT, preferred_element_type=F32))
            w3_ref[:, pl.ds(t0, PEER_TOKEN_GROUP), :] = jnp.swapaxes(jnp.stack(tiles, axis=0), 0, 1).astype(BF16)
            return carry

        lax.fori_loop(0, tm // PEER_TOKEN_GROUP, build, 0)

    act = _gelu(jnp.dot(h_ref[...], ut_ref[...], preferred_element_type=F32))
    wc = jnp.concatenate([w3_ref[j * ac + a] for a in range(ac)], axis=1).astype(F32)
    coef = (wc * act).astype(BF16)
    acc_ref[...] += jnp.dot(coef, v_ref[...], preferred_element_type=F32)

    @pl.when(j == pl.num_programs(2) - 1)
    def _():
        y_ref[0] = _post_norm_math(x_ref[0], acc_ref[...], gt_ref[0], lg_ref[...], lb_ref[...])


def peer_experts(x, shift, scale, gate, i1, i2, g, ut, v, ln_g, ln_b, tm, ac):
    nb, S, D = x.shape
    assert tm % PEER_TOKEN_GROUP == 0 and S % tm == 0, (S, tm)
    nsel = PEER_HEADS * PEER_TOPK
    ce = ac * PEER_KEYS
    mspec = _mod_specs(shift.shape[1], S, tm, D)
    pick =pl.BlockSpec((1, tm, nsel), lambda b, i, j: (b, i, 0))
    return pl.pallas_call(
        functools.partial(_peer_expert_kernel, ac=ac),
        out_shape=jax.ShapeDtypeStruct(x.shape, F32),
        grid=(nb, S // tm, PEER_KEYS // ac),
        in_specs=[pl.BlockSpec((1, tm, D), lambda b, i, j: (b, i, 0)), mspec, mspec, mspec, pick, pick, pick,
                  pl.BlockSpec((D, ce), lambda b, i, j: (0, j)),
                  pl.BlockSpec((ce, D), lambda b, i, j: (j, 0)),
                  pl.BlockSpec((1, D), lambda b, i, j: (0, 0)),
                  pl.BlockSpec((1, D), lambda b, i, j: (0, 0))],
        out_specs=pl.BlockSpec((1, tm, D), lambda b, i, j: (b, i, 0)),
        scratch_shapes=[pltpu.VMEM((PEER_KEYS, tm, PEER_KEYS), BF16),
                        pltpu.VMEM((tm, D), F32),
                        pltpu.VMEM((tm, D), BF16)],
        compiler_params=_cparams("parallel", "parallel", "arbitrary"),
        name="peer_experts",
    )(x, shift, scale, gate, i1, i2, g, ut, v, ln_g.reshape(1, D), ln_b.reshape(1, D))


def peer_layer(x, shift, scale, gate, wq, keys, ut, v, ln_g, ln_b, tm_route, tm):
    i1, i2, g = peer_route(x, shift, scale, wq, keys, tm_route)
    return peer_experts(x, shift, scale, gate, i1, i2, g, ut, v, ln_g, ln_b, tm, ac=8)


INT_MIN = -2 ** 31
NEG_INF = float("-inf")


def _to_key(x):
    b = lax.bitcast_convert_type(x, jnp.int32)
    return b ^ ((b >> 31) & 0x7FFFFFFF)


COUNT_ROWS = 64


def _key_tile(key_ref, c0, kc, axis):
    return key_ref[:, pl.ds(c0, kc)] if axis == 1 else key_ref[pl.ds(c0, kc), :]


def _count_keys(key_ref, nck, kc, pred, axis):
    other = key_ref.shape[1 - axis]

    def body(c, acc):
        c0 = pl.multiple_of(c * kc, kc)
        hit = jnp.where(pred(_key_tile(key_ref, c0, kc, axis), c0), 1.0, 0.0)
        if axis == 1:
            part = hit[:, 0:LANES]
            for u in range(1, kc // LANES):
                part = part + hit[:, u * LANES:(u + 1) * LANES]
        else:
            part = jnp.sum(hit.reshape(kc // COUNT_ROWS, COUNT_ROWS, other), axis=0)
        return acc + part

    acc = lax.fori_loop(0, nck, body, jnp.zeros((other, LANES) if axis == 1 else (COUNT_ROWS, other), F32))
    return jnp.sum(acc, axis=axis, keepdims=True)


HALF = 1 << 15


def _count16(ref16, nck, kc, cand, strict=False):
    cols = ref16.shape[1]
    cand16 = cand.astype(jnp.int16)

    def body(c, acc):
        tile = ref16[pl.ds(pl.multiple_of(c * kc, kc), kc), :]
        hit = jnp.where((tile > cand16) if strict else (tile >= cand16), jnp.int16(1), jnp.int16(0))
        hit = hit.reshape(kc // COUNT_ROWS, COUNT_ROWS, cols)
        for u in range(kc // COUNT_ROWS):
            acc = acc + hit[u]
        return acc

    acc = lax.fori_loop(0, nck, body, jnp.zeros((COUNT_ROWS, cols), jnp.int16))
    return jnp.sum(acc.astype(F32), axis=0, keepdims=True)


def _search16(ref16, nck, kc, need):
    def bit_step(p, t_u):
        cand_u = t_u | jnp.left_shift(jnp.int32(1), 15 - p)
        return jnp.where(_count16(ref16, nck, kc, cand_u - HALF) >= need, cand_u, t_u)

    return lax.fori_loop(0, 16, bit_step, jnp.zeros((1, ref16.shape[1]), jnp.int32))


def _threshold_by_halves(key_ref, hi_ref, lo_ref, nck, kc, kf):
    def split(c, carry):
        rows = pl.ds(pl.multiple_of(c * kc, kc), kc)
        hi_ref[rows, :] = (key_ref[rows, :] >> 16).astype(jnp.int16)
        return carry

    lax.fori_loop(0, nck, split, 0)
    t_hi = _search16(hi_ref, nck, kc, kf) - HALF
    above = _count16(hi_ref, nck, kc, t_hi, strict=True)

    def low(c, carry):
        rows = pl.ds(pl.multiple_of(c * kc, kc), kc)
        key = key_ref[rows, :]
        lo_ref[rows, :] = jnp.where((key >> 16) == t_hi, (key & 0xFFFF) - HALF, -HALF).astype(jnp.int16)
        return carry

    lax.fori_loop(0, nck, low, 0)
    t_lo = _search16(lo_ref, nck, kc, kf - above)
    return t_hi * (1 << 16) + t_lo


def _topk_threshold(key_ref, nck, kc, k, idx_bits, axis=1, halves=None):
    other = key_ref.shape[1 - axis]
    vec = (other, 1) if axis == 1 else (1, other)
    kf = float(k)

    def bit_step(p, t_u):
        cand_u = t_u | jnp.left_shift(jnp.int32(1), 31 - p)
        cand_s = cand_u ^ INT_MIN
        cnt = _count_keys(key_ref, nck, kc, lambda tile, c0: tile >= cand_s, axis)
        return jnp.where(cnt >= kf, cand_u, t_u)

    if halves is None:
        t_s = lax.fori_loop(0, 32, bit_step, jnp.zeros(vec, jnp.int32)) ^ INT_MIN
    else:
        t_s = _threshold_by_halves(key_ref, *halves, nck, kc, kf)
    thr = jnp.maximum(t_s, INT_MIN + 1)
    n_ge = _count_keys(key_ref, nck, kc, lambda tile, c0: tile >= thr, axis)

    @pl.when(jnp.max(n_ge) > kf)
    def _():
        need = kf - _count_keys(key_ref, nck, kc, lambda tile, c0: tile > thr, axis)
        iota = lax.broadcasted_iota(jnp.int32, (other, kc) if axis == 1 else (kc, other), axis)

        def idx_step(p, j_hi):
            cand = j_hi | jnp.left_shift(jnp.int32(1), idx_bits - 1 - p)
            cnt = _count_keys(key_ref, nck, kc, lambda tile, c0: (tile == thr) & (c0 + iota < cand), axis)
            return jnp.where(cnt <= need, cand, j_hi)

        j_hi = lax.fori_loop(0, idx_bits, idx_step, jnp.zeros(vec, jnp.int32))
        surplus = n_ge > kf

        def lower(c, carry):
            c0 = pl.multiple_of(c * kc, kc)
            tile = _key_tile(key_ref, c0, kc, axis)
            drop = (tile == thr) & (c0 + iota >= j_hi) & surplus
            tile = jnp.where(drop, thr - 1, tile)
            if axis == 1:
                key_ref[:, pl.ds(c0, kc)] = tile
            else:
                key_ref[pl.ds(c0, kc), :] = tile
            return carry

        lax.fori_loop(0, nck, lower, 0)

    return thr


def _stack_heads(q_ref, qs_ref, tq):
    for hd in range(N_HEADS):
        qs_ref[hd * tq:(hd + 1) * tq, :] = (q_ref[0, :, hd * HEAD_DIM:(hd + 1) * HEAD_DIM] * ATTN_SCALE).astype(BF16)


def _flash_init(m_ref, l_ref, acc_ref):
    m_ref[...] = jnp.full(m_ref.shape, NEG_INF, F32)
    l_ref[...] = jnp.zeros(l_ref.shape, F32)
    acc_ref[...] = jnp.zeros(acc_ref.shape, F32)


def _flash_result(rows, l_ref, acc_ref):
    return acc_ref[rows, :] / jnp.maximum(l_ref[rows, :HEAD_DIM], 1e-30)


def _grouped_flash_step(mask_of, kch, vch, qs_ref, m_ref, l_ref, acc_ref, tq):
    rows_h = GROUP * tq
    kc = kch.shape[0]
    parts = []
    for h in range(N_KV_HEADS):
        logits = lax.dot_general(qs_ref[h * rows_h:(h + 1) * rows_h, :], kch[:, h * HEAD_DIM:(h + 1) * HEAD_DIM],
                                 _NT, preferred_element_type=F32)
        parts.append(jnp.where(mask_of(h)[None], logits.reshape(GROUP, tq, kc), NEG_INF).reshape(rows_h, kc))
    s = jnp.concatenate(parts, axis=0)
    m_prev = m_ref[...]
    m_new = jnp.maximum(m_prev, jnp.max(s, axis=1, keepdims=True))
    m_safe = jnp.where(m_new == NEG_INF, 0.0, m_new)
    p = jnp.exp(s - jnp.concatenate([m_safe] * (kc // LANES), axis=1))
    alpha = jnp.exp(m_prev - m_safe)
    l_ref[...] = alpha * l_ref[...] + jnp.sum(p, axis=1, keepdims=True)
    pb = p.astype(BF16)
    pv = [jnp.dot(pb[h * rows_h:(h + 1) * rows_h], vch[:, h * HEAD_DIM:(h + 1) * HEAD_DIM],
                  preferred_element_type=F32) for h in range(N_KV_HEADS)]
    acc_ref[...] = alpha[:, :HEAD_DIM] * acc_ref[...] + jnp.concatenate(pv, axis=0)
    m_ref[...] = m_new


def _index_scores(qi, wi, kic):
    sc = None
    for hh in range(IDX_HEADS):
        s = lax.dot_general(qi[:, hh * IDX_DIM:(hh + 1) * IDX_DIM], kic, _NT, preferred_element_type=F32)
        term = jnp.maximum(s, 0.0) * wi[:, hh:hh + 1]
        sc = term if sc is None else sc + term
    return sc


def _dsa_prompt_kernel(q_ref, qi_ref, wi_ref, k_ref, vt_ref, ki_ref, o_ref, key_ref, hi_ref, lo_ref, *fl,
                       kc, topk, idx_bits):
    qs_ref = fl[0]
    tq = q_ref.shape[1]
    t0 = pl.program_id(1) * tq
    nck = (t0 + tq + kc - 1) // kc
    _stack_heads(q_ref, qs_ref, tq)
    qpos = t0 + lax.broadcasted_iota(jnp.int32, (1, tq), 1)
    kiota = lax.broadcasted_iota(jnp.int32, (kc, 1), 0)
    qi = qi_ref[0]
    wi = wi_ref[0] * IDX_SCALE

    def score(c, carry):
        c0 = pl.multiple_of(c * kc, kc)
        sc = _index_scores(qi, wi, ki_ref[0, pl.ds(c0, kc), :])
        key_ref[pl.ds(c0, kc), :] = jnp.where(c0 + kiota <= qpos, _to_key(sc.T), INT_MIN)
        return carry

    lax.fori_loop(0, nck, score, 0)
    thr = _topk_threshold(key_ref, nck, kc, topk, idx_bits, axis=0, halves=(hi_ref, lo_ref))
    _flash_init_t(*fl)

    def attend(c, carry):
        c0 = pl.multiple_of(c * kc, kc)
        bias = jnp.where(key_ref[pl.ds(c0, kc), :] >= thr, 0.0, NEG_INF)
        _flash_step_t(lambda h: bias, k_ref[0, pl.ds(c0, kc), :], vt_ref[0, :, pl.ds(c0, kc)], *fl, tq=tq)
        return carry

    lax.fori_loop(0, nck, attend, 0)
    for h in range(N_KV_HEADS):
        res = _flash_result_t(h, *fl)
        for g in range(GROUP):
            hd = h * GROUP + g
            o_ref[0, :, hd * HEAD_DIM:(hd + 1) * HEAD_DIM] = res[g * tq:(g + 1) * tq].astype(o_ref.dtype)


ONES_ROWS = 16
FLASH_MAX_ROWS = 16


def _flash_scratch_t(tq, kmax):
    cols = GROUP * tq
    return [pltpu.VMEM((N_HEADS * tq, HEAD_DIM), BF16),
            pltpu.VMEM((N_KV_HEADS, 1, cols), F32),
            pltpu.VMEM((N_KV_HEADS, HEAD_DIM + ONES_ROWS, cols), F32),
            pltpu.VMEM((N_KV_HEADS, kmax, cols), F32),
            pltpu.VMEM((N_KV_HEADS, kmax, cols), BF16)]


def _flash_init_t(qs_ref, m_ref, acc_ref, s_ref, p_ref):
    m_ref[...] = jnp.full(m_ref.shape, NEG_INF, F32)
    acc_ref[...] = jnp.zeros(acc_ref.shape, F32)


def _flash_step_t(bias_of, kch, vtch, qs_ref, m_ref, acc_ref, s_ref, p_ref, *, tq):
    cols_h = GROUP * tq
    kc = kch.shape[0]
    ones = jnp.ones((ONES_ROWS, kc), BF16)
    hcols = lambda h: slice(h * HEAD_DIM, (h + 1) * HEAD_DIM)

    def logits(h):
        st = lax.dot_general(kch[:, hcols(h)], qs_ref[h * cols_h:(h + 1) * cols_h, :], _NT,
                             preferred_element_type=F32)
        s_ref[h, 0:kc, :] = st + jnp.concatenate([bias_of(h)] * GROUP, axis=1)
        mx = s_ref[h, 0:FLASH_MAX_ROWS, :]
        for r in range(FLASH_MAX_ROWS, kc, FLASH_MAX_ROWS):
            mx = jnp.maximum(mx, s_ref[h, r:r + FLASH_MAX_ROWS, :])
        return jnp.max(mx, axis=0, keepdims=True)

    m_cur = logits(0)
    for h in range(N_KV_HEADS):
        m_next = logits(h + 1) if h + 1 < N_KV_HEADS else None
        m_prev = m_ref[h]
        m_new = jnp.maximum(m_prev, m_cur)
        m_safe = jnp.where(m_new == NEG_INF, 0.0, m_new)
        p_ref[h, 0:kc, :] = jnp.exp(s_ref[h, 0:kc, :] - m_safe).astype(BF16)
        va = jnp.concatenate([vtch[hcols(h), :], ones], axis=0)
        acc_ref[h] = jnp.exp(m_prev - m_safe) * acc_ref[h] + jnp.dot(va, p_ref[h, 0:kc, :],
                                                                     preferred_element_type=F32)
        m_ref[h] = m_new
        m_cur = m_next


def _flash_result_t(h, qs_ref, m_ref, acc_ref, s_ref, p_ref):
    acc = acc_ref[h]
    return (acc[0:HEAD_DIM] / jnp.maximum(acc[HEAD_DIM:HEAD_DIM + 1], 1e-30)).T


def _flash_scratch(tq):
    return [pltpu.VMEM((N_HEADS * tq, HEAD_DIM), BF16),
            pltpu.VMEM((N_HEADS * tq, LANES), F32),
            pltpu.VMEM((N_HEADS * tq, LANES), F32),
            pltpu.VMEM((N_HEADS * tq, HEAD_DIM), F32)]


def dsa_prompt_attend(q, qi, wi, k, vt, ki, tq=Q_BLOCK, kc=512):
    B, S, D = q.shape
    kc = min(kc, S)
    topk = min(DSA_TOPK, S // 4)
    blk = lambda w: pl.BlockSpec((1, tq, w), lambda b, i: (b, i, 0))
    full = lambda w: pl.BlockSpec((1, S, w), lambda b, i: (b, 0, 0))
    return pl.pallas_call(
        functools.partial(_dsa_prompt_kernel, kc=kc, topk=topk, idx_bits=S.bit_length()),
        out_shape=jax.ShapeDtypeStruct((B, S, D), BF16),
        grid=(B, S // tq),
        in_specs=[blk(D), blk(qi.shape[-1]), blk(wi.shape[-1]), full(KV_WIDTH),
                  pl.BlockSpec((1, KV_WIDTH, S), lambda b, i: (b, 0, 0)), full(IDX_DIM)],
        out_specs=blk(D),
        scratch_shapes=[pltpu.VMEM((S, tq), jnp.int32), pltpu.VMEM((S, tq), jnp.int16),
                        pltpu.VMEM((S, tq), jnp.int16)] + _flash_scratch_t(tq, kc),
        compiler_params=_cparams("parallel", "arbitrary"),
        name="dsa_prompt_attend",
    )(q, qi, wi, k, vt, ki)


def _dsa_sample_kernel(pt_ref, q_ref, qi_ref, wi_ref, kvn_ref, kin_ref, *refs, pg, kc, topk, idx_bits, n_new):
    ckv_refs, cki_refs = refs[:pg], refs[pg:2 * pg]
    o_ref, key_ref, kv_ref, qs_ref, m_ref, l_ref, acc_ref = refs[2 * pg:]
    tq = q_ref.shape[1]
    j = pl.program_id(1)
    n_steps = pl.num_programs(1)
    past = kv_ref.shape[0] - PAGE_SIZE
    qi = qi_ref[0].astype(BF16)
    wi = wi_ref[0] * IDX_SCALE
    span = pg * PAGE_SIZE
    s0 = pl.multiple_of(j * span, span)
    for k in range(pg):
        kv_ref[pl.ds(s0 + k * PAGE_SIZE, PAGE_SIZE), :] = ckv_refs[k][0, 0].astype(BF16)
    ki_step = jnp.concatenate([r[0, 0] for r in cki_refs], axis=0).astype(BF16)
    key_ref[:, pl.ds(s0, span)] = _to_key(_index_scores(qi, wi, ki_step))

    @pl.when(j == n_steps - 1)
    def _():
        nck = (past + PAGE_SIZE) // kc
        kv_ref[past:past + PAGE_SIZE, :] = kvn_ref[0]
        trow = lax.broadcasted_iota(jnp.int32, (tq, PAGE_SIZE), 0)
        ncol = lax.broadcasted_iota(jnp.int32, (tq, PAGE_SIZE), 1)
        visible = (ncol <= trow) & (ncol < n_new)
        key_ref[:, past:past + PAGE_SIZE] = jnp.where(visible, _to_key(_index_scores(qi, wi, kin_ref[0])), INT_MIN)
        _stack_heads(q_ref, qs_ref, tq)
        thr = _topk_threshold(key_ref, nck, kc, topk, idx_bits)
        _flash_init(m_ref, l_ref, acc_ref)

        def attend(c, carry):
            c0 = pl.multiple_of(c * kc, kc)
            sel = key_ref[:, pl.ds(c0, kc)] >= thr
            _grouped_flash_step(lambda h: sel, kv_ref[pl.ds(c0, kc), 0:KV_WIDTH],
                                kv_ref[pl.ds(c0, kc), KV_WIDTH:2 * KV_WIDTH], qs_ref, m_ref, l_ref, acc_ref, tq)
            return carry

        lax.fori_loop(0, nck, attend, 0)
        for hd in range(N_HEADS):
            o_ref[0, :, hd * HEAD_DIM:(hd + 1) * HEAD_DIM] = _flash_result(
                slice(hd * tq, (hd + 1) * tq), l_ref, acc_ref).astype(o_ref.dtype)


PAGES_PER_STEP = 16


def _pages_per_step(n_pages):
    return max(d for d in range(1, PAGES_PER_STEP + 1) if n_pages % d == 0)


def _page_specs(width, col, layer, pg):
    return [pl.BlockSpec((1, 1, PAGE_SIZE, width), lambda b, j, pt, k=k: (layer, pt[b, j * pg + k], 0, col))
            for k in range(pg)]


def _key_chunk(total):
    n = total // LANES
    return LANES * max(d for d in range(1, 9) if n % d == 0)


def dsa_sample_attend(q, qi, wi, kv_new, ki_new, cache_kv, cache_ki, layer, page_table, n_new):
    DB, tq, D = q.shape
    n_pages = page_table.shape[1]
    past = n_pages * PAGE_SIZE
    total = past + PAGE_SIZE
    kc = _key_chunk(total)
    topk = min(DSA_TOPK, (past + n_new) // 4)
    per_b = lambda r, w: pl.BlockSpec((1, r, w), lambda b, j, pt: (b, 0, 0))
    pg = _pages_per_step(n_pages)
    grid_spec = pltpu.PrefetchScalarGridSpec(
        num_scalar_prefetch=1,
        grid=(DB, n_pages // pg),
        in_specs=[per_b(tq, D), per_b(tq, qi.shape[-1]), per_b(tq, wi.shape[-1]),
                  per_b(PAGE_SIZE, 2 * KV_WIDTH), per_b(PAGE_SIZE, IDX_DIM)]
        + _page_specs(2 * KV_WIDTH, 0, layer, pg) + _page_specs(IDX_DIM, 0, layer, pg),
        out_specs=per_b(tq, D),
        scratch_shapes=[pltpu.VMEM((tq, total), jnp.int32),
                        pltpu.VMEM((total, 2 * KV_WIDTH), BF16)] + _flash_scratch(tq),
    )
    return pl.pallas_call(
        functools.partial(_dsa_sample_kernel, pg=pg, kc=kc, topk=topk, idx_bits=total.bit_length(), n_new=n_new),
        out_shape=jax.ShapeDtypeStruct((DB, tq, D), BF16),
        grid_spec=grid_spec,
        compiler_params=_cparams("parallel", "arbitrary"),
        name="dsa_sample_attend",
    )(page_table, q, qi, wi, kv_new, ki_new, *([cache_kv] * pg), *([cache_ki] * pg))


def _nsa_compress_kernel(pt_ref, *refs, pg):
    x_refs = refs[:pg]
    pe_ref, w1_ref, w2_ref, o_ref, rows_ref = refs[pg:]
    j = pl.program_id(2)
    pair = 2 * CMP_LEN
    per_step = pg * PAGE_SIZE // pair
    x = jnp.concatenate([r[0, 0] for r in x_refs], axis=0) if pg > 1 else x_refs[0][0, 0]
    rows_ref[:, pl.ds(pl.multiple_of(j * per_step, per_step), per_step), :] = jnp.swapaxes(
        x.reshape(per_step, pair, x.shape[-1]), 0, 1)

    @pl.when(j == pl.num_programs(2) - 1)
    def _():
        acc = None
        for l in range(CMP_LEN):
            xl = (jnp.concatenate([rows_ref[l], rows_ref[CMP_LEN + l]], axis=0) + pe_ref[0, l:l + 1, :]).astype(BF16)
            part = jnp.dot(xl, w1_ref[0, l], preferred_element_type=F32)
            acc = part if acc is None else acc + part
        hid = _gelu(acc).astype(BF16)
        o_ref[0] = jnp.dot(hid, w2_ref[0], preferred_element_type=F32).astype(o_ref.dtype)


def nsa_compress(rows, layer, page_table, pe_t, w1_bd, w2_bd):
    NB, n_pages = page_table.shape
    nc = n_pages * PAGE_SIZE // CMP_LEN
    pg = _pages_per_step(n_pages)
    grid_spec = pltpu.PrefetchScalarGridSpec(
        num_scalar_prefetch=1,
        grid=(NB, 2, n_pages // pg),
        in_specs=[pl.BlockSpec((1, 1, PAGE_SIZE, KV_WIDTH), lambda b, s, j, pt, k=k: (layer, pt[b, j * pg + k], 0, s))
                  for k in range(pg)]
        + [pl.BlockSpec((1, CMP_LEN, KV_WIDTH), lambda b, s, j, pt: (s, 0, 0)),
           pl.BlockSpec((1, CMP_LEN, KV_WIDTH, N_KV_HEADS * CMP_HIDDEN), lambda b, s, j, pt: (s, 0, 0, 0)),
           pl.BlockSpec((1, N_KV_HEADS * CMP_HIDDEN, KV_WIDTH), lambda b, s, j, pt: (s, 0, 0))],
        out_specs=pl.BlockSpec((1, nc, KV_WIDTH), lambda b, s, j, pt: (b, 0, s)),
        scratch_shapes=[pltpu.VMEM((2 * CMP_LEN, nc // 2, KV_WIDTH), F32)],
    )
    return pl.pallas_call(
        functools.partial(_nsa_compress_kernel, pg=pg),
        out_shape=jax.ShapeDtypeStruct((NB, nc, 2 * KV_WIDTH), BF16),
        grid_spec=grid_spec,
        compiler_params=_cparams("parallel", "arbitrary", "arbitrary"),
        name="nsa_compress",
    )(page_table, *([rows] * pg), pe_t, w1_bd, w2_bd)


def nsa_compress_weights(cmp_pe, cmp_w1, cmp_w2):
    eye = jnp.eye(N_KV_HEADS, dtype=F32)
    w1 = cmp_w1.reshape(2, CMP_LEN, HEAD_DIM, CMP_HIDDEN)
    w1_bd = jnp.einsum('hg,kldj->klhdgj', eye, w1).reshape(2, CMP_LEN, KV_WIDTH, N_KV_HEADS * CMP_HIDDEN)
    w2_bd = jnp.einsum('hg,kjd->khjgd', eye, cmp_w2).reshape(2, N_KV_HEADS * CMP_HIDDEN, KV_WIDTH)
    pe_t = jnp.tile(cmp_pe, (1, 1, N_KV_HEADS))
    return pe_t, w1_bd.astype(BF16), w2_bd.astype(BF16)


def _nsa_compressed_branch(qs_ref, kvc, qpos, gate_of, out_ref, tq):
    nc = kvc.shape[0]
    half = nc // 2
    pcol = lax.broadcasted_iota(jnp.int32, (1, nc), 1)
    cidx = jnp.where(pcol < half, 2 * pcol, 2 * (pcol - half) + 1)
    visible = (cidx + 1) * CMP_LEN - 1 <= qpos
    imps = []
    for h in range(N_KV_HEADS):
        cols = slice(h * HEAD_DIM, (h + 1) * HEAD_DIM)
        logits = lax.dot_general(qs_ref[h * GROUP * tq:(h + 1) * GROUP * tq, :], kvc[:, cols], _NT,
                                 preferred_element_type=F32)
        vcols = slice(KV_WIDTH + h * HEAD_DIM, KV_WIDTH + (h + 1) * HEAD_DIM)
        imp = None
        for g in range(GROUP):
            hd = h * GROUP + g
            s = jnp.where(visible, logits[g * tq:(g + 1) * tq], NEG_INF)
            m = jnp.max(s, axis=1, keepdims=True)
            e = jnp.exp(s - jnp.where(m == NEG_INF, 0.0, m))
            pc = e / jnp.maximum(jnp.sum(e, axis=1, keepdims=True), 1e-30)
            o = jnp.dot(pc.astype(BF16), kvc[:, vcols], preferred_element_type=F32)
            out_ref[hd * tq:(hd + 1) * tq, :] = gate_of(hd, 0) * o
            pair = pc[:, :half] + pc[:, half:]
            imp = pair if imp is None else imp + pair
        imps.append(imp)
    return imps


def _select_blocks(imp, cur, n_blocks, n_pick, axis):
    blk = lax.broadcasted_iota(jnp.int32, imp.shape, axis)
    forced = (blk == 0) | (blk == cur) | (blk == cur - 1)
    x = jnp.where(forced, 16.0, imp)
    x = jnp.where(blk <= cur, x, -1.0)
    x = jnp.where(blk < n_blocks, x, -2.0)
    blk_f = blk.astype(F32)
    sel = jnp.zeros(imp.shape, F32)
    for _ in range(n_pick):
        m = jnp.max(x, axis=axis, keepdims=True)
        first = jnp.min(jnp.where(x == m, blk_f, float(imp.shape[axis])), axis=axis, keepdims=True)
        hit = blk_f == first
        sel = jnp.where(hit, 1.0, sel)
        x = jnp.where(hit, -2.0, x)
    return sel


def _nsa_finish_branch(br, gate_of, out_ref, l_ref, acc_ref, tq):
    for hd in range(N_HEADS):
        rows = slice(hd * tq, (hd + 1) * tq)
        out_ref[rows, :] = out_ref[rows, :] + gate_of(hd, br) * _flash_result(rows, l_ref, acc_ref)


def _nsa_selected_branch(sel_ref, kv_chunk, nck, kc, qpos, qs_ref, m_ref, l_ref, acc_ref, tq):
    ns_pad = sel_ref.shape[2]
    _flash_init(m_ref, l_ref, acc_ref)
    brow = lax.broadcasted_iota(jnp.int32, (ns_pad, kc), 0)
    kcol = lax.broadcasted_iota(jnp.int32, (ns_pad, kc), 1)
    kiota = lax.broadcasted_iota(jnp.int32, (1, kc), 1)

    def attend(c, carry):
        c0 = pl.multiple_of(c * kc, kc)
        expand = jnp.where((c0 + kcol) // SEL_BLOCK == brow, 1.0, 0.0).astype(BF16)
        causal = c0 + kiota <= qpos
        kch, vch = kv_chunk(c0)

        def mask_of(h):
            picked = jnp.dot(sel_ref[h], expand, preferred_element_type=F32)
            return jnp.where(causal, picked, 0.0) > 0.5

        _grouped_flash_step(mask_of, kch, vch, qs_ref, m_ref, l_ref, acc_ref, tq)
        return carry

    lax.fori_loop(0, nck, attend, 0)


def _nsa_window_branch(kw, vw, kwpos, qpos, qs_ref, m_ref, l_ref, acc_ref, tq):
    _flash_init(m_ref, l_ref, acc_ref)
    dist = qpos - kwpos
    wmask = (dist >= 0) & (dist < WINDOW) & (kwpos >= 0)
    _grouped_flash_step(lambda h: wmask, kw, vw, qs_ref, m_ref, l_ref, acc_ref, tq)


def _gate_fn(gates_ref):
    sig = jax.nn.sigmoid(gates_ref[0])
    return lambda hd, br: sig[:, hd * 3 + br:hd * 3 + br + 1]


def _nsa_write_out(o_ref, out_ref, tq):
    for hd in range(N_HEADS):
        o_ref[0, :, hd * HEAD_DIM:(hd + 1) * HEAD_DIM] = out_ref[hd * tq:(hd + 1) * tq, :].astype(o_ref.dtype)


def _nsa_finish_branch_t(br, gate_of, out_ref, fl, tq):
    for h in range(N_KV_HEADS):
        res = _flash_result_t(h, *fl)
        for g in range(GROUP):
            hd = h * GROUP + g
            rows = slice(hd * tq, (hd + 1) * tq)
            out_ref[rows, :] = out_ref[rows, :] + gate_of(hd, br) * res[g * tq:(g + 1) * tq]


def _nsa_prompt_kernel(q_ref, gates_ref, kvc_ref, kk_ref, vt_ref, o_ref, selt_ref, out_ref, *fl, kc, wlen):
    qs_ref = fl[0]
    tq = q_ref.shape[1]
    S = kk_ref.shape[1]
    t0 = pl.program_id(1) * tq
    _stack_heads(q_ref, qs_ref, tq)
    qpos = t0 + lax.broadcasted_iota(jnp.int32, (tq, 1), 0)
    qpos_row = t0 + lax.broadcasted_iota(jnp.int32, (1, tq), 1)
    gate_of = _gate_fn(gates_ref)
    imps = _nsa_compressed_branch(qs_ref, kvc_ref[0], qpos, gate_of, out_ref, tq)
    n_blocks = S // SEL_BLOCK
    for h in range(N_KV_HEADS):
        selt_ref[h] = _select_blocks(imps[h].T, qpos_row // SEL_BLOCK, n_blocks, min(N_SEL, n_blocks), 0).astype(BF16)
    nck = (t0 + tq + kc - 1) // kc
    _flash_init_t(*fl)
    krow = lax.broadcasted_iota(jnp.int32, (kc, n_blocks), 0)
    bcol = lax.broadcasted_iota(jnp.int32, (kc, n_blocks), 1)
    kiota = lax.broadcasted_iota(jnp.int32, (kc, 1), 0)

    def attend(c, carry):
        c0 = pl.multiple_of(c * kc, kc)
        expand = jnp.where((c0 + krow) // SEL_BLOCK == bcol, 1.0, 0.0).astype(BF16)
        causal = c0 + kiota <= qpos_row

        def bias_of(h):
            picked = jnp.dot(expand, selt_ref[h], preferred_element_type=F32)
            return jnp.where(causal, jnp.where(picked > 0.5, 0.0, NEG_INF), NEG_INF)

        _flash_step_t(bias_of, kk_ref[0, pl.ds(c0, kc), 0:KV_WIDTH], vt_ref[0, 0:KV_WIDTH, pl.ds(c0, kc)],
                      *fl, tq=tq)
        return carry

    lax.fori_loop(0, nck, attend, 0)
    _nsa_finish_branch_t(1, gate_of, out_ref, fl, tq)
    start = pl.multiple_of(jnp.clip(t0 + tq - wlen, 0, S - wlen), tq)
    dist = qpos_row - (start + lax.broadcasted_iota(jnp.int32, (wlen, 1), 0))
    wbias = jnp.where(dist >= 0, jnp.where(dist < WINDOW, 0.0, NEG_INF), NEG_INF)
    _flash_init_t(*fl)
    _flash_step_t(lambda h: wbias, kk_ref[0, pl.ds(start, wlen), KV_WIDTH:2 * KV_WIDTH],
                  vt_ref[0, KV_WIDTH:2 * KV_WIDTH, pl.ds(start, wlen)], *fl, tq=tq)
    _nsa_finish_branch_t(2, gate_of, out_ref, fl, tq)
    _nsa_write_out(o_ref, out_ref, tq)


def nsa_prompt_attend(q, gates, kvc, kk, vt, tq=Q_BLOCK, kc=512):
    B, S, D = q.shape
    kc = min(kc, S)
    wlen = min(WINDOW + tq, S)
    n_blocks = S // SEL_BLOCK
    blk = lambda w: pl.BlockSpec((1, tq, w), lambda b, i: (b, i, 0))
    return pl.pallas_call(
        functools.partial(_nsa_prompt_kernel, kc=kc, wlen=wlen),
        out_shape=jax.ShapeDtypeStruct((B, S, D), BF16),
        grid=(B, S // tq),
        in_specs=[blk(D), blk(gates.shape[-1]),
                  pl.BlockSpec((1,) + kvc.shape[1:], lambda b, i: (b, 0, 0)),
                  pl.BlockSpec((1, S, 2 * KV_WIDTH), lambda b, i: (b, 0, 0), pipeline_mode=pl.Buffered(1)),
                  pl.BlockSpec((1, 2 * KV_WIDTH, S), lambda b, i: (b, 0, 0), pipeline_mode=pl.Buffered(1))],
        out_specs=blk(D),
        scratch_shapes=[pltpu.VMEM((N_KV_HEADS, n_blocks, tq), BF16),
                        pltpu.VMEM((N_HEADS * tq, HEAD_DIM), F32)] + _flash_scratch_t(tq, max(kc, wlen)),
        compiler_params=_cparams("parallel", "arbitrary"),
        name="nsa_prompt_attend",
    )(q, gates, kvc, kk, vt)


def _nsa_sample_kernel(pt_ref, q_ref, gates_ref, kvc_ref, kvn_ref, win_ref, winn_ref, *refs, pg, kc, n_new):
    csel_refs = refs[:pg]
    o_ref, kv_ref, wkv_ref, sel_ref, out_ref, qs_ref, m_ref, l_ref, acc_ref = refs[pg:]
    tq = q_ref.shape[1]
    j = pl.program_id(1)
    past = kv_ref.shape[0] - PAGE_SIZE
    for k in range(pg):
        p0 = pl.multiple_of((j * pg + k) * PAGE_SIZE, PAGE_SIZE)
        kv_ref[pl.ds(p0, PAGE_SIZE), :] = csel_refs[k][0, 0].astype(BF16)

    @pl.when(j == pl.num_programs(1) - 1)
    def _():
        kv_ref[past:past + PAGE_SIZE, :] = kvn_ref[0]
        wb = win_ref.shape[1]
        wkv_ref[0:wb, :] = win_ref[0].astype(BF16)
        wkv_ref[wb:wb + PAGE_SIZE, :] = winn_ref[0]
        _stack_heads(q_ref, qs_ref, tq)
        qpos = past + jnp.minimum(lax.broadcasted_iota(jnp.int32, (tq, 1), 0), n_new - 1)
        gate_of = _gate_fn(gates_ref)
        imps = _nsa_compressed_branch(qs_ref, kvc_ref[0], qpos, gate_of, out_ref, tq)
        n_blocks = -(-(past + n_new) // SEL_BLOCK)
        ns_pad = sel_ref.shape[2]
        for h in range(N_KV_HEADS):
            imp = imps[h]
            imp = jnp.concatenate([imp, jnp.zeros((tq, ns_pad - imp.shape[1]), F32)], axis=1)
            sel_ref[h] = _select_blocks(imp, qpos // SEL_BLOCK, n_blocks, min(N_SEL, n_blocks), 1).astype(BF16)
        nck = (past + PAGE_SIZE) // kc
        kv_chunk = lambda c0: (kv_ref[pl.ds(c0, kc), 0:KV_WIDTH], kv_ref[pl.ds(c0, kc), KV_WIDTH:2 * KV_WIDTH])
        _nsa_selected_branch(sel_ref, kv_chunk, nck, kc, qpos, qs_ref, m_ref, l_ref, acc_ref, tq)
        _nsa_finish_branch(1, gate_of, out_ref, l_ref, acc_ref, tq)
        kwpos = past - wb + lax.broadcasted_iota(jnp.int32, (1, wb + PAGE_SIZE), 1)
        _nsa_window_branch(wkv_ref[:, 0:KV_WIDTH], wkv_ref[:, KV_WIDTH:2 * KV_WIDTH], kwpos, qpos,
                           qs_ref, m_ref, l_ref, acc_ref, tq)
        _nsa_finish_branch(2, gate_of, out_ref, l_ref, acc_ref, tq)
        _nsa_write_out(o_ref, out_ref, tq)


def nsa_sample_attend(q, gates, kvc, sel_new, win_buf, win_new, cache_kv, layer, page_table, n_new):
    DB, tq, D = q.shape
    n_pages = page_table.shape[1]
    past = n_pages * PAGE_SIZE
    total = past + PAGE_SIZE
    kc = _key_chunk(total)
    wb = win_buf.shape[1]
    ns_pad = -(-(total // SEL_BLOCK) // LANES) * LANES
    per_b = lambda r, w: pl.BlockSpec((1, r, w), lambda b, j, pt: (b, 0, 0))
    pg = _pages_per_step(n_pages)
    grid_spec = pltpu.PrefetchScalarGridSpec(
        num_scalar_prefetch=1,
        grid=(DB, n_pages // pg),
        in_specs=[per_b(tq, D), per_b(tq, gates.shape[-1]), per_b(kvc.shape[1], kvc.shape[2]),
                  per_b(PAGE_SIZE, 2 * KV_WIDTH), per_b(wb, 2 * KV_WIDTH), per_b(PAGE_SIZE, 2 * KV_WIDTH)]
        + _page_specs(2 * KV_WIDTH, 1, layer, pg),
        out_specs=per_b(tq, D),
        scratch_shapes=[pltpu.VMEM((total, 2 * KV_WIDTH), BF16),
                        pltpu.VMEM((wb + PAGE_SIZE, 2 * KV_WIDTH), BF16),
                        pltpu.VMEM((N_KV_HEADS, tq, ns_pad), BF16),
                        pltpu.VMEM((N_HEADS * tq, HEAD_DIM), F32)] + _flash_scratch(tq),
    )
    return pl.pallas_call(
        functools.partial(_nsa_sample_kernel, pg=pg, kc=kc, n_new=n_new),
        out_shape=jax.ShapeDtypeStruct((DB, tq, D), BF16),
        grid_spec=grid_spec,
        compiler_params=_cparams("parallel", "arbitrary"),
        name="nsa_sample_attend",
    )(page_table, q, gates, kvc, sel_new, win_buf, win_new, *([cache_kv] * pg))


HALO = 8
G_IG, G_LF, G_CUM = 0, MLSTM_HEADS, 2 * MLSTM_HEADS
MASKED_GATE = -1e30


def _split3(x):
    hi = x.astype(BF16)
    r1 = x - hi.astype(F32)
    mid = r1.astype(BF16)
    lo = (r1 - mid.astype(F32)).astype(BF16)
    return hi, mid, lo


def _mlstm_pre_kernel(xm_ref, halo_ref, cw_ref, cb_ref, wq_ref, wk_ref, wv_ref, wvt_ref, wg_ref, bg_ref,
                      q_ref, k_ref, v_ref, vt_ref, xc_ref, g_ref, gt_ref, xp_ref, *, chunk, n_valid):
    tm = xm_ref.shape[1]
    hd = MLSTM_HEAD_DIM
    xp_ref[0:HALO, :] = halo_ref[0, 0]
    xp_ref[HALO:HALO + tm, :] = xm_ref[0]
    y = cb_ref[...]
    for j in range(CONV_WIDTH):
        off = HALO - (CONV_WIDTH - 1) + j
        y = y + cw_ref[j:j + 1, :] * xp_ref[off:off + tm, :]
    xc = y * jax.nn.sigmoid(y)
    xc_ref[0] = xc
    xcb = xc.astype(BF16)
    xmb = xm_ref[0].astype(BF16)
    g = bg_ref[...]
    for h in range(MLSTM_HEADS):
        cols = slice(h * hd, (h + 1) * hd)
        q = jnp.dot(xcb[:, cols], wq_ref[h], preferred_element_type=F32).astype(BF16)
        k = (jnp.dot(xcb[:, cols], wk_ref[h], preferred_element_type=F32) * hd ** -0.5).astype(BF16)
        v = jnp.dot(xmb[:, cols], wv_ref[h], preferred_element_type=F32).astype(BF16)
        q_ref[0, :, cols] = q
        k_ref[0, :, cols] = k
        v_ref[0, :, cols] = v
        vt_ref[0, cols, :] = lax.dot_general(wvt_ref[h], xmb[:, cols], _NT, preferred_element_type=F32).astype(BF16)
        for i, a in enumerate((q, k, v)):
            g = g + jnp.dot(a, wg_ref[i * MLSTM_INNER + h * hd:i * MLSTM_INNER + (h + 1) * hd, :],
                            preferred_element_type=F32)
    col = lax.broadcasted_iota(jnp.int32, (tm, LANES), 1)
    row = pl.program_id(1) * tm + lax.broadcasted_iota(jnp.int32, (tm, LANES), 0)
    log_f = jnp.minimum(g, 0.0) - jnp.log1p(jnp.exp(-jnp.abs(g)))
    g = jnp.where(col < G_LF, g, log_f)
    live = row < n_valid
    g = jnp.where(live, g, jnp.where(col < G_LF, MASKED_GATE, 0.0))
    tri = (lax.broadcasted_iota(jnp.int32, (chunk, chunk), 0) >= lax.broadcasted_iota(jnp.int32, (chunk, chunk), 1))
    tri = jnp.where(tri, 1.0, 0.0).astype(BF16)
    for c in range(tm // chunk):
        rows = slice(c * chunk, (c + 1) * chunk)
        cum = None
        for piece in _split3(g[rows]):
            part = jnp.dot(tri, piece, preferred_element_type=F32)
            cum = part if cum is None else cum + part
        out = jnp.where(col[rows] < G_CUM, g[rows], pltpu.roll(cum, G_CUM - G_LF, 1))
        g_ref[0, rows, :] = out
        gt_ref[0, :, rows] = out.T[0:gt_ref.shape[1], :]


def mlstm_pre(xm, halo, conv_w, conv_b, wq, wk, wv, wvt, wg, bg, tm, chunk, n_valid):
    NB, S, W = xm.shape
    hd = MLSTM_HEAD_DIM
    const = lambda a: pl.BlockSpec(a.shape, lambda b, i: (0,) * a.ndim)
    row_blk = lambda w: pl.BlockSpec((1, tm, w), lambda b, i: (b, i, 0))
    sds = jax.ShapeDtypeStruct
    return pl.pallas_call(
        functools.partial(_mlstm_pre_kernel, chunk=chunk, n_valid=n_valid),
        out_shape=[sds((NB, S, W), BF16)] * 3 + [sds((NB, W, S), BF16), sds((NB, S, W), F32),
                                                 sds((NB, S, LANES), F32), sds((NB, 2 * HALO, S), F32)],
        grid=(NB, S // tm),
        in_specs=[row_blk(W), pl.BlockSpec((1, 1, HALO, W), lambda b, i: (b, i, 0, 0)),
                  const(conv_w), const(conv_b), const(wq), const(wk), const(wv), const(wvt), const(wg), const(bg)],
        out_specs=[row_blk(W)] * 3 + [pl.BlockSpec((1, W, tm), lambda b, i: (b, 0, i)), row_blk(W), row_blk(LANES),
                                      pl.BlockSpec((1, 2 * HALO, tm), lambda b, i: (b, 0, i))],
        scratch_shapes=[pltpu.VMEM((HALO + tm, W), F32)],
        compiler_params=_cparams("parallel", "parallel"),
        name="mlstm_pre",
    )(xm, halo, conv_w, conv_b, wq, wk, wv, wvt, wg, bg)


def _mlstm_scan_kernel(q_ref, k_ref, v_ref, vt_ref, g_ref, gt_ref, xc_ref, z_ref, ng_ref, sk_ref, c0_ref, n0_ref, m0_ref,
                       o_ref, c_out, n_out, m_out, c_ref, n_ref, m_ref):
    L = q_ref.shape[1]
    h = pl.program_id(1)
    ci = pl.program_id(2)

    @pl.when(ci == 0)
    def _():
        c_ref[...] = c0_ref[0, 0]
        n_ref[...] = n0_ref[0, 0]
        m_ref[...] = m0_ref[0, 0]

    q, k, v = q_ref[0], k_ref[0], v_ref[0]
    col = lax.broadcasted_iota(jnp.int32, (L, LANES), 1)
    b_col = jnp.sum(jnp.where(col == G_CUM + h, g_ref[0], 0.0), axis=1, keepdims=True)
    i_row = gt_ref[0, pl.ds(G_IG + h, 1), :]
    b_row = gt_ref[0, pl.ds(G_CUM + h, 1), :]
    m = m_ref[0:1, 0:1]
    inter = b_col + m
    dmat = b_col - b_row + i_row
    tril = lax.broadcasted_iota(jnp.int32, (L, L), 0) >= lax.broadcasted_iota(jnp.int32, (L, L), 1)
    dmat = jnp.where(tril, dmat, NEG_INF)
    m_loc = jnp.maximum(inter, jnp.max(dmat, axis=1, keepdims=True))
    a = lax.dot_general(q, k, _NT, preferred_element_type=F32) * jnp.exp(dmat - m_loc)
    w_inter = jnp.exp(inter - m_loc)
    c_prev = c_ref[...]
    n_prev = n_ref[...]
    num = (jnp.dot(a.astype(BF16), v, preferred_element_type=F32)
           + w_inter * lax.dot_general(q, c_prev.astype(BF16), _NT, preferred_element_type=F32))
    qn = lax.dot_general(q, n_prev.astype(BF16), _NT, preferred_element_type=F32)[:, 0:1]
    den = jnp.sum(a, axis=1, keepdims=True) + w_inter * qn
    hc = num / jnp.maximum(jnp.abs(den), jnp.exp(-m_loc))
    b_end = b_row[:, L - 1:L]
    lg = b_end - b_row + i_row
    m_new = jnp.maximum(b_end + m, jnp.max(lg, axis=1, keepdims=True))
    wg = jnp.exp(lg - m_new)
    decay = jnp.exp(b_end + m - m_new)
    c_ref[...] = decay * c_prev + jnp.dot((vt_ref[0] * wg).astype(BF16), k, preferred_element_type=F32)
    n_ref[...] = decay * n_prev + jnp.dot(jnp.broadcast_to(wg, (n_ref.shape[0], L)).astype(BF16), k,
                                          preferred_element_type=F32)
    m_ref[...] = jnp.broadcast_to(m_new, m_ref.shape)
    mu = jnp.mean(hc, axis=1, keepdims=True)
    hz = hc - mu
    var = jnp.mean(hz * hz, axis=1, keepdims=True)
    hn = hz * lax.rsqrt(var + LN_EPS) * ng_ref[...]
    z = z_ref[0]
    o_ref[0] = ((hn + sk_ref[...] * xc_ref[0]) * (z * jax.nn.sigmoid(z))).astype(o_ref.dtype)

    @pl.when(ci == pl.num_programs(2) - 1)
    def _():
        c_out[0, 0] = c_ref[...]
        n_out[0, 0] = n_ref[...]
        m_out[0, 0] = m_ref[...]


def mlstm_scan(q, k, v, vt, g, gt, xc, z, norm_g, skip, c0, n0, m0, chunk):
    NB, S, W = q.shape
    hd = MLSTM_HEAD_DIM
    H = W // hd
    seq = lambda: pl.BlockSpec((1, chunk, hd), lambda b, h, c: (b, c, h))
    vec = pl.BlockSpec((1, hd), lambda b, h, c: (0, h))
    st = lambda r, w: pl.BlockSpec((1, 1, r, w), lambda b, h, c: (b, h, 0, 0))
    sds = jax.ShapeDtypeStruct
    return pl.pallas_call(
        _mlstm_scan_kernel,
        out_shape=[sds((NB, S, W), BF16), sds((NB, H, hd, hd), F32), sds((NB, H, HALO, hd), F32), sds((NB, H, HALO, LANES), F32)],
        grid=(NB, H, S // chunk),
        in_specs=[seq(), seq(), seq(), pl.BlockSpec((1, hd, chunk), lambda b, h, c: (b, h, c)),
                  pl.BlockSpec((1, chunk, LANES), lambda b, h, c: (b, c, 0)),
                  pl.BlockSpec((1, 2 * HALO, chunk), lambda b, h, c: (b, 0, c)),
                  seq(), seq(), vec, vec, st(hd, hd), st(HALO, hd), st(HALO, LANES)],
        out_specs=[seq(), st(hd, hd), st(HALO, hd), st(HALO, LANES)],
        scratch_shapes=[pltpu.VMEM((hd, hd), F32), pltpu.VMEM((HALO, hd), F32), pltpu.VMEM((HALO, LANES), F32)],
        compiler_params=_cparams("parallel", "parallel", "arbitrary"),
        name="mlstm_scan",
    )(q, k, v, vt, g, gt, xc, z, norm_g.reshape(1, W), skip.reshape(1, W), c0, n0, m0)


SAMPLE_Q_ROWS = 16
TM_PROMPT = 512
TM_PEER_ROUTE = 256
TM_PEER = 512
TM_MLSTM = 256
CHUNK_MLSTM = 256


def _pad_rows(a, rows):
    return jnp.pad(a, ((0, 0), (0, rows - a.shape[1]), (0, 0)))


def _pad_cols(a, cols):
    return jnp.pad(a, ((0, 0),) * (a.ndim - 1) + ((0, cols - a.shape[-1]),))


def _dsa_layer(xp, xs, mp, ms, w_in, w_o, ln_g, ln_b, cache_kv, cache_ki, layer, page_table, DB, T):
    B, S, D = xp.shape
    wb = w_in.astype(BF16)
    o1, o2, o3, o4 = D, D + KV_WIDTH, D + 2 * KV_WIDTH, D + 2 * KV_WIDTH + IDX_HEADS * IDX_DIM
    weights = [wb[:, :o1], wb[:, o1:o3], wb[:, o1:o2], wb[:, o2:o3], wb[:, o3:o4], wb[:, o4:o4 + IDX_DIM],
               _pad_cols(wb[:, o4 + IDX_DIM:], LANES)]
    outs = [(0, BF16), (1, F32), (2, BF16), (3, BF16), (4, BF16), (5, F32), (5, BF16), (6, F32)]
    w_ob = w_o.astype(BF16)
    q, kv32, kb, _, qi, ki32, kib, wi, vt = proj(xp, mp(0, 0), mp(0, 1), weights, outs, TM_PROMPT,
                                                 weights_t=[wb[:, o2:o3].T])
    o = dsa_prompt_attend(q, qi, wi, kb, vt, kib)
    xp = outproj_postnorm(o, w_ob, xp, mp(0, 2), ln_g, ln_b, TM_PROMPT)
    kv_p = kv32.reshape(B, S, 2, N_KV_HEADS, HEAD_DIM)
    n = DB * T
    q, kv32s, kb, vb, qi, ki32s, kib, wi = proj(xs, ms(0, 0), ms(0, 1), weights, outs, n)
    per_b = lambda a: a.reshape(DB, T, a.shape[-1])
    kv_new = _pad_rows(jnp.concatenate([per_b(kb), per_b(vb)], axis=-1), PAGE_SIZE)
    o = dsa_sample_attend(_pad_rows(per_b(q), SAMPLE_Q_ROWS), _pad_rows(per_b(qi), SAMPLE_Q_ROWS),
                          _pad_rows(per_b(wi), SAMPLE_Q_ROWS), kv_new, _pad_rows(per_b(kib), PAGE_SIZE),
                          cache_kv.reshape(cache_kv.shape[:3] + (2 * KV_WIDTH,)), cache_ki, layer, page_table, T)
    xs = outproj_postnorm(o[:, :T].reshape(1, n, D), w_ob, xs, ms(0, 2), ln_g, ln_b, n)
    kv_s = kv32s.reshape(DB, T, 2, N_KV_HEADS, HEAD_DIM)
    return xp, xs, kv_p, kv_s, ki32, ki32s.reshape(DB, T, IDX_DIM)


def _nsa_layer(xp, xs, mp, ms, w_in, cmp_pe, cmp_w1, cmp_w2, w_o, ln_g, ln_b, cache_kv, win_state, layer,
               page_table, DB, T):
    B, S, D = xp.shape
    wb = w_in.astype(BF16)
    c1, c2, c3 = D + 4 * KV_WIDTH, D + 6 * KV_WIDTH, D + 2 * KV_WIDTH
    sel_k, sel_v = wb[:, c3:c3 + KV_WIDTH], wb[:, c3 + KV_WIDTH:c1]
    win_k, win_v = wb[:, c1:c1 + KV_WIDTH], wb[:, c1 + KV_WIDTH:c2]
    weights = [wb[:, :D], wb[:, D:c1], wb[:, c1:c2], wb[:, c3:c2], _pad_cols(wb[:, c2:], LANES),
               jnp.concatenate([sel_k, win_k], axis=1)]
    outs = [(0, BF16), (1, F32), (2, F32), (3, BF16), (4, F32)]
    w_ob = w_o.astype(BF16)
    cmp_w = nsa_compress_weights(cmp_pe, cmp_w1, cmp_w2)
    q, cs32, win32, kk, gates, vt = proj(xp, mp(0, 0), mp(0, 1), weights,
                                         [(0, BF16), (1, F32), (2, F32), (5, BF16), (4, F32)], TM_PROMPT,
                                         weights_t=[jnp.concatenate([sel_v, win_v], axis=1).T])
    pages = S // PAGE_SIZE
    ident = jnp.arange(B * pages, dtype=jnp.int32).reshape(B, pages)
    kvc = nsa_compress(cs32.reshape(1, B * pages, PAGE_SIZE, 4 * KV_WIDTH), 0, ident, *cmp_w)
    o = nsa_prompt_attend(q, gates, kvc, kk, vt)
    xp = outproj_postnorm(o, w_ob, xp, mp(0, 2), ln_g, ln_b, TM_PROMPT)
    kv_p = cs32.reshape(B, S, 2, 2, N_KV_HEADS, HEAD_DIM)
    keep = min(WINDOW, S)
    win_p = win32[:, S - keep:].reshape(B, keep, 2, N_KV_HEADS, HEAD_DIM)
    n = DB * T
    q, cs32s, win32s, selwin, gates = proj(xs, ms(0, 0), ms(0, 1), weights, outs, n)
    per_b = lambda a: a.reshape(DB, T, a.shape[-1])
    kvc = nsa_compress(cache_kv.reshape(cache_kv.shape[:3] + (4 * KV_WIDTH,)), layer, page_table, *cmp_w)
    selwin = per_b(selwin)
    wbuf = win_state[layer]
    o = nsa_sample_attend(_pad_rows(per_b(q), SAMPLE_Q_ROWS), _pad_rows(per_b(gates), SAMPLE_Q_ROWS), kvc,
                          _pad_rows(selwin[..., :2 * KV_WIDTH], PAGE_SIZE), wbuf.reshape(DB, wbuf.shape[1], 2 * KV_WIDTH),
                          _pad_rows(selwin[..., 2 * KV_WIDTH:], PAGE_SIZE),
                          cache_kv.reshape(cache_kv.shape[:3] + (4 * KV_WIDTH,)), layer, page_table, T)
    xs = outproj_postnorm(o[:, :T].reshape(1, n, D), w_ob, xs, ms(0, 2), ln_g, ln_b, n)
    kv_s = cs32s.reshape(DB, T, 2, 2, N_KV_HEADS, HEAD_DIM)
    win_s = jnp.concatenate([wbuf, win32s.reshape(DB, T, 2, N_KV_HEADS, HEAD_DIM)], axis=1)[:, T:]
    return xp, xs, kv_p, kv_s, win_p, win_s


def _mlstm_layer(xp, xs, mp, ms, w_in, conv_w, conv_b, w_qkv, w_gate, b_gate, norm_g, skip, w_o, ln_g, ln_b,
                 conv_state, c_state, n_state, m_state, DB, T):
    B, S, D = xp.shape
    W, H, hd = MLSTM_INNER, MLSTM_HEADS, MLSTM_HEAD_DIM
    wb = w_in.astype(BF16)
    weights = [wb[:, :W], wb[:, W:]]
    outs = [(0, F32), (1, F32)]
    wq, wk, wv = (w_qkv[i].astype(BF16) for i in range(3))
    pre_w = (conv_w, conv_b.reshape(1, W), wq, wk, wv, jnp.swapaxes(wv, 1, 2), _pad_cols(w_gate, LANES).astype(BF16),
             _pad_cols(b_gate.reshape(1, -1), LANES))
    w_ob = w_o.astype(BF16)
    keep = CONV_WIDTH - 1
    rep = lambda a, r: jnp.broadcast_to(a[..., None, :], a.shape[:-1] + (r, a.shape[-1]))
    xm, z = proj(xp, mp(0, 0), mp(0, 1), weights, outs, TM_PROMPT)
    tm = min(TM_MLSTM, S)
    tiles = xm.reshape(B, S // tm, tm, W)
    halo = jnp.concatenate([jnp.zeros((B, 1, HALO, W), F32), tiles[:, :-1, tm - HALO:]], axis=1)
    chunk = min(CHUNK_MLSTM, S)
    q, k, v, vt, xc, g, gt = mlstm_pre(xm, halo, *pre_w, tm=tm, chunk=chunk, n_valid=S)
    zeros = lambda *s: jnp.zeros(s, F32)
    o, c_p, n_p, m_p = mlstm_scan(q, k, v, vt, g, gt, xc, z, norm_g, skip, zeros(B, H, hd, hd), zeros(B, H, HALO, hd),
                                  zeros(B, H, HALO, LANES), chunk)
    xp = outproj_postnorm(o, w_ob, xp, mp(0, 2), ln_g, ln_b, TM_PROMPT)
    conv_p = jnp.concatenate([zeros(B, keep, W), xm], axis=1)[:, -keep:]
    n = DB * T
    xm_s, z_s = proj(xs, ms(0, 0), ms(0, 1), weights, outs, n)
    xm_s = xm_s.reshape(DB, T, W)
    rows = LANES
    halo = jnp.concatenate([zeros(DB, HALO - keep, W), conv_state], axis=1)[:, None]
    q, k, v, vt, xc, g, gt = mlstm_pre(_pad_rows(xm_s, rows), halo, *pre_w, tm=rows, chunk=rows, n_valid=T)
    m0 = jnp.broadcast_to(m_state[..., None, None], (DB, H, HALO, LANES))
    o, c_s, n_s, m_s = mlstm_scan(q, k, v, vt, g, gt, xc, _pad_rows(z_s.reshape(DB, T, W), rows), norm_g, skip,
                                  c_state, rep(n_state, HALO), m0, rows)
    xs = outproj_postnorm(o[:, :T].reshape(1, n, W), w_ob, xs, ms(0, 2), ln_g, ln_b, n)
    conv_s = jnp.concatenate([conv_state, xm_s], axis=1)[:, -keep:]
    return (xp, xs, conv_p, conv_s, c_p, c_s, n_p[:, :, 0], n_s[:, :, 0], m_p[:, :, 0, 0], m_s[:, :, 0, 0])


def kernel(x_prompt, x_sample, cache_a_kv, cache_a_kidx, cache_b_kv, state_b_win, state_c_conv, state_c_C, state_c_n,
           state_c_m, page_table, c_prompt, c_sample, a_w_in, a_w_o, b_w_in, b_cmp_pe, b_cmp_w1, b_cmp_w2, b_w_o,
           c_w_in, c_conv_w, c_conv_b, c_w_qkv, c_w_gate, c_b_gate, c_norm_g, c_skip, c_w_o,
           ada_w, ada_b, ln_g, ln_b, peer_w_q, peer_sub_keys, peer_u, peer_v):
    B, S, D = x_prompt.shape
    DB, T, _ = x_sample.shape
    n = DB * T
    n_cond = B + DB
    cond = _pad_rows(jnp.concatenate([c_prompt, c_sample], axis=0)[None], -(-n_cond // 8) * 8)[0]
    mods = adaln_all(cond, ada_w, ada_b).reshape(DEPTH, cond.shape[0], 2, 3, D)
    xp, xs = x_prompt, x_sample.reshape(1, n, D)
    outs = {name: [] for name in ("a_kv_p", "a_kv_s", "a_ki_p", "a_ki_s", "b_kv_p", "b_kv_s", "b_win_p", "b_win_s",
                                  "conv_p", "conv_s", "C_p", "C_s", "n_p", "n_s", "m_p", "m_s")}
    for i in range(DEPTH):
        kind, j = i % N_MIXERS, i // N_MIXERS
        mod_p = mods[i, :B]
        mod_s = jnp.repeat(mods[i, B:n_cond], T, axis=0)
        mp = lambda s, r, mod_p=mod_p: mod_p[:, s, r][:, None, :]
        ms = lambda s, r, mod_s=mod_s: mod_s[:, s, r][None]
        if kind == 0:
            xp, xs, kvp, kvs, kip, kis = _dsa_layer(xp, xs, mp, ms, a_w_in[j], a_w_o[j], ln_g[i, 0], ln_b[i, 0],
                                                    cache_a_kv, cache_a_kidx, j, page_table, DB, T)
            for name, val in zip(("a_kv_p", "a_kv_s", "a_ki_p", "a_ki_s"), (kvp, kvs, kip, kis)):
                outs[name].append(val)
        elif kind == 1:
            xp, xs, kvp, kvs, wp, ws = _nsa_layer(xp, xs, mp, ms, b_w_in[j], b_cmp_pe[j], b_cmp_w1[j], b_cmp_w2[j],
                                                  b_w_o[j], ln_g[i, 0], ln_b[i, 0], cache_b_kv, state_b_win, j,
                                                  page_table, DB, T)
            for name, val in zip(("b_kv_p", "b_kv_s", "b_win_p", "b_win_s"), (kvp, kvs, wp, ws)):
                outs[name].append(val)
        else:
            res = _mlstm_layer(xp, xs, mp, ms, c_w_in[j], c_conv_w[j], c_conv_b[j], c_w_qkv[j], c_w_gate[j], c_b_gate[j],
                               c_norm_g[j], c_skip[j], c_w_o[j], ln_g[i, 0], ln_b[i, 0],
                               state_c_conv[j], state_c_C[j], state_c_n[j], state_c_m[j], DB, T)
            xp, xs = res[0], res[1]
            for name, val in zip(("conv_p", "conv_s", "C_p", "C_s", "n_p", "n_s", "m_p", "m_s"), res[2:]):
                outs[name].append(val)
        peer_w = (peer_w_q[i].astype(BF16), peer_sub_keys[i].astype(BF16), peer_u[i].astype(BF16).T,
                  peer_v[i].astype(BF16), ln_g[i, 1], ln_b[i, 1])
        xp = peer_layer(xp, mp(1, 0), mp(1, 1), mp(1, 2), *peer_w, tm_route=min(TM_PEER_ROUTE, S), tm=min(TM_PEER, S))
        xs = peer_layer(xs, ms(1, 0), ms(1, 1), ms(1, 2), *peer_w, tm_route=n, tm=n)
    return (xp, xs.reshape(DB, T, D)) + tuple(jnp.stack(outs[name]) for name in outs)
```

```python
import functools
import math

import jax
import jax.numpy as jnp
from jax import lax
from jax.experimental import pallas as pl
from jax.experimental.pallas import tpu as pltpu

F32 = jnp.float32
BF16 = jnp.bfloat16

D_MODEL = 1024
DEPTH = 4
PAGE_SIZE = 128
N_MIXERS = 3

N_HEADS = 16
HEAD_DIM = D_MODEL // N_HEADS
N_KV_HEADS = 4
GROUP = N_HEADS // N_KV_HEADS
KV_WIDTH = N_KV_HEADS * HEAD_DIM
Q_BLOCK = 128
ATTN_SCALE = HEAD_DIM ** -0.5

IDX_HEADS = 8
IDX_DIM = 64
IDX_SCALE = (IDX_HEADS * IDX_DIM) ** -0.5
DSA_TOPK = 256

CMP_LEN = 32
CMP_HIDDEN = 2 * HEAD_DIM
SEL_BLOCK = 64
N_SEL = 16
WINDOW = 512

MLSTM_INNER = 2 * D_MODEL
MLSTM_HEADS = 4
MLSTM_HEAD_DIM = MLSTM_INNER // MLSTM_HEADS
CONV_WIDTH = 4
MLSTM_CHUNK = 64

PEER_HEADS = 8
PEER_KEYS = 128
N_EXPERTS = PEER_KEYS * PEER_KEYS
PEER_KEY_DIM = 256
PEER_TOPK = 16

ALPHA = (2 * DEPTH) ** 0.25
LN_EPS = 1e-5

LANES = 128
VMEM_LIMIT = 56 * 1024 * 1024

_NT = (((1,), (1,)), ((), ()))


def _cparams(*sem):
    return pltpu.CompilerParams(dimension_semantics=sem, vmem_limit_bytes=VMEM_LIMIT)


def _gelu(x):
    return 0.5 * x * (1.0 + jnp.tanh(math.sqrt(2.0 / math.pi) * (x + 0.044715 * (x * x * x))))


def _post_norm_math(x, y, gate, g, b):
    z = ALPHA * x + gate * y
    mu = jnp.mean(z, axis=-1, keepdims=True)
    zc = z - mu
    var = jnp.mean(zc * zc, axis=-1, keepdims=True)
    return zc * lax.rsqrt(var + LN_EPS) * g + b


def _adaln_kernel(c_ref, w_ref, b_ref, o_ref):
    c = c_ref[...]
    s = (c * jax.nn.sigmoid(c)).astype(BF16)
    o_ref[0] = jnp.dot(s, w_ref[0].astype(BF16), preferred_element_type=F32) + b_ref[0]


def adaln_all(c, ada_w, ada_b):
    M = c.shape[0]
    n = ada_w.shape[-1] // D_MODEL
    return pl.pallas_call(
        _adaln_kernel,
        out_shape=jax.ShapeDtypeStruct((DEPTH, M, n * D_MODEL), F32),
        grid=(DEPTH, n),
        in_specs=[pl.BlockSpec((M, D_MODEL), lambda i, j: (0, 0)),
                  pl.BlockSpec((1, D_MODEL, D_MODEL), lambda i, j: (i, 0, j)),
                  pl.BlockSpec((1, 1, D_MODEL), lambda i, j: (i, 0, j))],
        out_specs=pl.BlockSpec((1, M, D_MODEL), lambda i, j: (i, 0, j)),
        compiler_params=_cparams("arbitrary", "arbitrary"),
        name="adaln",
    )(c, ada_w, ada_b.reshape(DEPTH, 1, -1))


def _proj_kernel(x_ref, sh_ref, sc_ref, *refs, n_w, n_t, out_map):
    w_refs, wt_refs, o_refs = refs[:n_w], refs[n_w:n_w + n_t], refs[n_w + n_t:]
    h = (x_ref[0] * (1.0 + sc_ref[0]) + sh_ref[0]).astype(BF16)
    done = {}
    for o_ref, wi in zip(o_refs, out_map):
        if wi not in done:
            done[wi] = jnp.dot(h, w_refs[wi][...], preferred_element_type=F32)
        o_ref[0] = done[wi].astype(o_ref.dtype)
    for o_ref, wt_ref in zip(o_refs[len(out_map):], wt_refs):
        o_ref[0] = lax.dot_general(wt_ref[...], h, _NT, preferred_element_type=F32).astype(o_ref.dtype)


def proj(x, shift, scale, weights, outs, tm, weights_t=()):
    nb, S, D = x.shape
    rows = shift.shape[1]
    mblk = (1, tm, D) if rows == S else (1, 1, D)
    mmap = (lambda b, i: (b, i, 0)) if rows == S else (lambda b, i: (b, 0, 0))
    in_specs = [pl.BlockSpec((1, tm, D), lambda b, i: (b, i, 0)),
                pl.BlockSpec(mblk, mmap), pl.BlockSpec(mblk, mmap)]
    in_specs += [pl.BlockSpec(w.shape, lambda b, i: (0, 0)) for w in (*weights, *weights_t)]
    out_shape = [jax.ShapeDtypeStruct((nb, S, weights[wi].shape[1]), dt) for wi, dt in outs]
    out_specs = [pl.BlockSpec((1, tm, weights[wi].shape[1]), lambda b, i: (b, i, 0)) for wi, _ in outs]
    out_shape += [jax.ShapeDtypeStruct((nb, w.shape[0], S), BF16) for w in weights_t]
    out_specs += [pl.BlockSpec((1, w.shape[0], tm), lambda b, i: (b, 0, i)) for w in weights_t]
    return pl.pallas_call(
        functools.partial(_proj_kernel, n_w=len(weights), n_t=len(weights_t), out_map=tuple(wi for wi, _ in outs)),
        out_shape=out_shape, grid=(nb, S // tm), in_specs=in_specs, out_specs=out_specs,
        compiler_params=_cparams("parallel", "parallel"),
        name="proj",
    )(x, shift, scale, *weights, *weights_t)


def _outproj_kernel(o_ref, w_ref, x_ref, gate_ref, g_ref, b_ref, y_ref):
    y = jnp.dot(o_ref[0], w_ref[...], preferred_element_type=F32)
    y_ref[0] = _post_norm_math(x_ref[0], y, gate_ref[0], g_ref[...], b_ref[...])


def outproj_postnorm(o, w, x, gate, g, b, tm):
    nb, S, K = o.shape
    D = x.shape[-1]
    rows = gate.shape[1]
    mblk = (1, tm, D) if rows == S else (1, 1, D)
    mmap = (lambda bi, i: (bi, i, 0)) if rows == S else (lambda bi, i: (bi, 0, 0))
    return pl.pallas_call(
        _outproj_kernel,
        out_shape=jax.ShapeDtypeStruct(x.shape, F32),
        grid=(nb, S // tm),
        in_specs=[pl.BlockSpec((1, tm, K), lambda bi, i: (bi, i, 0)),
                  pl.BlockSpec((K, D), lambda bi, i: (0, 0)),
                  pl.BlockSpec((1, tm, D), lambda bi, i: (bi, i, 0)),
                  pl.BlockSpec(mblk, mmap),
                  pl.BlockSpec((1, D), lambda bi, i: (0, 0)),
                  pl.BlockSpec((1, D), lambda bi, i: (0, 0))],
        out_specs=pl.BlockSpec((1, tm, D), lambda bi, i: (bi, i, 0)),
        compiler_params=_cparams("parallel", "parallel"),
        name="outproj_postnorm",
    )(o, w, x, gate, g.reshape(1, D), b.reshape(1, D))


def _top_rows(x, iota, n):
    vals, idxs = [], []
    for _ in range(n):
        m = jnp.max(x, axis=0, keepdims=True)
        idx = jnp.min(jnp.where(x == m, iota, jnp.inf), axis=0, keepdims=True)
        vals.append(m)
        idxs.append(idx)
        x = jnp.where(iota == idx, -jnp.inf, x)
    return vals, idxs


def _peer_route_kernel(x_ref, sh_ref, sc_ref, wq_ref, keys_ref, i1_ref, i2_ref, g_ref):
    tm = x_ref.shape[1]
    h = (x_ref[0] * (1.0 + sc_ref[0]) + sh_ref[0]).astype(BF16)
    q = jnp.dot(h, wq_ref[...], preferred_element_type=F32).astype(BF16)
    iota_k = lax.broadcasted_iota(jnp.int32, (PEER_KEYS, tm), 0).astype(F32)
    half = PEER_KEY_DIM // 2
    n = PEER_TOPK
    row16 = lax.broadcasted_iota(jnp.int32, (n, tm), 0).astype(F32)
    row8 = lax.broadcasted_iota(jnp.int32, (8, tm), 0).astype(F32)
    pieces = [(0, 1, 16, 0, 15), (1, 1, 8, 0, 7), (2, 1, 8, 0, 4), (3, 1, 8, 0, 3),
              (0, 2, 16, 4, 15), (1, 2, 8, 4, 7), (2, 2, 8, 4, 4)]
    ids = []
    for fixed, which, rows, lo, hi in pieces:
        r = row16 if rows == 16 else row8
        pair = fixed * n + r if which == 1 else r * n + fixed
        ids.append(jnp.where(r < lo, float(n * n), jnp.where(r > hi, float(n * n), pair)))
    cand_id = jnp.concatenate(ids, axis=0)
    cand_ok = cand_id < float(n * n)
    i1_rows, i2_rows, g_rows = [], [], []
    for hd in range(PEER_HEADS):
        tops = []
        for c in range(2):
            col = (hd * 2 + c) * half
            s_t = lax.dot_general(keys_ref[c], q[:, col:col + half], _NT, preferred_element_type=F32)
            tops.append(_top_rows(s_t, iota_k, n))
        (v1, id1), (v2, id2) = tops
        stacked = {(w, rows): jnp.concatenate((v1, v2)[w - 1][:rows], axis=0) for w in (1, 2) for rows in (8, n)}
        sums = [(v1[fixed] + stacked[2, rows]) if which == 1 else (stacked[1, rows] + v2[fixed])
                for fixed, which, rows, _, _ in pieces]
        cand = jnp.where(cand_ok, jnp.concatenate(sums, axis=0), NEG_INF)
        cvals, cidx = _top_rows(cand, cand_id, n)
        cv = jnp.concatenate(cvals, axis=0)
        ci = jnp.concatenate(cidx, axis=0)
        r1 = jnp.floor(ci * (1.0 / PEER_TOPK))
        r2 = ci - r1 * PEER_TOPK
        i1 = jnp.zeros_like(ci)
        i2 = jnp.zeros_like(ci)
        for r in range(PEER_TOPK):
            i1 = i1 + jnp.where(r1 == float(r), id1[r], 0.0)
            i2 = i2 + jnp.where(r2 == float(r), id2[r], 0.0)
        e = jnp.exp(cv - cvals[0])
        gate = e / jnp.sum(e, axis=0, keepdims=True)
        i1_rows.append(i1)
        i2_rows.append(i2)
        g_rows.append(gate)
    i1_ref[0] = jnp.concatenate(i1_rows, axis=0).T
    i2_ref[0] = jnp.concatenate(i2_rows, axis=0).T
    g_ref[0] = jnp.concatenate(g_rows, axis=0).T


def _mod_specs(rows, S, tm, D):
    if rows == S:
        return pl.BlockSpec((1, tm, D), lambda b, i, *_: (b, i, 0))
    return pl.BlockSpec((1, 1, D), lambda b, i, *_: (b, 0, 0))


def peer_route(x, shift, scale, wq, keys, tm):
    nb, S, D = x.shape
    nsel = PEER_HEADS * PEER_TOPK
    mspec = _mod_specs(shift.shape[1], S, tm, D)
    return pl.pallas_call(
        _peer_route_kernel,
        out_shape=[jax.ShapeDtypeStruct((nb, S, nsel), F32)] * 3,
        grid=(nb, S // tm),
        in_specs=[pl.BlockSpec((1, tm, D), lambda b, i: (b, i, 0)), mspec, mspec,
                  pl.BlockSpec(wq.shape, lambda b, i: (0, 0)),
                  pl.BlockSpec(keys.shape, lambda b, i: (0, 0, 0))],
        out_specs=[pl.BlockSpec((1, tm, nsel), lambda b, i: (b, i, 0))] * 3,
        compiler_params=_cparams("parallel", "parallel"),
        name="peer_route",
    )(x, shift, scale, wq, keys)


PEER_TOKEN_GROUP = 16


def _peer_expert_kernel(x_ref, sh_ref, sc_ref, gt_ref, i1_ref, i2_ref, g_ref, ut_ref, v_ref, lg_ref, lb_ref,
                        y_ref, w3_ref, acc_ref, h_ref, *, ac):
    tm = x_ref.shape[1]
    j = pl.program_id(2)

    @pl.when(j == 0)
    def _():
        h_ref[...] = (x_ref[0] * (1.0 + sc_ref[0]) + sh_ref[0]).astype(BF16)
        acc_ref[...] = jnp.zeros_like(acc_ref)
        iota_s = lax.broadcasted_iota(jnp.int32, (PEER_KEYS, LANES), 0).astype(F32)

        def build(gi, carry):
            t0 = pl.multiple_of(gi * PEER_TOKEN_GROUP, PEER_TOKEN_GROUP)
            i1g = i1_ref[0, pl.ds(t0, PEER_TOKEN_GROUP), :]
            i2g = i2_ref[0, pl.ds(t0, PEER_TOKEN_GROUP), :]
            gg = g_ref[0, pl.ds(t0, PEER_TOKEN_GROUP), :]
            tiles = []
            for t in range(PEER_TOKEN_GROUP):
                g1t = jnp.where(iota_s == i1g[t:t + 1], gg[t:t + 1], 0.0).astype(BF16)
                o2t = jnp.where(iota_s == i2g[t:t + 1], 1.0, 0.0).astype(BF16)
                tiles.append(lax.dot_general(g1t, o2t, _NT, preferred_element_type=F32))
            w3_ref[:, pl.ds(t0, PEER_TOKEN_GROUP), :] = jnp.swapaxes(jnp.stack(tiles, axis=0), 0, 1).astype(BF16)
            return carry

        lax.fori_loop(0, tm // PEER_TOKEN_GROUP, build, 0)

    act = _gelu(jnp.dot(h_ref[...], ut_ref[...], preferred_element_type=F32))
    wc = jnp.concatenate([w3_ref[j * ac + a] for a in range(ac)], axis=1).astype(F32)
    coef = (wc * act).astype(BF16)
    acc_ref[...] += jnp.dot(coef, v_ref[...], preferred_element_type=F32)

    @pl.when(j == pl.num_programs(2) - 1)
    def _():
        y_ref[0] = _post_norm_math(x_ref[0], acc_ref[...], gt_ref[0], lg_ref[...], lb_ref[...])


def peer_experts(x, shift, scale, gate, i1, i2, g, ut, v, ln_g, ln_b, tm, ac):
    nb, S, D = x.shape
    assert tm % PEER_TOKEN_GROUP == 0 and S % tm == 0, (S, tm)
    nsel = PEER_HEADS * PEER_TOPK
    ce = ac * PEER_KEYS
    mspec = _mod_specs(shift.shape[1], S, tm, D)
    pick =pl.BlockSpec((1, tm, nsel), lambda b, i, j: (b, i, 0))
    return pl.pallas_call(
        functools.partial(_peer_expert_kernel, ac=ac),
        out_shape=jax.ShapeDtypeStruct(x.shape, F32),
        grid=(nb, S // tm, PEER_KEYS // ac),
        in_specs=[pl.BlockSpec((1, tm, D), lambda b, i, j: (b, i, 0)), mspec, mspec, mspec, pick, pick, pick,
                  pl.BlockSpec((D, ce), lambda b, i, j: (0, j)),
                  pl.BlockSpec((ce, D), lambda b, i, j: (j, 0)),
                  pl.BlockSpec((1, D), lambda b, i, j: (0, 0)),
                  pl.BlockSpec((1, D), lambda b, i, j: (0, 0))],
        out_specs=pl.BlockSpec((1, tm, D), lambda b, i, j: (b, i, 0)),
        scratch_shapes=[pltpu.VMEM((PEER_KEYS, tm, PEER_KEYS), BF16),
                        pltpu.VMEM((tm, D), F32),
                        pltpu.VMEM((tm, D), BF16)],
        compiler_params=_cparams("parallel", "parallel", "arbitrary"),
        name="peer_experts",
    )(x, shift, scale, gate, i1, i2, g, ut, v, ln_g.reshape(1, D), ln_b.reshape(1, D))


def peer_layer(x, shift, scale, gate, wq, keys, ut, v, ln_g, ln_b, tm_route, tm):
    i1, i2, g = peer_route(x, shift, scale, wq, keys, tm_route)
    return peer_experts(x, shift, scale, gate, i1, i2, g, ut, v, ln_g, ln_b, tm, ac=8)


INT_MIN = -2 ** 31
NEG_INF = float("-inf")


def _to_key(x):
    b = lax.bitcast_convert_type(x, jnp.int32)
    return b ^ ((b >> 31) & 0x7FFFFFFF)


COUNT_ROWS = 64


def _key_tile(key_ref, c0, kc, axis):
    return key_ref[:, pl.ds(c0, kc)] if axis == 1 else key_ref[pl.ds(c0, kc), :]


def _count_keys(key_ref, nck, kc, pred, axis):
    other = key_ref.shape[1 - axis]

    def body(c, acc):
        c0 = pl.multiple_of(c * kc, kc)
        hit = jnp.where(pred(_key_tile(key_ref, c0, kc, axis), c0), 1.0, 0.0)
        if axis == 1:
            part = hit[:, 0:LANES]
            for u in range(1, kc // LANES):
                part = part + hit[:, u * LANES:(u + 1) * LANES]
        else:
            part = jnp.sum(hit.reshape(kc // COUNT_ROWS, COUNT_ROWS, other), axis=0)
        return acc + part

    acc = lax.fori_loop(0, nck, body, jnp.zeros((other, LANES) if axis == 1 else (COUNT_ROWS, other), F32))
    return jnp.sum(acc, axis=axis, keepdims=True)


def _topk_threshold(key_ref, nck, kc, k, idx_bits, axis=1):
    other = key_ref.shape[1 - axis]
    vec = (other, 1) if axis == 1 else (1, other)
    kf = float(k)

    def bit_step(p, t_u):
        cand_u = t_u | jnp.left_shift(jnp.int32(1), 31 - p)
        cand_s = cand_u ^ INT_MIN
        cnt = _count_keys(key_ref, nck, kc, lambda tile, c0: tile >= cand_s, axis)
        return jnp.where(cnt >= kf, cand_u, t_u)

    t_u = lax.fori_loop(0, 32, bit_step, jnp.zeros(vec, jnp.int32))
    thr = jnp.maximum(t_u ^ INT_MIN, INT_MIN + 1)
    n_ge = _count_keys(key_ref, nck, kc, lambda tile, c0: tile >= thr, axis)

    @pl.when(jnp.max(n_ge) > kf)
    def _():
        need = kf - _count_keys(key_ref, nck, kc, lambda tile, c0: tile > thr, axis)
        iota = lax.broadcasted_iota(jnp.int32, (other, kc) if axis == 1 else (kc, other), axis)

        def idx_step(p, j_hi):
            cand = j_hi | jnp.left_shift(jnp.int32(1), idx_bits - 1 - p)
            cnt = _count_keys(key_ref, nck, kc, lambda tile, c0: (tile == thr) & (c0 + iota < cand), axis)
            return jnp.where(cnt <= need, cand, j_hi)

        j_hi = lax.fori_loop(0, idx_bits, idx_step, jnp.zeros(vec, jnp.int32))
        surplus = n_ge > kf

        def lower(c, carry):
            c0 = pl.multiple_of(c * kc, kc)
            tile = _key_tile(key_ref, c0, kc, axis)
            drop = (tile == thr) & (c0 + iota >= j_hi) & surplus
            tile = jnp.where(drop, thr - 1, tile)
            if axis == 1:
                key_ref[:, pl.ds(c0, kc)] = tile
            else:
                key_ref[pl.ds(c0, kc), :] = tile
            return carry

        lax.fori_loop(0, nck, lower, 0)

    return thr


def _stack_heads(q_ref, qs_ref, tq):
    for hd in range(N_HEADS):
        qs_ref[hd * tq:(hd + 1) * tq, :] = (q_ref[0, :, hd * HEAD_DIM:(hd + 1) * HEAD_DIM] * ATTN_SCALE).astype(BF16)


def _flash_init(m_ref, l_ref, acc_ref):
    m_ref[...] = jnp.full(m_ref.shape, NEG_INF, F32)
    l_ref[...] = jnp.zeros(l_ref.shape, F32)
    acc_ref[...] = jnp.zeros(acc_ref.shape, F32)


def _flash_result(rows, l_ref, acc_ref):
    return acc_ref[rows, :] / jnp.maximum(l_ref[rows, :HEAD_DIM], 1e-30)


def _grouped_flash_step(mask_of, kch, vch, qs_ref, m_ref, l_ref, acc_ref, tq):
    rows_h = GROUP * tq
    kc = kch.shape[0]
    parts = []
    for h in range(N_KV_HEADS):
        logits = lax.dot_general(qs_ref[h * rows_h:(h + 1) * rows_h, :], kch[:, h * HEAD_DIM:(h + 1) * HEAD_DIM],
                                 _NT, preferred_element_type=F32)
        parts.append(jnp.where(mask_of(h)[None], logits.reshape(GROUP, tq, kc), NEG_INF).reshape(rows_h, kc))
    s = jnp.concatenate(parts, axis=0)
    m_prev = m_ref[...]
    m_new = jnp.maximum(m_prev, jnp.max(s, axis=1, keepdims=True))
    m_safe = jnp.where(m_new == NEG_INF, 0.0, m_new)
    p = jnp.exp(s - jnp.concatenate([m_safe] * (kc // LANES), axis=1))
    alpha = jnp.exp(m_prev - m_safe)
    l_ref[...] = alpha * l_ref[...] + jnp.sum(p, axis=1, keepdims=True)
    pb = p.astype(BF16)
    pv = [jnp.dot(pb[h * rows_h:(h + 1) * rows_h], vch[:, h * HEAD_DIM:(h + 1) * HEAD_DIM],
                  preferred_element_type=F32) for h in range(N_KV_HEADS)]
    acc_ref[...] = alpha[:, :HEAD_DIM] * acc_ref[...] + jnp.concatenate(pv, axis=0)
    m_ref[...] = m_new


def _index_scores(qi, wi, kic):
    sc = None
    for hh in range(IDX_HEADS):
        s = lax.dot_general(qi[:, hh * IDX_DIM:(hh + 1) * IDX_DIM], kic, _NT, preferred_element_type=F32)
        term = jnp.maximum(s, 0.0) * wi[:, hh:hh + 1]
        sc = term if sc is None else sc + term
    return sc


def _dsa_prompt_kernel(q_ref, qi_ref, wi_ref, k_ref, vt_ref, ki_ref, o_ref, key_ref, *fl, kc, topk, idx_bits):
    qs_ref = fl[0]
    tq = q_ref.shape[1]
    t0 = pl.program_id(1) * tq
    nck = (t0 + tq + kc - 1) // kc
    _stack_heads(q_ref, qs_ref, tq)
    qpos = t0 + lax.broadcasted_iota(jnp.int32, (1, tq), 1)
    kiota = lax.broadcasted_iota(jnp.int32, (kc, 1), 0)
    qi = qi_ref[0]
    wi = wi_ref[0] * IDX_SCALE

    def score(c, carry):
        c0 = pl.multiple_of(c * kc, kc)
        sc = _index_scores(qi, wi, ki_ref[0, pl.ds(c0, kc), :])
        key_ref[pl.ds(c0, kc), :] = jnp.where(c0 + kiota <= qpos, _to_key(sc.T), INT_MIN)
        return carry

    lax.fori_loop(0, nck, score, 0)
    thr = _topk_threshold(key_ref, nck, kc, topk, idx_bits, axis=0)
    _flash_init_t(*fl)

    def attend(c, carry):
        c0 = pl.multiple_of(c * kc, kc)
        bias = jnp.where(key_ref[pl.ds(c0, kc), :] >= thr, 0.0, NEG_INF)
        _flash_step_t(lambda h: bias, k_ref[0, pl.ds(c0, kc), :], vt_ref[0, :, pl.ds(c0, kc)], *fl, tq=tq)
        return carry

    lax.fori_loop(0, nck, attend, 0)
    for h in range(N_KV_HEADS):
        res = _flash_result_t(h, *fl)
        for g in range(GROUP):
            hd = h * GROUP + g
            o_ref[0, :, hd * HEAD_DIM:(hd + 1) * HEAD_DIM] = res[g * tq:(g + 1) * tq].astype(o_ref.dtype)


ONES_ROWS = 16
FLASH_MAX_ROWS = 16


def _flash_scratch_t(tq, kmax):
    cols = GROUP * tq
    return [pltpu.VMEM((N_HEADS * tq, HEAD_DIM), BF16),
            pltpu.VMEM((N_KV_HEADS, 1, cols), F32),
            pltpu.VMEM((N_KV_HEADS, HEAD_DIM + ONES_ROWS, cols), F32),
            pltpu.VMEM((N_KV_HEADS, kmax, cols), F32),
            pltpu.VMEM((N_KV_HEADS, kmax, cols), BF16)]


def _flash_init_t(qs_ref, m_ref, acc_ref, s_ref, p_ref):
    m_ref[...] = jnp.full(m_ref.shape, NEG_INF, F32)
    acc_ref[...] = jnp.zeros(acc_ref.shape, F32)


def _flash_step_t(bias_of, kch, vtch, qs_ref, m_ref, acc_ref, s_ref, p_ref, *, tq):
    cols_h = GROUP * tq
    kc = kch.shape[0]
    ones = jnp.ones((ONES_ROWS, kc), BF16)
    hcols = lambda h: slice(h * HEAD_DIM, (h + 1) * HEAD_DIM)

    def logits(h):
        st = lax.dot_general(kch[:, hcols(h)], qs_ref[h * cols_h:(h + 1) * cols_h, :], _NT,
                             preferred_element_type=F32)
        s_ref[h, 0:kc, :] = st + jnp.concatenate([bias_of(h)] * GROUP, axis=1)
        mx = s_ref[h, 0:FLASH_MAX_ROWS, :]
        for r in range(FLASH_MAX_ROWS, kc, FLASH_MAX_ROWS):
            mx = jnp.maximum(mx, s_ref[h, r:r + FLASH_MAX_ROWS, :])
        return jnp.max(mx, axis=0, keepdims=True)

    m_cur = logits(0)
    for h in range(N_KV_HEADS):
        m_next = logits(h + 1) if h + 1 < N_KV_HEADS else None
        m_prev = m_ref[h]
        m_new = jnp.maximum(m_prev, m_cur)
        m_safe = jnp.where(m_new == NEG_INF, 0.0, m_new)
        p_ref[h, 0:kc, :] = jnp.exp(s_ref[h, 0:kc, :] - m_safe).astype(BF16)
        va = jnp.concatenate([vtch[hcols(h), :], ones], axis=0)
        acc_ref[h] = jnp.exp(m_prev - m_safe) * acc_ref[h] + jnp.dot(va, p_ref[h, 0:kc, :],
                                                                     preferred_element_type=F32)
        m_ref[h] = m_new
        m_cur = m_next


def _flash_result_t(h, qs_ref, m_ref, acc_ref, s_ref, p_ref):
    acc = acc_ref[h]
    return (acc[0:HEAD_DIM] / jnp.maximum(acc[HEAD_DIM:HEAD_DIM + 1], 1e-30)).T


def _flash_scratch(tq):
    return [pltpu.VMEM((N_HEADS * tq, HEAD_DIM), BF16),
            pltpu.VMEM((N_HEADS * tq, LANES), F32),
            pltpu.VMEM((N_HEADS * tq, LANES), F32),
            pltpu.VMEM((N_HEADS * tq, HEAD_DIM), F32)]


def dsa_prompt_attend(q, qi, wi, k, vt, ki, tq=Q_BLOCK, kc=512):
    B, S, D = q.shape
    kc = min(kc, S)
    topk = min(DSA_TOPK, S // 4)
    blk = lambda w: pl.BlockSpec((1, tq, w), lambda b, i: (b, i, 0))
    full = lambda w: pl.BlockSpec((1, S, w), lambda b, i: (b, 0, 0))
    return pl.pallas_call(
        functools.partial(_dsa_prompt_kernel, kc=kc, topk=topk, idx_bits=S.bit_length()),
        out_shape=jax.ShapeDtypeStruct((B, S, D), BF16),
        grid=(B, S // tq),
        in_specs=[blk(D), blk(qi.shape[-1]), blk(wi.shape[-1]), full(KV_WIDTH),
                  pl.BlockSpec((1, KV_WIDTH, S), lambda b, i: (b, 0, 0)), full(IDX_DIM)],
        out_specs=blk(D),
        scratch_shapes=[pltpu.VMEM((S, tq), jnp.int32)] + _flash_scratch_t(tq, kc),
        compiler_params=_cparams("parallel", "arbitrary"),
        name="dsa_prompt_attend",
    )(q, qi, wi, k, vt, ki)


def _dsa_sample_kernel(pt_ref, q_ref, qi_ref, wi_ref, kvn_ref, kin_ref, *refs, pg, kc, topk, idx_bits, n_new):
    ckv_refs, cki_refs = refs[:pg], refs[pg:2 * pg]
    o_ref, key_ref, kv_ref, qs_ref, m_ref, l_ref, acc_ref = refs[2 * pg:]
    tq = q_ref.shape[1]
    j = pl.program_id(1)
    n_steps = pl.num_programs(1)
    past = kv_ref.shape[0] - PAGE_SIZE
    qi = qi_ref[0].astype(BF16)
    wi = wi_ref[0] * IDX_SCALE
    span = pg * PAGE_SIZE
    s0 = pl.multiple_of(j * span, span)
    for k in range(pg):
        kv_ref[pl.ds(s0 + k * PAGE_SIZE, PAGE_SIZE), :] = ckv_refs[k][0, 0].astype(BF16)
    ki_step = jnp.concatenate([r[0, 0] for r in cki_refs], axis=0).astype(BF16)
    key_ref[:, pl.ds(s0, span)] = _to_key(_index_scores(qi, wi, ki_step))

    @pl.when(j == n_steps - 1)
    def _():
        nck = (past + PAGE_SIZE) // kc
        kv_ref[past:past + PAGE_SIZE, :] = kvn_ref[0]
        trow = lax.broadcasted_iota(jnp.int32, (tq, PAGE_SIZE), 0)
        ncol = lax.broadcasted_iota(jnp.int32, (tq, PAGE_SIZE), 1)
        visible = (ncol <= trow) & (ncol < n_new)
        key_ref[:, past:past + PAGE_SIZE] = jnp.where(visible, _to_key(_index_scores(qi, wi, kin_ref[0])), INT_MIN)
        _stack_heads(q_ref, qs_ref, tq)
        thr = _topk_threshold(key_ref, nck, kc, topk, idx_bits)
        _flash_init(m_ref, l_ref, acc_ref)

        def attend(c, carry):
            c0 = pl.multiple_of(c * kc, kc)
            sel = key_ref[:, pl.ds(c0, kc)] >= thr
            _grouped_flash_step(lambda h: sel, kv_ref[pl.ds(c0, kc), 0:KV_WIDTH],
                                kv_ref[pl.ds(c0, kc), KV_WIDTH:2 * KV_WIDTH], qs_ref, m_ref, l_ref, acc_ref, tq)
            return carry

        lax.fori_loop(0, nck, attend, 0)
        for hd in range(N_HEADS):
            o_ref[0, :, hd * HEAD_DIM:(hd + 1) * HEAD_DIM] = _flash_result(
                slice(hd * tq, (hd + 1) * tq), l_ref, acc_ref).astype(o_ref.dtype)


PAGES_PER_STEP = 16


def _pages_per_step(n_pages):
    return max(d for d in range(1, PAGES_PER_STEP + 1) if n_pages % d == 0)


def _page_specs(width, col, layer, pg):
    return [pl.BlockSpec((1, 1, PAGE_SIZE, width), lambda b, j, pt, k=k: (layer, pt[b, j * pg + k], 0, col))
            for k in range(pg)]


def _key_chunk(total):
    n = total // LANES
    return LANES * max(d for d in range(1, 9) if n % d == 0)


def dsa_sample_attend(q, qi, wi, kv_new, ki_new, cache_kv, cache_ki, layer, page_table, n_new):
    DB, tq, D = q.shape
    n_pages = page_table.shape[1]
    past = n_pages * PAGE_SIZE
    total = past + PAGE_SIZE
    kc = _key_chunk(total)
    topk = min(DSA_TOPK, (past + n_new) // 4)
    per_b = lambda r, w: pl.BlockSpec((1, r, w), lambda b, j, pt: (b, 0, 0))
    pg = _pages_per_step(n_pages)
    grid_spec = pltpu.PrefetchScalarGridSpec(
        num_scalar_prefetch=1,
        grid=(DB, n_pages // pg),
        in_specs=[per_b(tq, D), per_b(tq, qi.shape[-1]), per_b(tq, wi.shape[-1]),
                  per_b(PAGE_SIZE, 2 * KV_WIDTH), per_b(PAGE_SIZE, IDX_DIM)]
        + _page_specs(2 * KV_WIDTH, 0, layer, pg) + _page_specs(IDX_DIM, 0, layer, pg),
        out_specs=per_b(tq, D),
        scratch_shapes=[pltpu.VMEM((tq, total), jnp.int32),
                        pltpu.VMEM((total, 2 * KV_WIDTH), BF16)] + _flash_scratch(tq),
    )
    return pl.pallas_call(
        functools.partial(_dsa_sample_kernel, pg=pg, kc=kc, topk=topk, idx_bits=total.bit_length(), n_new=n_new),
        out_shape=jax.ShapeDtypeStruct((DB, tq, D), BF16),
        grid_spec=grid_spec,
        compiler_params=_cparams("parallel", "arbitrary"),
        name="dsa_sample_attend",
    )(page_table, q, qi, wi, kv_new, ki_new, *([cache_kv] * pg), *([cache_ki] * pg))


def _nsa_compress_kernel(pt_ref, *refs, pg):
    x_refs = refs[:pg]
    pe_ref, w1_ref, w2_ref, o_ref, rows_ref = refs[pg:]
    j = pl.program_id(2)
    pair = 2 * CMP_LEN
    per_step = pg * PAGE_SIZE // pair
    x = jnp.concatenate([r[0, 0] for r in x_refs], axis=0) if pg > 1 else x_refs[0][0, 0]
    rows_ref[:, pl.ds(pl.multiple_of(j * per_step, per_step), per_step), :] = jnp.swapaxes(
        x.reshape(per_step, pair, x.shape[-1]), 0, 1)

    @pl.when(j == pl.num_programs(2) - 1)
    def _():
        acc = None
        for l in range(CMP_LEN):
            xl = (jnp.concatenate([rows_ref[l], rows_ref[CMP_LEN + l]], axis=0) + pe_ref[0, l:l + 1, :]).astype(BF16)
            part = jnp.dot(xl, w1_ref[0, l], preferred_element_type=F32)
            acc = part if acc is None else acc + part
        hid = _gelu(acc).astype(BF16)
        o_ref[0] = jnp.dot(hid, w2_ref[0], preferred_element_type=F32).astype(o_ref.dtype)


def nsa_compress(rows, layer, page_table, pe_t, w1_bd, w2_bd):
    NB, n_pages = page_table.shape
    nc = n_pages * PAGE_SIZE // CMP_LEN
    pg = _pages_per_step(n_pages)
    grid_spec = pltpu.PrefetchScalarGridSpec(
        num_scalar_prefetch=1,
        grid=(NB, 2, n_pages // pg),
        in_specs=[pl.BlockSpec((1, 1, PAGE_SIZE, KV_WIDTH), lambda b, s, j, pt, k=k: (layer, pt[b, j * pg + k], 0, s))
                  for k in range(pg)]
        + [pl.BlockSpec((1, CMP_LEN, KV_WIDTH), lambda b, s, j, pt: (s, 0, 0)),
           pl.BlockSpec((1, CMP_LEN, KV_WIDTH, N_KV_HEADS * CMP_HIDDEN), lambda b, s, j, pt: (s, 0, 0, 0)),
           pl.BlockSpec((1, N_KV_HEADS * CMP_HIDDEN, KV_WIDTH), lambda b, s, j, pt: (s, 0, 0))],
        out_specs=pl.BlockSpec((1, nc, KV_WIDTH), lambda b, s, j, pt: (b, 0, s)),
        scratch_shapes=[pltpu.VMEM((2 * CMP_LEN, nc // 2, KV_WIDTH), F32)],
    )
    return pl.pallas_call(
        functools.partial(_nsa_compress_kernel, pg=pg),
        out_shape=jax.ShapeDtypeStruct((NB, nc, 2 * KV_WIDTH), BF16),
        grid_spec=grid_spec,
        compiler_params=_cparams("parallel", "arbitrary", "arbitrary"),
        name="nsa_compress",
    )(page_table, *([rows] * pg), pe_t, w1_bd, w2_bd)


def nsa_compress_weights(cmp_pe, cmp_w1, cmp_w2):
    eye = jnp.eye(N_KV_HEADS, dtype=F32)
    w1 = cmp_w1.reshape(2, CMP_LEN, HEAD_DIM, CMP_HIDDEN)
    w1_bd = jnp.einsum('hg,kldj->klhdgj', eye, w1).reshape(2, CMP_LEN, KV_WIDTH, N_KV_HEADS * CMP_HIDDEN)
    w2_bd = jnp.einsum('hg,kjd->khjgd', eye, cmp_w2).reshape(2, N_KV_HEADS * CMP_HIDDEN, KV_WIDTH)
    pe_t = jnp.tile(cmp_pe, (1, 1, N_KV_HEADS))
    return pe_t, w1_bd.astype(BF16), w2_bd.astype(BF16)


def _nsa_compressed_branch(qs_ref, kvc, qpos, gate_of, out_ref, tq):
    nc = kvc.shape[0]
    half = nc // 2
    pcol = lax.broadcasted_iota(jnp.int32, (1, nc), 1)
    cidx = jnp.where(pcol < half, 2 * pcol, 2 * (pcol - half) + 1)
    visible = (cidx + 1) * CMP_LEN - 1 <= qpos
    imps = []
    for h in range(N_KV_HEADS):
        cols = slice(h * HEAD_DIM, (h + 1) * HEAD_DIM)
        logits = lax.dot_general(qs_ref[h * GROUP * tq:(h + 1) * GROUP * tq, :], kvc[:, cols], _NT,
                                 preferred_element_type=F32)
        vcols = slice(KV_WIDTH + h * HEAD_DIM, KV_WIDTH + (h + 1) * HEAD_DIM)
        imp = None
        for g in range(GROUP):
            hd = h * GROUP + g
            s = jnp.where(visible, logits[g * tq:(g + 1) * tq], NEG_INF)
            m = jnp.max(s, axis=1, keepdims=True)
            e = jnp.exp(s - jnp.where(m == NEG_INF, 0.0, m))
            pc = e / jnp.maximum(jnp.sum(e, axis=1, keepdims=True), 1e-30)
            o = jnp.dot(pc.astype(BF16), kvc[:, vcols], preferred_element_type=F32)
            out_ref[hd * tq:(hd + 1) * tq, :] = gate_of(hd, 0) * o
            pair = pc[:, :half] + pc[:, half:]
            imp = pair if imp is None else imp + pair
        imps.append(imp)
    return imps


def _select_blocks(imp, cur, n_blocks, n_pick, axis):
    blk = lax.broadcasted_iota(jnp.int32, imp.shape, axis)
    forced = (blk == 0) | (blk == cur) | (blk == cur - 1)
    x = jnp.where(forced, 16.0, imp)
    x = jnp.where(blk <= cur, x, -1.0)
    x = jnp.where(blk < n_blocks, x, -2.0)
    blk_f = blk.astype(F32)
    sel = jnp.zeros(imp.shape, F32)
    for _ in range(n_pick):
        m = jnp.max(x, axis=axis, keepdims=True)
        first = jnp.min(jnp.where(x == m, blk_f, float(imp.shape[axis])), axis=axis, keepdims=True)
        hit = blk_f == first
        sel = jnp.where(hit, 1.0, sel)
        x = jnp.where(hit, -2.0, x)
    return sel


def _nsa_finish_branch(br, gate_of, out_ref, l_ref, acc_ref, tq):
    for hd in range(N_HEADS):
        rows = slice(hd * tq, (hd + 1) * tq)
        out_ref[rows, :] = out_ref[rows, :] + gate_of(hd, br) * _flash_result(rows, l_ref, acc_ref)


def _nsa_selected_branch(sel_ref, kv_chunk, nck, kc, qpos, qs_ref, m_ref, l_ref, acc_ref, tq):
    ns_pad = sel_ref.shape[2]
    _flash_init(m_ref, l_ref, acc_ref)
    brow = lax.broadcasted_iota(jnp.int32, (ns_pad, kc), 0)
    kcol = lax.broadcasted_iota(jnp.int32, (ns_pad, kc), 1)
    kiota = lax.broadcasted_iota(jnp.int32, (1, kc), 1)

    def attend(c, carry):
        c0 = pl.multiple_of(c * kc, kc)
        expand = jnp.where((c0 + kcol) // SEL_BLOCK == brow, 1.0, 0.0).astype(BF16)
        causal = c0 + kiota <= qpos
        kch, vch = kv_chunk(c0)

        def mask_of(h):
            picked = jnp.dot(sel_ref[h], expand, preferred_element_type=F32)
            return jnp.where(causal, picked, 0.0) > 0.5

        _grouped_flash_step(mask_of, kch, vch, qs_ref, m_ref, l_ref, acc_ref, tq)
        return carry

    lax.fori_loop(0, nck, attend, 0)


def _nsa_window_branch(kw, vw, kwpos, qpos, qs_ref, m_ref, l_ref, acc_ref, tq):
    _flash_init(m_ref, l_ref, acc_ref)
    dist = qpos - kwpos
    wmask = (dist >= 0) & (dist < WINDOW) & (kwpos >= 0)
    _grouped_flash_step(lambda h: wmask, kw, vw, qs_ref, m_ref, l_ref, acc_ref, tq)


def _gate_fn(gates_ref):
    sig = jax.nn.sigmoid(gates_ref[0])
    return lambda hd, br: sig[:, hd * 3 + br:hd * 3 + br + 1]


def _nsa_write_out(o_ref, out_ref, tq):
    for hd in range(N_HEADS):
        o_ref[0, :, hd * HEAD_DIM:(hd + 1) * HEAD_DIM] = out_ref[hd * tq:(hd + 1) * tq, :].astype(o_ref.dtype)


def _nsa_finish_branch_t(br, gate_of, out_ref, fl, tq):
    for h in range(N_KV_HEADS):
        res = _flash_result_t(h, *fl)
        for g in range(GROUP):
            hd = h * GROUP + g
            rows = slice(hd * tq, (hd + 1) * tq)
            out_ref[rows, :] = out_ref[rows, :] + gate_of(hd, br) * res[g * tq:(g + 1) * tq]


def _nsa_prompt_kernel(q_ref, gates_ref, kvc_ref, kk_ref, vt_ref, o_ref, selt_ref, out_ref, *fl, kc, wlen):
    qs_ref = fl[0]
    tq = q_ref.shape[1]
    S = kk_ref.shape[1]
    t0 = pl.program_id(1) * tq
    _stack_heads(q_ref, qs_ref, tq)
    qpos = t0 + lax.broadcasted_iota(jnp.int32, (tq, 1), 0)
    qpos_row = t0 + lax.broadcasted_iota(jnp.int32, (1, tq), 1)
    gate_of = _gate_fn(gates_ref)
    imps = _nsa_compressed_branch(qs_ref, kvc_ref[0], qpos, gate_of, out_ref, tq)
    n_blocks = S // SEL_BLOCK
    for h in range(N_KV_HEADS):
        selt_ref[h] = _select_blocks(imps[h].T, qpos_row // SEL_BLOCK, n_blocks, min(N_SEL, n_blocks), 0).astype(BF16)
    nck = (t0 + tq + kc - 1) // kc
    _flash_init_t(*fl)
    krow = lax.broadcasted_iota(jnp.int32, (kc, n_blocks), 0)
    bcol = lax.broadcasted_iota(jnp.int32, (kc, n_blocks), 1)
    kiota = lax.broadcasted_iota(jnp.int32, (kc, 1), 0)

    def attend(c, carry):
        c0 = pl.multiple_of(c * kc, kc)
        expand = jnp.where((c0 + krow) // SEL_BLOCK == bcol, 1.0, 0.0).astype(BF16)
        causal = c0 + kiota <= qpos_row

        def bias_of(h):
            picked = jnp.dot(expand, selt_ref[h], preferred_element_type=F32)
            return jnp.where(causal, jnp.where(picked > 0.5, 0.0, NEG_INF), NEG_INF)

        _flash_step_t(bias_of, kk_ref[0, pl.ds(c0, kc), 0:KV_WIDTH], vt_ref[0, 0:KV_WIDTH, pl.ds(c0, kc)],
                      *fl, tq=tq)
        return carry

    lax.fori_loop(0, nck, attend, 0)
    _nsa_finish_branch_t(1, gate_of, out_ref, fl, tq)
    start = pl.multiple_of(jnp.clip(t0 + tq - wlen, 0, S - wlen), tq)
    dist = qpos_row - (start + lax.broadcasted_iota(jnp.int32, (wlen, 1), 0))
    wbias = jnp.where(dist >= 0, jnp.where(dist < WINDOW, 0.0, NEG_INF), NEG_INF)
    _flash_init_t(*fl)
    _flash_step_t(lambda h: wbias, kk_ref[0, pl.ds(start, wlen), KV_WIDTH:2 * KV_WIDTH],
                  vt_ref[0, KV_WIDTH:2 * KV_WIDTH, pl.ds(start, wlen)], *fl, tq=tq)
    _nsa_finish_branch_t(2, gate_of, out_ref, fl, tq)
    _nsa_write_out(o_ref, out_ref, tq)


def nsa_prompt_attend(q, gates, kvc, kk, vt, tq=Q_BLOCK, kc=512):
    B, S, D = q.shape
    kc = min(kc, S)
    wlen = min(WINDOW + tq, S)
    n_blocks = S // SEL_BLOCK
    blk = lambda w: pl.BlockSpec((1, tq, w), lambda b, i: (b, i, 0))
    return pl.pallas_call(
        functools.partial(_nsa_prompt_kernel, kc=kc, wlen=wlen),
        out_shape=jax.ShapeDtypeStruct((B, S, D), BF16),
        grid=(B, S // tq),
        in_specs=[blk(D), blk(gates.shape[-1]),
                  pl.BlockSpec((1,) + kvc.shape[1:], lambda b, i: (b, 0, 0)),
                  pl.BlockSpec((1, S, 2 * KV_WIDTH), lambda b, i: (b, 0, 0), pipeline_mode=pl.Buffered(1)),
                  pl.BlockSpec((1, 2 * KV_WIDTH, S), lambda b, i: (b, 0, 0), pipeline_mode=pl.Buffered(1))],
        out_specs=blk(D),
        scratch_shapes=[pltpu.VMEM((N_KV_HEADS, n_blocks, tq), BF16),
                        pltpu.VMEM((N_HEADS * tq, HEAD_DIM), F32)] + _flash_scratch_t(tq, max(kc, wlen)),
        compiler_params=_cparams("parallel", "arbitrary"),
        name="nsa_prompt_attend",
    )(q, gates, kvc, kk, vt)


def _nsa_sample_kernel(pt_ref, q_ref, gates_ref, kvc_ref, kvn_ref, win_ref, winn_ref, *refs, pg, kc, n_new):
    csel_refs = refs[:pg]
    o_ref, kv_ref, wkv_ref, sel_ref, out_ref, qs_ref, m_ref, l_ref, acc_ref = refs[pg:]
    tq = q_ref.shape[1]
    j = pl.program_id(1)
    past = kv_ref.shape[0] - PAGE_SIZE
    for k in range(pg):
        p0 = pl.multiple_of((j * pg + k) * PAGE_SIZE, PAGE_SIZE)
        kv_ref[pl.ds(p0, PAGE_SIZE), :] = csel_refs[k][0, 0].astype(BF16)

    @pl.when(j == pl.num_programs(1) - 1)
    def _():
        kv_ref[past:past + PAGE_SIZE, :] = kvn_ref[0]
        wb = win_ref.shape[1]
        wkv_ref[0:wb, :] = win_ref[0].astype(BF16)
        wkv_ref[wb:wb + PAGE_SIZE, :] = winn_ref[0]
        _stack_heads(q_ref, qs_ref, tq)
        qpos = past + jnp.minimum(lax.broadcasted_iota(jnp.int32, (tq, 1), 0), n_new - 1)
        gate_of = _gate_fn(gates_ref)
        imps = _nsa_compressed_branch(qs_ref, kvc_ref[0], qpos, gate_of, out_ref, tq)
        n_blocks = -(-(past + n_new) // SEL_BLOCK)
        ns_pad = sel_ref.shape[2]
        for h in range(N_KV_HEADS):
            imp = imps[h]
            imp = jnp.concatenate([imp, jnp.zeros((tq, ns_pad - imp.shape[1]), F32)], axis=1)
            sel_ref[h] = _select_blocks(imp, qpos // SEL_BLOCK, n_blocks, min(N_SEL, n_blocks), 1).astype(BF16)
        nck = (past + PAGE_SIZE) // kc
        kv_chunk = lambda c0: (kv_ref[pl.ds(c0, kc), 0:KV_WIDTH], kv_ref[pl.ds(c0, kc), KV_WIDTH:2 * KV_WIDTH])
        _nsa_selected_branch(sel_ref, kv_chunk, nck, kc, qpos, qs_ref, m_ref, l_ref, acc_ref, tq)
        _nsa_finish_branch(1, gate_of, out_ref, l_ref, acc_ref, tq)
        kwpos = past - wb + lax.broadcasted_iota(jnp.int32, (1, wb + PAGE_SIZE), 1)
        _nsa_window_branch(wkv_ref[:, 0:KV_WIDTH], wkv_ref[:, KV_WIDTH:2 * KV_WIDTH], kwpos, qpos,
                           qs_ref, m_ref, l_ref, acc_ref, tq)
        _nsa_finish_branch(2, gate_of, out_ref, l_ref, acc_ref, tq)
        _nsa_write_out(o_ref, out_ref, tq)


def nsa_sample_attend(q, gates, kvc, sel_new, win_buf, win_new, cache_kv, layer, page_table, n_new):
    DB, tq, D = q.shape
    n_pages = page_table.shape[1]
    past = n_pages * PAGE_SIZE
    total = past + PAGE_SIZE
    kc = _key_chunk(total)
    wb = win_buf.shape[1]
    ns_pad = -(-(total // SEL_BLOCK) // LANES) * LANES
    per_b = lambda r, w: pl.BlockSpec((1, r, w), lambda b, j, pt: (b, 0, 0))
    pg = _pages_per_step(n_pages)
    grid_spec = pltpu.PrefetchScalarGridSpec(
        num_scalar_prefetch=1,
        grid=(DB, n_pages // pg),
        in_specs=[per_b(tq, D), per_b(tq, gates.shape[-1]), per_b(kvc.shape[1], kvc.shape[2]),
                  per_b(PAGE_SIZE, 2 * KV_WIDTH), per_b(wb, 2 * KV_WIDTH), per_b(PAGE_SIZE, 2 * KV_WIDTH)]
        + _page_specs(2 * KV_WIDTH, 1, layer, pg),
        out_specs=per_b(tq, D),
        scratch_shapes=[pltpu.VMEM((total, 2 * KV_WIDTH), BF16),
                        pltpu.VMEM((wb + PAGE_SIZE, 2 * KV_WIDTH), BF16),
                        pltpu.VMEM((N_KV_HEADS, tq, ns_pad), BF16),
                        pltpu.VMEM((N_HEADS * tq, HEAD_DIM), F32)] + _flash_scratch(tq),
    )
    return pl.pallas_call(
        functools.partial(_nsa_sample_kernel, pg=pg, kc=kc, n_new=n_new),
        out_shape=jax.ShapeDtypeStruct((DB, tq, D), BF16),
        grid_spec=grid_spec,
        compiler_params=_cparams("parallel", "arbitrary"),
        name="nsa_sample_attend",
    )(page_table, q, gates, kvc, sel_new, win_buf, win_new, *([cache_kv] * pg))


HALO = 8
G_IG, G_LF, G_CUM = 0, MLSTM_HEADS, 2 * MLSTM_HEADS
MASKED_GATE = -1e30


def _split3(x):
    hi = x.astype(BF16)
    r1 = x - hi.astype(F32)
    mid = r1.astype(BF16)
    lo = (r1 - mid.astype(F32)).astype(BF16)
    return hi, mid, lo


def _mlstm_pre_kernel(xm_ref, halo_ref, cw_ref, cb_ref, wq_ref, wk_ref, wv_ref, wvt_ref, wg_ref, bg_ref,
                      q_ref, k_ref, v_ref, vt_ref, xc_ref, g_ref, gt_ref, xp_ref, *, chunk, n_valid):
    tm = xm_ref.shape[1]
    hd = MLSTM_HEAD_DIM
    xp_ref[0:HALO, :] = halo_ref[0, 0]
    xp_ref[HALO:HALO + tm, :] = xm_ref[0]
    y = cb_ref[...]
    for j in range(CONV_WIDTH):
        off = HALO - (CONV_WIDTH - 1) + j
        y = y + cw_ref[j:j + 1, :] * xp_ref[off:off + tm, :]
    xc = y * jax.nn.sigmoid(y)
    xc_ref[0] = xc
    xcb = xc.astype(BF16)
    xmb = xm_ref[0].astype(BF16)
    g = bg_ref[...]
    for h in range(MLSTM_HEADS):
        cols = slice(h * hd, (h + 1) * hd)
        q = jnp.dot(xcb[:, cols], wq_ref[h], preferred_element_type=F32).astype(BF16)
        k = (jnp.dot(xcb[:, cols], wk_ref[h], preferred_element_type=F32) * hd ** -0.5).astype(BF16)
        v = jnp.dot(xmb[:, cols], wv_ref[h], preferred_element_type=F32).astype(BF16)
        q_ref[0, :, cols] = q
        k_ref[0, :, cols] = k
        v_ref[0, :, cols] = v
        vt_ref[0, cols, :] = lax.dot_general(wvt_ref[h], xmb[:, cols], _NT, preferred_element_type=F32).astype(BF16)
        for i, a in enumerate((q, k, v)):
            g = g + jnp.dot(a, wg_ref[i * MLSTM_INNER + h * hd:i * MLSTM_INNER + (h + 1) * hd, :],
                            preferred_element_type=F32)
    col = lax.broadcasted_iota(jnp.int32, (tm, LANES), 1)
    row = pl.program_id(1) * tm + lax.broadcasted_iota(jnp.int32, (tm, LANES), 0)
    log_f = jnp.minimum(g, 0.0) - jnp.log1p(jnp.exp(-jnp.abs(g)))
    g = jnp.where(col < G_LF, g, log_f)
    live = row < n_valid
    g = jnp.where(live, g, jnp.where(col < G_LF, MASKED_GATE, 0.0))
    tri = (lax.broadcasted_iota(jnp.int32, (chunk, chunk), 0) >= lax.broadcasted_iota(jnp.int32, (chunk, chunk), 1))
    tri = jnp.where(tri, 1.0, 0.0).astype(BF16)
    for c in range(tm // chunk):
        rows = slice(c * chunk, (c + 1) * chunk)
        cum = None
        for piece in _split3(g[rows]):
            part = jnp.dot(tri, piece, preferred_element_type=F32)
            cum = part if cum is None else cum + part
        out = jnp.where(col[rows] < G_CUM, g[rows], pltpu.roll(cum, G_CUM - G_LF, 1))
        g_ref[0, rows, :] = out
        gt_ref[0, :, rows] = out.T[0:gt_ref.shape[1], :]


def mlstm_pre(xm, halo, conv_w, conv_b, wq, wk, wv, wvt, wg, bg, tm, chunk, n_valid):
    NB, S, W = xm.shape
    hd = MLSTM_HEAD_DIM
    const = lambda a: pl.BlockSpec(a.shape, lambda b, i: (0,) * a.ndim)
    row_blk = lambda w: pl.BlockSpec((1, tm, w), lambda b, i: (b, i, 0))
    sds = jax.ShapeDtypeStruct
    return pl.pallas_call(
        functools.partial(_mlstm_pre_kernel, chunk=chunk, n_valid=n_valid),
        out_shape=[sds((NB, S, W), BF16)] * 3 + [sds((NB, W, S), BF16), sds((NB, S, W), F32),
                                                 sds((NB, S, LANES), F32), sds((NB, 2 * HALO, S), F32)],
        grid=(NB, S // tm),
        in_specs=[row_blk(W), pl.BlockSpec((1, 1, HALO, W), lambda b, i: (b, i, 0, 0)),
                  const(conv_w), const(conv_b), const(wq), const(wk), const(wv), const(wvt), const(wg), const(bg)],
        out_specs=[row_blk(W)] * 3 + [pl.BlockSpec((1, W, tm), lambda b, i: (b, 0, i)), row_blk(W), row_blk(LANES),
                                      pl.BlockSpec((1, 2 * HALO, tm), lambda b, i: (b, 0, i))],
        scratch_shapes=[pltpu.VMEM((HALO + tm, W), F32)],
        compiler_params=_cparams("parallel", "parallel"),
        name="mlstm_pre",
    )(xm, halo, conv_w, conv_b, wq, wk, wv, wvt, wg, bg)


def _mlstm_scan_kernel(q_ref, k_ref, v_ref, vt_ref, g_ref, gt_ref, xc_ref, z_ref, ng_ref, sk_ref, c0_ref, n0_ref, m0_ref,
                       o_ref, c_out, n_out, m_out, c_ref, n_ref, m_ref):
    L = q_ref.shape[1]
    h = pl.program_id(1)
    ci = pl.program_id(2)

    @pl.when(ci == 0)
    def _():
        c_ref[...] = c0_ref[0, 0]
        n_ref[...] = n0_ref[0, 0]
        m_ref[...] = m0_ref[0, 0]

    q, k, v = q_ref[0], k_ref[0], v_ref[0]
    col = lax.broadcasted_iota(jnp.int32, (L, LANES), 1)
    b_col = jnp.sum(jnp.where(col == G_CUM + h, g_ref[0], 0.0), axis=1, keepdims=True)
    i_row = gt_ref[0, pl.ds(G_IG + h, 1), :]
    b_row = gt_ref[0, pl.ds(G_CUM + h, 1), :]
    m = m_ref[0:1, 0:1]
    inter = b_col + m
    dmat = b_col - b_row + i_row
    tril = lax.broadcasted_iota(jnp.int32, (L, L), 0) >= lax.broadcasted_iota(jnp.int32, (L, L), 1)
    dmat = jnp.where(tril, dmat, NEG_INF)
    m_loc = jnp.maximum(inter, jnp.max(dmat, axis=1, keepdims=True))
    a = lax.dot_general(q, k, _NT, preferred_element_type=F32) * jnp.exp(dmat - m_loc)
    w_inter = jnp.exp(inter - m_loc)
    c_prev = c_ref[...]
    n_prev = n_ref[...]
    num = (jnp.dot(a.astype(BF16), v, preferred_element_type=F32)
           + w_inter * lax.dot_general(q, c_prev.astype(BF16), _NT, preferred_element_type=F32))
    qn = lax.dot_general(q, n_prev.astype(BF16), _NT, preferred_element_type=F32)[:, 0:1]
    den = jnp.sum(a, axis=1, keepdims=True) + w_inter * qn
    hc = num / jnp.maximum(jnp.abs(den), jnp.exp(-m_loc))
    b_end = b_row[:, L - 1:L]
    lg = b_end - b_row + i_row
    m_new = jnp.maximum(b_end + m, jnp.max(lg, axis=1, keepdims=True))
    wg = jnp.exp(lg - m_new)
    decay = jnp.exp(b_end + m - m_new)
    c_ref[...] = decay * c_prev + jnp.dot((vt_ref[0] * wg).astype(BF16), k, preferred_element_type=F32)
    n_ref[...] = decay * n_prev + jnp.dot(jnp.broadcast_to(wg, (n_ref.shape[0], L)).astype(BF16), k,
                                          preferred_element_type=F32)
    m_ref[...] = jnp.broadcast_to(m_new, m_ref.shape)
    mu = jnp.mean(hc, axis=1, keepdims=True)
    hz = hc - mu
    var = jnp.mean(hz * hz, axis=1, keepdims=True)
    hn = hz * lax.rsqrt(var + LN_EPS) * ng_ref[...]
    z = z_ref[0]
    o_ref[0] = ((hn + sk_ref[...] * xc_ref[0]) * (z * jax.nn.sigmoid(z))).astype(o_ref.dtype)

    @pl.when(ci == pl.num_programs(2) - 1)
    def _():
        c_out[0, 0] = c_ref[...]
        n_out[0, 0] = n_ref[...]
        m_out[0, 0] = m_ref[...]


def mlstm_scan(q, k, v, vt, g, gt, xc, z, norm_g, skip, c0, n0, m0, chunk):
    NB, S, W = q.shape
    hd = MLSTM_HEAD_DIM
    H = W // hd
    seq = lambda: pl.BlockSpec((1, chunk, hd), lambda b, h, c: (b, c, h))
    vec = pl.BlockSpec((1, hd), lambda b, h, c: (0, h))
    st = lambda r, w: pl.BlockSpec((1, 1, r, w), lambda b, h, c: (b, h, 0, 0))
    sds = jax.ShapeDtypeStruct
    return pl.pallas_call(
        _mlstm_scan_kernel,
        out_shape=[sds((NB, S, W), BF16), sds((NB, H, hd, hd), F32), sds((NB, H, HALO, hd), F32), sds((NB, H, HALO, LANES), F32)],
        grid=(NB, H, S // chunk),
        in_specs=[seq(), seq(), seq(), pl.BlockSpec((1, hd, chunk), lambda b, h, c: (b, h, c)),
                  pl.BlockSpec((1, chunk, LANES), lambda b, h, c: (b, c, 0)),
                  pl.BlockSpec((1, 2 * HALO, chunk), lambda b, h, c: (b, 0, c)),
                  seq(), seq(), vec, vec, st(hd, hd), st(HALO, hd), st(HALO, LANES)],
        out_specs=[seq(), st(hd, hd), st(HALO, hd), st(HALO, LANES)],
        scratch_shapes=[pltpu.VMEM((hd, hd), F32), pltpu.VMEM((HALO, hd), F32), pltpu.VMEM((HALO, LANES), F32)],
        compiler_params=_cparams("parallel", "parallel", "arbitrary"),
        name="mlstm_scan",
    )(q, k, v, vt, g, gt, xc, z, norm_g.reshape(1, W), skip.reshape(1, W), c0, n0, m0)


SAMPLE_Q_ROWS = 16
TM_PROMPT = 512
TM_PEER_ROUTE = 256
TM_PEER = 512
TM_MLSTM = 256
CHUNK_MLSTM = 256


def _pad_rows(a, rows):
    return jnp.pad(a, ((0, 0), (0, rows - a.shape[1]), (0, 0)))


def _pad_cols(a, cols):
    return jnp.pad(a, ((0, 0),) * (a.ndim - 1) + ((0, cols - a.shape[-1]),))


def _dsa_layer(xp, xs, mp, ms, w_in, w_o, ln_g, ln_b, cache_kv, cache_ki, layer, page_table, DB, T):
    B, S, D = xp.shape
    wb = w_in.astype(BF16)
    o1, o2, o3, o4 = D, D + KV_WIDTH, D + 2 * KV_WIDTH, D + 2 * KV_WIDTH + IDX_HEADS * IDX_DIM
    weights = [wb[:, :o1], wb[:, o1:o3], wb[:, o1:o2], wb[:, o2:o3], wb[:, o3:o4], wb[:, o4:o4 + IDX_DIM],
               _pad_cols(wb[:, o4 + IDX_DIM:], LANES)]
    outs = [(0, BF16), (1, F32), (2, BF16), (3, BF16), (4, BF16), (5, F32), (5, BF16), (6, F32)]
    w_ob = w_o.astype(BF16)
    q, kv32, kb, _, qi, ki32, kib, wi, vt = proj(xp, mp(0, 0), mp(0, 1), weights, outs, TM_PROMPT,
                                                 weights_t=[wb[:, o2:o3].T])
    o = dsa_prompt_attend(q, qi, wi, kb, vt, kib)
    xp = outproj_postnorm(o, w_ob, xp, mp(0, 2), ln_g, ln_b, TM_PROMPT)
    kv_p = kv32.reshape(B, S, 2, N_KV_HEADS, HEAD_DIM)
    n = DB * T
    q, kv32s, kb, vb, qi, ki32s, kib, wi = proj(xs, ms(0, 0), ms(0, 1), weights, outs, n)
    per_b = lambda a: a.reshape(DB, T, a.shape[-1])
    kv_new = _pad_rows(jnp.concatenate([per_b(kb), per_b(vb)], axis=-1), PAGE_SIZE)
    o = dsa_sample_attend(_pad_rows(per_b(q), SAMPLE_Q_ROWS), _pad_rows(per_b(qi), SAMPLE_Q_ROWS),
                          _pad_rows(per_b(wi), SAMPLE_Q_ROWS), kv_new, _pad_rows(per_b(kib), PAGE_SIZE),
                          cache_kv.reshape(cache_kv.shape[:3] + (2 * KV_WIDTH,)), cache_ki, layer, page_table, T)
    xs = outproj_postnorm(o[:, :T].reshape(1, n, D), w_ob, xs, ms(0, 2), ln_g, ln_b, n)
    kv_s = kv32s.reshape(DB, T, 2, N_KV_HEADS, HEAD_DIM)
    return xp, xs, kv_p, kv_s, ki32, ki32s.reshape(DB, T, IDX_DIM)


def _nsa_layer(xp, xs, mp, ms, w_in, cmp_pe, cmp_w1, cmp_w2, w_o, ln_g, ln_b, cache_kv, win_state, layer,
               page_table, DB, T):
    B, S, D = xp.shape
    wb = w_in.astype(BF16)
    c1, c2, c3 = D + 4 * KV_WIDTH, D + 6 * KV_WIDTH, D + 2 * KV_WIDTH
    sel_k, sel_v = wb[:, c3:c3 + KV_WIDTH], wb[:, c3 + KV_WIDTH:c1]
    win_k, win_v = wb[:, c1:c1 + KV_WIDTH], wb[:, c1 + KV_WIDTH:c2]
    weights = [wb[:, :D], wb[:, D:c1], wb[:, c1:c2], wb[:, c3:c2], _pad_cols(wb[:, c2:], LANES),
               jnp.concatenate([sel_k, win_k], axis=1)]
    outs = [(0, BF16), (1, F32), (2, F32), (3, BF16), (4, F32)]
    w_ob = w_o.astype(BF16)
    cmp_w = nsa_compress_weights(cmp_pe, cmp_w1, cmp_w2)
    q, cs32, win32, kk, gates, vt = proj(xp, mp(0, 0), mp(0, 1), weights,
                                         [(0, BF16), (1, F32), (2, F32), (5, BF16), (4, F32)], TM_PROMPT,
                                         weights_t=[jnp.concatenate([sel_v, win_v], axis=1).T])
    pages = S // PAGE_SIZE
    ident = jnp.arange(B * pages, dtype=jnp.int32).reshape(B, pages)
    kvc = nsa_compress(cs32.reshape(1, B * pages, PAGE_SIZE, 4 * KV_WIDTH), 0, ident, *cmp_w)
    o = nsa_prompt_attend(q, gates, kvc, kk, vt)
    xp = outproj_postnorm(o, w_ob, xp, mp(0, 2), ln_g, ln_b, TM_PROMPT)
    kv_p = cs32.reshape(B, S, 2, 2, N_KV_HEADS, HEAD_DIM)
    keep = min(WINDOW, S)
    win_p = win32[:, S - keep:].reshape(B, keep, 2, N_KV_HEADS, HEAD_DIM)
    n = DB * T
    q, cs32s, win32s, selwin, gates = proj(xs, ms(0, 0), ms(0, 1), weights, outs, n)
    per_b = lambda a: a.reshape(DB, T, a.shape[-1])
    kvc = nsa_compress(cache_kv.reshape(cache_kv.shape[:3] + (4 * KV_WIDTH,)), layer, page_table, *cmp_w)
    selwin = per_b(selwin)
    wbuf = win_state[layer]
    o = nsa_sample_attend(_pad_rows(per_b(q), SAMPLE_Q_ROWS), _pad_rows(per_b(gates), SAMPLE_Q_ROWS), kvc,
                          _pad_rows(selwin[..., :2 * KV_WIDTH], PAGE_SIZE), wbuf.reshape(DB, wbuf.shape[1], 2 * KV_WIDTH),
                          _pad_rows(selwin[..., 2 * KV_WIDTH:], PAGE_SIZE),
                          cache_kv.reshape(cache_kv.shape[:3] + (4 * KV_WIDTH,)), layer, page_table, T)
    xs = outproj_postnorm(o[:, :T].reshape(1, n, D), w_ob, xs, ms(0, 2), ln_g, ln_b, n)
    kv_s = cs32s.reshape(DB, T, 2, 2, N_KV_HEADS, HEAD_DIM)
    win_s = jnp.concatenate([wbuf, win32s.reshape(DB, T, 2, N_KV_HEADS, HEAD_DIM)], axis=1)[:, T:]
    return xp, xs, kv_p, kv_s, win_p, win_s


def _mlstm_layer(xp, xs, mp, ms, w_in, conv_w, conv_b, w_qkv, w_gate, b_gate, norm_g, skip, w_o, ln_g, ln_b,
                 conv_state, c_state, n_state, m_state, DB, T):
    B, S, D = xp.shape
    W, H, hd = MLSTM_INNER, MLSTM_HEADS, MLSTM_HEAD_DIM
    wb = w_in.astype(BF16)
    weights = [wb[:, :W], wb[:, W:]]
    outs = [(0, F32), (1, F32)]
    wq, wk, wv = (w_qkv[i].astype(BF16) for i in range(3))
    pre_w = (conv_w, conv_b.reshape(1, W), wq, wk, wv, jnp.swapaxes(wv, 1, 2), _pad_cols(w_gate, LANES).astype(BF16),
             _pad_cols(b_gate.reshape(1, -1), LANES))
    w_ob = w_o.astype(BF16)
    keep = CONV_WIDTH - 1
    rep = lambda a, r: jnp.broadcast_to(a[..., None, :], a.shape[:-1] + (r, a.shape[-1]))
    xm, z = proj(xp, mp(0, 0), mp(0, 1), weights, outs, TM_PROMPT)
    tm = min(TM_MLSTM, S)
    tiles = xm.reshape(B, S // tm, tm, W)
    halo = jnp.concatenate([jnp.zeros((B, 1, HALO, W), F32), tiles[:, :-1, tm - HALO:]], axis=1)
    chunk = min(CHUNK_MLSTM, S)
    q, k, v, vt, xc, g, gt = mlstm_pre(xm, halo, *pre_w, tm=tm, chunk=chunk, n_valid=S)
    zeros = lambda *s: jnp.zeros(s, F32)
    o, c_p, n_p, m_p = mlstm_scan(q, k, v, vt, g, gt, xc, z, norm_g, skip, zeros(B, H, hd, hd), zeros(B, H, HALO, hd),
                                  zeros(B, H, HALO, LANES), chunk)
    xp = outproj_postnorm(o, w_ob, xp, mp(0, 2), ln_g, ln_b, TM_PROMPT)
    conv_p = jnp.concatenate([zeros(B, keep, W), xm], axis=1)[:, -keep:]
    n = DB * T
    xm_s, z_s = proj(xs, ms(0, 0), ms(0, 1), weights, outs, n)
    xm_s = xm_s.reshape(DB, T, W)
    rows = LANES
    halo = jnp.concatenate([zeros(DB, HALO - keep, W), conv_state], axis=1)[:, None]
    q, k, v, vt, xc, g, gt = mlstm_pre(_pad_rows(xm_s, rows), halo, *pre_w, tm=rows, chunk=rows, n_valid=T)
    m0 = jnp.broadcast_to(m_state[..., None, None], (DB, H, HALO, LANES))
    o, c_s, n_s, m_s = mlstm_scan(q, k, v, vt, g, gt, xc, _pad_rows(z_s.reshape(DB, T, W), rows), norm_g, skip,
                                  c_state, rep(n_state, HALO), m0, rows)
    xs = outproj_postnorm(o[:, :T].reshape(1, n, W), w_ob, xs, ms(0, 2), ln_g, ln_b, n)
    conv_s = jnp.concatenate([conv_state, xm_s], axis=1)[:, -keep:]
    return (xp, xs, conv_p, conv_s, c_p, c_s, n_p[:, :, 0], n_s[:, :, 0], m_p[:, :, 0, 0], m_s[:, :, 0, 0])


def kernel(x_prompt, x_sample, cache_a_kv, cache_a_kidx, cache_b_kv, state_b_win, state_c_conv, state_c_C, state_c_n,
           state_c_m, page_table, c_prompt, c_sample, a_w_in, a_w_o, b_w_in, b_cmp_pe, b_cmp_w1, b_cmp_w2, b_w_o,
           c_w_in, c_conv_w, c_conv_b, c_w_qkv, c_w_gate, c_b_gate, c_norm_g, c_skip, c_w_o,
           ada_w, ada_b, ln_g, ln_b, peer_w_q, peer_sub_keys, peer_u, peer_v):
    B, S, D = x_prompt.shape
    DB, T, _ = x_sample.shape
    n = DB * T
    n_cond = B + DB
    cond = _pad_rows(jnp.concatenate([c_prompt, c_sample], axis=0)[None], -(-n_cond // 8) * 8)[0]
    mods = adaln_all(cond, ada_w, ada_b).reshape(DEPTH, cond.shape[0], 2, 3, D)
    xp, xs = x_prompt, x_sample.reshape(1, n, D)
    outs = {name: [] for name in ("a_kv_p", "a_kv_s", "a_ki_p", "a_ki_s", "b_kv_p", "b_kv_s", "b_win_p", "b_win_s",
                                  "conv_p", "conv_s", "C_p", "C_s", "n_p", "n_s", "m_p", "m_s")}
    for i in range(DEPTH):
        kind, j = i % N_MIXERS, i // N_MIXERS
        mod_p = mods[i, :B]
        mod_s = jnp.repeat(mods[i, B:n_cond], T, axis=0)
        mp = lambda s, r, mod_p=mod_p: mod_p[:, s, r][:, None, :]
        ms = lambda s, r, mod_s=mod_s: mod_s[:, s, r][None]
        if kind == 0:
            xp, xs, kvp, kvs, kip, kis = _dsa_layer(xp, xs, mp, ms, a_w_in[j], a_w_o[j], ln_g[i, 0], ln_b[i, 0],
                                                    cache_a_kv, cache_a_kidx, j, page_table, DB, T)
            for name, val in zip(("a_kv_p", "a_kv_s", "a_ki_p", "a_ki_s"), (kvp, kvs, kip, kis)):
                outs[name].append(val)
        elif kind == 1:
            xp, xs, kvp, kvs, wp, ws = _nsa_layer(xp, xs, mp, ms, b_w_in[j], b_cmp_pe[j], b_cmp_w1[j], b_cmp_w2[j],
                                                  b_w_o[j], ln_g[i, 0], ln_b[i, 0], cache_b_kv, state_b_win, j,
                                                  page_table, DB, T)
            for name, val in zip(("b_kv_p", "b_kv_s", "b_win_p", "b_win_s"), (kvp, kvs, wp, ws)):
                outs[name].append(val)
        else:
            res = _mlstm_layer(xp, xs, mp, ms, c_w_in[j], c_conv_w[j], c_conv_b[j], c_w_qkv[j], c_w_gate[j], c_b_gate[j],
                               c_norm_g[j], c_skip[j], c_w_o[j], ln_g[i, 0], ln_b[i, 0],
                               state_c_conv[j], state_c_C[j], state_c_n[j], state_c_m[j], DB, T)
            xp, xs = res[0], res[1]
            for name, val in zip(("conv_p", "conv_s", "C_p", "C_s", "n_p", "n_s", "m_p", "m_s"), res[2:]):
                outs[name].append(val)
        peer_w = (peer_w_q[i].astype(BF16), peer_sub_keys[i].astype(BF16), peer_u[i].astype(BF16).T,
                  peer_v[i].astype(BF16), ln_g[i, 1], ln_b[i, 1])
        xp = peer_layer(xp, mp(1, 0), mp(1, 1), mp(1, 2), *peer_w, tm_route=min(TM_PEER_ROUTE, S), tm=min(TM_PEER, S))
        xs = peer_layer(xs, ms(1, 0), ms(1, 1), ms(1, 2), *peer_w, tm_route=n, tm=n)
    return (xp, xs.reshape(DB, T, D)) + tuple(jnp.stack(outs[name]) for name in outs)
```

```python
import functools
import math

import jax
import jax.numpy as jnp
from jax import lax
from jax.experimental import pallas as pl
from jax.experimental.pallas import tpu as pltpu

F32 = jnp.float32
BF16 = jnp.bfloat16

D_MODEL = 1024
DEPTH = 4
PAGE_SIZE = 128
N_MIXERS = 3

N_HEADS = 16
HEAD_DIM = D_MODEL // N_HEADS
N_KV_HEADS = 4
GROUP = N_HEADS // N_KV_HEADS
KV_WIDTH = N_KV_HEADS * HEAD_DIM
Q_BLOCK = 128
ATTN_SCALE = HEAD_DIM ** -0.5

IDX_HEADS = 8
IDX_DIM = 64
IDX_SCALE = (IDX_HEADS * IDX_DIM) ** -0.5
DSA_TOPK = 256

CMP_LEN = 32
CMP_HIDDEN = 2 * HEAD_DIM
SEL_BLOCK = 64
N_SEL = 16
WINDOW = 512

MLSTM_INNER = 2 * D_MODEL
MLSTM_HEADS = 4
MLSTM_HEAD_DIM = MLSTM_INNER // MLSTM_HEADS
CONV_WIDTH = 4
MLSTM_CHUNK = 64

PEER_HEADS = 8
PEER_KEYS = 128
N_EXPERTS = PEER_KEYS * PEER_KEYS
PEER_KEY_DIM = 256
PEER_TOPK = 16

ALPHA = (2 * DEPTH) ** 0.25
LN_EPS = 1e-5

LANES = 128
VMEM_LIMIT = 56 * 1024 * 1024

_NT = (((1,), (1,)), ((), ()))


def _cparams(*sem):
    return pltpu.CompilerParams(dimension_semantics=sem, vmem_limit_bytes=VMEM_LIMIT)


def _gelu(x):
    return 0.5 * x * (1.0 + jnp.tanh(math.sqrt(2.0 / math.pi) * (x + 0.044715 * (x * x * x))))


def _post_norm_math(x, y, gate, g, b):
    z = ALPHA * x + gate * y
    mu = jnp.mean(z, axis=-1, keepdims=True)
    zc = z - mu
    var = jnp.mean(zc * zc, axis=-1, keepdims=True)
    return zc * lax.rsqrt(var + LN_EPS) * g + b


def _adaln_kernel(c_ref, w_ref, b_ref, o_ref):
    c = c_ref[...]
    s = (c * jax.nn.sigmoid(c)).astype(BF16)
    o_ref[0] = jnp.dot(s, w_ref[0].astype(BF16), preferred_element_type=F32) + b_ref[0]


def adaln_all(c, ada_w, ada_b):
    M = c.shape[0]
    n = ada_w.shape[-1] // D_MODEL
    return pl.pallas_call(
        _adaln_kernel,
        out_shape=jax.ShapeDtypeStruct((DEPTH, M, n * D_MODEL), F32),
        grid=(DEPTH, n),
        in_specs=[pl.BlockSpec((M, D_MODEL), lambda i, j: (0, 0)),
                  pl.BlockSpec((1, D_MODEL, D_MODEL), lambda i, j: (i, 0, j)),
                  pl.BlockSpec((1, 1, D_MODEL), lambda i, j: (i, 0, j))],
        out_specs=pl.BlockSpec((1, M, D_MODEL), lambda i, j: (i, 0, j)),
        compiler_params=_cparams("arbitrary", "arbitrary"),
        name="adaln",
    )(c, ada_w, ada_b.reshape(DEPTH, 1, -1))


def _proj_kernel(x_ref, sh_ref, sc_ref, *refs, n_w, n_t, out_map):
    w_refs, wt_refs, o_refs = refs[:n_w], refs[n_w:n_w + n_t], refs[n_w + n_t:]
    h = (x_ref[0] * (1.0 + sc_ref[0]) + sh_ref[0]).astype(BF16)
    done = {}
    for o_ref, wi in zip(o_refs, out_map):
        if wi not in done:
            done[wi] = jnp.dot(h, w_refs[wi][...], preferred_element_type=F32)
        o_ref[0] = done[wi].astype(o_ref.dtype)
    for o_ref, wt_ref in zip(o_refs[len(out_map):], wt_refs):
        o_ref[0] = lax.dot_general(wt_ref[...], h, _NT, preferred_element_type=F32).astype(o_ref.dtype)


def proj(x, shift, scale, weights, outs, tm, weights_t=()):
    nb, S, D = x.shape
    rows = shift.shape[1]
    mblk = (1, tm, D) if rows == S else (1, 1, D)
    mmap = (lambda b, i: (b, i, 0)) if rows == S else (lambda b, i: (b, 0, 0))
    in_specs = [pl.BlockSpec((1, tm, D), lambda b, i: (b, i, 0)),
                pl.BlockSpec(mblk, mmap), pl.BlockSpec(mblk, mmap)]
    in_specs += [pl.BlockSpec(w.shape, lambda b, i: (0, 0)) for w in (*weights, *weights_t)]
    out_shape = [jax.ShapeDtypeStruct((nb, S, weights[wi].shape[1]), dt) for wi, dt in outs]
    out_specs = [pl.BlockSpec((1, tm, weights[wi].shape[1]), lambda b, i: (b, i, 0)) for wi, _ in outs]
    out_shape += [jax.ShapeDtypeStruct((nb, w.shape[0], S), BF16) for w in weights_t]
    out_specs += [pl.BlockSpec((1, w.shape[0], tm), lambda b, i: (b, 0, i)) for w in weights_t]
    return pl.pallas_call(
        functools.partial(_proj_kernel, n_w=len(weights), n_t=len(weights_t), out_map=tuple(wi for wi, _ in outs)),
        out_shape=out_shape, grid=(nb, S // tm), in_specs=in_specs, out_specs=out_specs,
        compiler_params=_cparams("parallel", "parallel"),
        name="proj",
    )(x, shift, scale, *weights, *weights_t)


def _outproj_kernel(o_ref, w_ref, x_ref, gate_ref, g_ref, b_ref, y_ref):
    y = jnp.dot(o_ref[0], w_ref[...], preferred_element_type=F32)
    y_ref[0] = _post_norm_math(x_ref[0], y, gate_ref[0], g_ref[...], b_ref[...])


def outproj_postnorm(o, w, x, gate, g, b, tm):
    nb, S, K = o.shape
    D = x.shape[-1]
    rows = gate.shape[1]
    mblk = (1, tm, D) if rows == S else (1, 1, D)
    mmap = (lambda bi, i: (bi, i, 0)) if rows == S else (lambda bi, i: (bi, 0, 0))
    return pl.pallas_call(
        _outproj_kernel,
        out_shape=jax.ShapeDtypeStruct(x.shape, F32),
        grid=(nb, S // tm),
        in_specs=[pl.BlockSpec((1, tm, K), lambda bi, i: (bi, i, 0)),
                  pl.BlockSpec((K, D), lambda bi, i: (0, 0)),
                  pl.BlockSpec((1, tm, D), lambda bi, i: (bi, i, 0)),
                  pl.BlockSpec(mblk, mmap),
                  pl.BlockSpec((1, D), lambda bi, i: (0, 0)),
                  pl.BlockSpec((1, D), lambda bi, i: (0, 0))],
        out_specs=pl.BlockSpec((1, tm, D), lambda bi, i: (bi, i, 0)),
        compiler_params=_cparams("parallel", "parallel"),
        name="outproj_postnorm",
    )(o, w, x, gate, g.reshape(1, D), b.reshape(1, D))


def _top_rows(x, iota, n):
    vals, idxs = [], []
    for _ in range(n):
        m = jnp.max(x, axis=0, keepdims=True)
        idx = jnp.min(jnp.where(x == m, iota, jnp.inf), axis=0, keepdims=True)
        vals.append(m)
        idxs.append(idx)
        x = jnp.where(iota == idx, -jnp.inf, x)
    return vals, idxs


def _peer_route_kernel(x_ref, sh_ref, sc_ref, wq_ref, keys_ref, i1_ref, i2_ref, g_ref):
    tm = x_ref.shape[1]
    h = (x_ref[0] * (1.0 + sc_ref[0]) + sh_ref[0]).astype(BF16)
    q = jnp.dot(h, wq_ref[...], preferred_element_type=F32).astype(BF16)
    iota_k = lax.broadcasted_iota(jnp.int32, (PEER_KEYS, tm), 0).astype(F32)
    half = PEER_KEY_DIM // 2
    n = PEER_TOPK
    row16 = lax.broadcasted_iota(jnp.int32, (n, tm), 0).astype(F32)
    row8 = lax.broadcasted_iota(jnp.int32, (8, tm), 0).astype(F32)
    pieces = [(0, 1, 16, 0, 15), (1, 1, 8, 0, 7), (2, 1, 8, 0, 4), (3, 1, 8, 0, 3),
              (0, 2, 16, 4, 15), (1, 2, 8, 4, 7), (2, 2, 8, 4, 4)]
    ids = []
    for fixed, which, rows, lo, hi in pieces:
        r = row16 if rows == 16 else row8
        pair = fixed * n + r if which == 1 else r * n + fixed
        ids.append(jnp.where(r < lo, float(n * n), jnp.where(r > hi, float(n * n), pair)))
    cand_id = jnp.concatenate(ids, axis=0)
    cand_ok = cand_id < float(n * n)
    i1_rows, i2_rows, g_rows = [], [], []
    for hd in range(PEER_HEADS):
        tops = []
        for c in range(2):
            col = (hd * 2 + c) * half
            s_t = lax.dot_general(keys_ref[c], q[:, col:col + half], _NT, preferred_element_type=F32)
            tops.append(_top_rows(s_t, iota_k, n))
        (v1, id1), (v2, id2) = tops
        stacked = {(w, rows): jnp.concatenate((v1, v2)[w - 1][:rows], axis=0) for w in (1, 2) for rows in (8, n)}
        sums = [(v1[fixed] + stacked[2, rows]) if which == 1 else (stacked[1, rows] + v2[fixed])
                for fixed, which, rows, _, _ in pieces]
        cand = jnp.where(cand_ok, jnp.concatenate(sums, axis=0), NEG_INF)
        cvals, cidx = _top_rows(cand, cand_id, n)
        cv = jnp.concatenate(cvals, axis=0)
        ci = jnp.concatenate(cidx, axis=0)
        r1 = jnp.floor(ci * (1.0 / PEER_TOPK))
        r2 = ci - r1 * PEER_TOPK
        i1 = jnp.zeros_like(ci)
        i2 = jnp.zeros_like(ci)
        for r in range(PEER_TOPK):
            i1 = i1 + jnp.where(r1 == float(r), id1[r], 0.0)
            i2 = i2 + jnp.where(r2 == float(r), id2[r], 0.0)
        e = jnp.exp(cv - cvals[0])
        gate = e / jnp.sum(e, axis=0, keepdims=True)
        i1_rows.append(i1)
        i2_rows.append(i2)
        g_rows.append(gate)
    i1_ref[0] = jnp.concatenate(i1_rows, axis=0).T
    i2_ref[0] = jnp.concatenate(i2_rows, axis=0).T
    g_ref[0] = jnp.concatenate(g_rows, axis=0).T


def _mod_specs(rows, S, tm, D):
    if rows == S:
        return pl.BlockSpec((1, tm, D), lambda b, i, *_: (b, i, 0))
    return pl.BlockSpec((1, 1, D), lambda b, i, *_: (b, 0, 0))


def peer_route(x, shift, scale, wq, keys, tm):
    nb, S, D = x.shape
    nsel = PEER_HEADS * PEER_TOPK
    mspec = _mod_specs(shift.shape[1], S, tm, D)
    return pl.pallas_call(
        _peer_route_kernel,
        out_shape=[jax.ShapeDtypeStruct((nb, S, nsel), F32)] * 3,
        grid=(nb, S // tm),
        in_specs=[pl.BlockSpec((1, tm, D), lambda b, i: (b, i, 0)), mspec, mspec,
                  pl.BlockSpec(wq.shape, lambda b, i: (0, 0)),
                  pl.BlockSpec(keys.shape, lambda b, i: (0, 0, 0))],
        out_specs=[pl.BlockSpec((1, tm, nsel), lambda b, i: (b, i, 0))] * 3,
        compiler_params=_cparams("parallel", "parallel"),
        name="peer_route",
    )(x, shift, scale, wq, keys)


PEER_TOKEN_GROUP = 16


def _peer_expert_kernel(x_ref, sh_ref, sc_ref, gt_ref, i1_ref, i2_ref, g_ref, ut_ref, v_ref, lg_ref, lb_ref,
                        y_ref, w3_ref, acc_ref, h_ref, *, ac):
    tm = x_ref.shape[1]
    j = pl.program_id(2)

    @pl.when(j == 0)
    def _():
        h_ref[...] = (x_ref[0] * (1.0 + sc_ref[0]) + sh_ref[0]).astype(BF16)
        acc_ref[...] = jnp.zeros_like(acc_ref)
        iota_s = lax.broadcasted_iota(jnp.int32, (PEER_KEYS, LANES), 0).astype(F32)

        def build(gi, carry):
            t0 = pl.multiple_of(gi * PEER_TOKEN_GROUP, PEER_TOKEN_GROUP)
            i1g = i1_ref[0, pl.ds(t0, PEER_TOKEN_GROUP), :]
            i2g = i2_ref[0, pl.ds(t0, PEER_TOKEN_GROUP), :]
            gg = g_ref[0, pl.ds(t0, PEER_TOKEN_GROUP), :]
            tiles = []
            for t in range(PEER_TOKEN_GROUP):
                g1t = jnp.where(iota_s == i1g[t:t + 1], gg[t:t + 1], 0.0).astype(BF16)
                o2t = jnp.where(iota_s == i2g[t:t + 1], 1.0, 0.0).astype(BF16)
                tiles.append(lax.dot_general(g1t, o2t, _NT, preferred_element_type=F32))
            w3_ref[:, pl.ds(t0, PEER_TOKEN_GROUP), :] = jnp.swapaxes(jnp.stack(tiles, axis=0), 0, 1).astype(BF16)
            return carry

        lax.fori_loop(0, tm // PEER_TOKEN_GROUP, build, 0)

    act = _gelu(jnp.dot(h_ref[...], ut_ref[...], preferred_element_type=F32))
    wc = jnp.concatenate([w3_ref[j * ac + a] for a in range(ac)], axis=1).astype(F32)
    coef = (wc * act).astype(BF16)
    acc_ref[...] += jnp.dot(coef, v_ref[...], preferred_element_type=F32)

    @pl.when(j == pl.num_programs(2) - 1)
    def _():
        y_ref[0] = _post_norm_math(x_ref[0], acc_ref[...], gt_ref[0], lg_ref[...], lb_ref[...])


def peer_experts(x, shift, scale, gate, i1, i2, g, ut, v, ln_g, ln_b, tm, ac):
    nb, S, D = x.shape
    assert tm % PEER_TOKEN_GROUP == 0 and S % tm == 0, (S, tm)
    nsel = PEER_HEADS * PEER_TOPK
    ce = ac * PEER_KEYS
    mspec = _mod_specs(shift.shape[1], S, tm, D)
    pick =pl.BlockSpec((1, tm, nsel), lambda b, i, j: (b, i, 0))
    return pl.pallas_call(
        functools.partial(_peer_expert_kernel, ac=ac),
        out_shape=jax.ShapeDtypeStruct(x.shape, F32),
        grid=(nb, S // tm, PEER_KEYS // ac),
        in_specs=[pl.BlockSpec((1, tm, D), lambda b, i, j: (b, i, 0)), mspec, mspec, mspec, pick, pick, pick,
                  pl.BlockSpec((D, ce), lambda b, i, j: (0, j)),
                  pl.BlockSpec((ce, D), lambda b, i, j: (j, 0)),
                  pl.BlockSpec((1, D), lambda b, i, j: (0, 0)),
                  pl.BlockSpec((1, D), lambda b, i, j: (0, 0))],
        out_specs=pl.BlockSpec((1, tm, D), lambda b, i, j: (b, i, 0)),
        scratch_shapes=[pltpu.VMEM((PEER_KEYS, tm, PEER_KEYS), BF16),
                        pltpu.VMEM((tm, D), F32),
                        pltpu.VMEM((tm, D), BF16)],
        compiler_params=_cparams("parallel", "parallel", "arbitrary"),
        name="peer_experts",
    )(x, shift, scale, gate, i1, i2, g, ut, v, ln_g.reshape(1, D), ln_b.reshape(1, D))


def peer_layer(x, shift, scale, gate, wq, keys, ut, v, ln_g, ln_b, tm_route, tm):
    i1, i2, g = peer_route(x, shift, scale, wq, keys, tm_route)
    return peer_experts(x, shift, scale, gate, i1, i2, g, ut, v, ln_g, ln_b, tm, ac=8)


INT_MIN = -2 ** 31
NEG_INF = float("-inf")


def _to_key(x):
    b = lax.bitcast_convert_type(x, jnp.int32)
    return b ^ ((b >> 31) & 0x7FFFFFFF)


COUNT_ROWS = 64


def _key_tile(key_ref, c0, kc, axis):
    return key_ref[:, pl.ds(c0, kc)] if axis == 1 else key_ref[pl.ds(c0, kc), :]


def _count_keys(key_ref, nck, kc, pred, axis):
    other = key_ref.shape[1 - axis]

    def body(c, acc):
        c0 = pl.multiple_of(c * kc, kc)
        hit = jnp.where(pred(_key_tile(key_ref, c0, kc, axis), c0), 1.0, 0.0)
        if axis == 1:
            part = hit[:, 0:LANES]
            for u in range(1, kc // LANES):
                part = part + hit[:, u * LANES:(u + 1) * LANES]
        else:
            part = jnp.sum(hit.reshape(kc // COUNT_ROWS, COUNT_ROWS, other), axis=0)
        return acc + part

    acc = lax.fori_loop(0, nck, body, jnp.zeros((other, LANES) if axis == 1 else (COUNT_ROWS, other), F32))
    return jnp.sum(acc, axis=axis, keepdims=True)


def _topk_threshold(key_ref, nck, kc, k, idx_bits, axis=1):
    other = key_ref.shape[1 - axis]
    vec = (other, 1) if axis == 1 else (1, other)
    kf = float(k)

    def bit_step(p, t_u):
        cand_u = t_u | jnp.left_shift(jnp.int32(1), 31 - p)
        cand_s = cand_u ^ INT_MIN
        cnt = _count_keys(key_ref, nck, kc, lambda tile, c0: tile >= cand_s, axis)
        return jnp.where(cnt >= kf, cand_u, t_u)

    t_u = lax.fori_loop(0, 32, bit_step, jnp.zeros(vec, jnp.int32))
    thr = jnp.maximum(t_u ^ INT_MIN, INT_MIN + 1)
    n_ge = _count_keys(key_ref, nck, kc, lambda tile, c0: tile >= thr, axis)

    @pl.when(jnp.max(n_ge) > kf)
    def _():
        need = kf - _count_keys(key_ref, nck, kc, lambda tile, c0: tile > thr, axis)
        iota = lax.broadcasted_iota(jnp.int32, (other, kc) if axis == 1 else (kc, other), axis)

        def idx_step(p, j_hi):
            cand = j_hi | jnp.left_shift(jnp.int32(1), idx_bits - 1 - p)
            cnt = _count_keys(key_ref, nck, kc, lambda tile, c0: (tile == thr) & (c0 + iota < cand), axis)
            return jnp.where(cnt <= need, cand, j_hi)

        j_hi = lax.fori_loop(0, idx_bits, idx_step, jnp.zeros(vec, jnp.int32))
        surplus = n_ge > kf

        def lower(c, carry):
            c0 = pl.multiple_of(c * kc, kc)
            tile = _key_tile(key_ref, c0, kc, axis)
            drop = (tile == thr) & (c0 + iota >= j_hi) & surplus
            tile = jnp.where(drop, thr - 1, tile)
            if axis == 1:
                key_ref[:, pl.ds(c0, kc)] = tile
            else:
                key_ref[pl.ds(c0, kc), :] = tile
            return carry

        lax.fori_loop(0, nck, lower, 0)

    return thr


def _stack_heads(q_ref, qs_ref, tq):
    for hd in range(N_HEADS):
        qs_ref[hd * tq:(hd + 1) * tq, :] = (q_ref[0, :, hd * HEAD_DIM:(hd + 1) * HEAD_DIM] * ATTN_SCALE).astype(BF16)


def _flash_init(m_ref, l_ref, acc_ref):
    m_ref[...] = jnp.full(m_ref.shape, NEG_INF, F32)
    l_ref[...] = jnp.zeros(l_ref.shape, F32)
    acc_ref[...] = jnp.zeros(acc_ref.shape, F32)


def _flash_result(rows, l_ref, acc_ref):
    return acc_ref[rows, :] / jnp.maximum(l_ref[rows, :HEAD_DIM], 1e-30)


def _grouped_flash_step(mask_of, kch, vch, qs_ref, m_ref, l_ref, acc_ref, tq):
    rows_h = GROUP * tq
    kc = kch.shape[0]
    parts = []
    for h in range(N_KV_HEADS):
        logits = lax.dot_general(qs_ref[h * rows_h:(h + 1) * rows_h, :], kch[:, h * HEAD_DIM:(h + 1) * HEAD_DIM],
                                 _NT, preferred_element_type=F32)
        parts.append(jnp.where(mask_of(h)[None], logits.reshape(GROUP, tq, kc), NEG_INF).reshape(rows_h, kc))
    s = jnp.concatenate(parts, axis=0)
    m_prev = m_ref[...]
    m_new = jnp.maximum(m_prev, jnp.max(s, axis=1, keepdims=True))
    m_safe = jnp.where(m_new == NEG_INF, 0.0, m_new)
    p = jnp.exp(s - jnp.concatenate([m_safe] * (kc // LANES), axis=1))
    alpha = jnp.exp(m_prev - m_safe)
    l_ref[...] = alpha * l_ref[...] + jnp.sum(p, axis=1, keepdims=True)
    pb = p.astype(BF16)
    pv = [jnp.dot(pb[h * rows_h:(h + 1) * rows_h], vch[:, h * HEAD_DIM:(h + 1) * HEAD_DIM],
                  preferred_element_type=F32) for h in range(N_KV_HEADS)]
    acc_ref[...] = alpha[:, :HEAD_DIM] * acc_ref[...] + jnp.concatenate(pv, axis=0)
    m_ref[...] = m_new


def _index_scores(qi, wi, kic):
    sc = None
    for hh in range(IDX_HEADS):
        s = lax.dot_general(qi[:, hh * IDX_DIM:(hh + 1) * IDX_DIM], kic, _NT, preferred_element_type=F32)
        term = jnp.maximum(s, 0.0) * wi[:, hh:hh + 1]
        sc = term if sc is None else sc + term
    return sc


def _dsa_prompt_kernel(q_ref, qi_ref, wi_ref, k_ref, vt_ref, ki_ref, o_ref, key_ref, *fl, kc, topk, idx_bits):
    qs_ref = fl[0]
    tq = q_ref.shape[1]
    t0 = pl.program_id(1) * tq
    nck = (t0 + tq + kc - 1) // kc
    _stack_heads(q_ref, qs_ref, tq)
    qpos = t0 + lax.broadcasted_iota(jnp.int32, (1, tq), 1)
    kiota = lax.broadcasted_iota(jnp.int32, (kc, 1), 0)
    qi = qi_ref[0]
    wi = wi_ref[0] * IDX_SCALE

    def score(c, carry):
        c0 = pl.multiple_of(c * kc, kc)
        sc = _index_scores(qi, wi, ki_ref[0, pl.ds(c0, kc), :])
        key_ref[pl.ds(c0, kc), :] = jnp.where(c0 + kiota <= qpos, _to_key(sc.T), INT_MIN)
        return carry

    lax.fori_loop(0, nck, score, 0)
    thr = _topk_threshold(key_ref, nck, kc, topk, idx_bits, axis=0)
    _flash_init_t(*fl)

    def chunk(c):
        c0 = pl.multiple_of(jnp.minimum(c, nck - 1) * kc, kc)
        bias = jnp.where(key_ref[pl.ds(c0, kc), :] >= thr, 0.0, NEG_INF)
        return bias, k_ref[0, pl.ds(c0, kc), :], vt_ref[0, :, pl.ds(c0, kc)]

    def first_logits(c):
        bias, kch, _ = chunk(c)
        return _flash_logits(0, bias, kch, qs_ref, fl[3], tq)

    def attend(c, m_first):
        bias, kch, vtch = chunk(c)
        return _flash_step_t(lambda h: bias, kch, vtch, *fl, tq=tq, first=m_first,
                             prefetch=lambda: first_logits(c + 1))

    lax.fori_loop(0, nck, attend, first_logits(0))
    for h in range(N_KV_HEADS):
        res = _flash_result_t(h, *fl)
        for g in range(GROUP):
            hd = h * GROUP + g
            o_ref[0, :, hd * HEAD_DIM:(hd + 1) * HEAD_DIM] = res[g * tq:(g + 1) * tq].astype(o_ref.dtype)


ONES_ROWS = 16
FLASH_MAX_ROWS = 16


def _flash_scratch_t(tq, kmax):
    cols = GROUP * tq
    return [pltpu.VMEM((N_HEADS * tq, HEAD_DIM), BF16),
            pltpu.VMEM((N_KV_HEADS, 1, cols), F32),
            pltpu.VMEM((N_KV_HEADS, HEAD_DIM + ONES_ROWS, cols), F32),
            pltpu.VMEM((N_KV_HEADS, kmax, cols), F32),
            pltpu.VMEM((N_KV_HEADS, kmax, cols), BF16)]


def _flash_init_t(qs_ref, m_ref, acc_ref, s_ref, p_ref):
    m_ref[...] = jnp.full(m_ref.shape, NEG_INF, F32)
    acc_ref[...] = jnp.zeros(acc_ref.shape, F32)


def _flash_logits(h, bias, kch, qs_ref, s_ref, tq):
    cols_h = GROUP * tq
    kc = kch.shape[0]
    st = lax.dot_general(kch[:, h * HEAD_DIM:(h + 1) * HEAD_DIM], qs_ref[h * cols_h:(h + 1) * cols_h, :], _NT,
                         preferred_element_type=F32)
    s_ref[h, 0:kc, :] = st + jnp.concatenate([bias] * GROUP, axis=1)
    mx = s_ref[h, 0:FLASH_MAX_ROWS, :]
    for r in range(FLASH_MAX_ROWS, kc, FLASH_MAX_ROWS):
        mx = jnp.maximum(mx, s_ref[h, r:r + FLASH_MAX_ROWS, :])
    return jnp.max(mx, axis=0, keepdims=True)


def _flash_step_t(bias_of, kch, vtch, qs_ref, m_ref, acc_ref, s_ref, p_ref, *, tq, first=None, prefetch=None):
    kc = kch.shape[0]
    ones = jnp.ones((ONES_ROWS, kc), BF16)
    hcols = lambda h: slice(h * HEAD_DIM, (h + 1) * HEAD_DIM)
    logits = lambda h: _flash_logits(h, bias_of(h), kch, qs_ref, s_ref, tq)
    m_cur = logits(0) if first is None else first
    m_out = None
    for h in range(N_KV_HEADS):
        if h + 1 < N_KV_HEADS:
            m_next = logits(h + 1)
        else:
            m_next = None
            m_out = prefetch() if prefetch is not None else None
        m_prev = m_ref[h]
        m_new = jnp.maximum(m_prev, m_cur)
        m_safe = jnp.where(m_new == NEG_INF, 0.0, m_new)
        p_ref[h, 0:kc, :] = jnp.exp(s_ref[h, 0:kc, :] - m_safe).astype(BF16)
        va = jnp.concatenate([vtch[hcols(h), :], ones], axis=0)
        acc_ref[h] = jnp.exp(m_prev - m_safe) * acc_ref[h] + jnp.dot(va, p_ref[h, 0:kc, :],
                                                                     preferred_element_type=F32)
        m_ref[h] = m_new
        m_cur = m_next
    return m_out


def _flash_result_t(h, qs_ref, m_ref, acc_ref, s_ref, p_ref):
    acc = acc_ref[h]
    return (acc[0:HEAD_DIM] / jnp.maximum(acc[HEAD_DIM:HEAD_DIM + 1], 1e-30)).T


def _flash_scratch(tq):
    return [pltpu.VMEM((N_HEADS * tq, HEAD_DIM), BF16),
            pltpu.VMEM((N_HEADS * tq, LANES), F32),
            pltpu.VMEM((N_HEADS * tq, LANES), F32),
            pltpu.VMEM((N_HEADS * tq, HEAD_DIM), F32)]


def dsa_prompt_attend(q, qi, wi, k, vt, ki, tq=Q_BLOCK, kc=512):
    B, S, D = q.shape
    kc = min(kc, S)
    topk = min(DSA_TOPK, S // 4)
    blk = lambda w: pl.BlockSpec((1, tq, w), lambda b, i: (b, i, 0))
    full = lambda w: pl.BlockSpec((1, S, w), lambda b, i: (b, 0, 0))
    return pl.pallas_call(
        functools.partial(_dsa_prompt_kernel, kc=kc, topk=topk, idx_bits=S.bit_length()),
        out_shape=jax.ShapeDtypeStruct((B, S, D), BF16),
        grid=(B, S // tq),
        in_specs=[blk(D), blk(qi.shape[-1]), blk(wi.shape[-1]), full(KV_WIDTH),
                  pl.BlockSpec((1, KV_WIDTH, S), lambda b, i: (b, 0, 0)), full(IDX_DIM)],
        out_specs=blk(D),
        scratch_shapes=[pltpu.VMEM((S, tq), jnp.int32)] + _flash_scratch_t(tq, kc),
        compiler_params=_cparams("parallel", "arbitrary"),
        name="dsa_prompt_attend",
    )(q, qi, wi, k, vt, ki)


def _dsa_sample_kernel(pt_ref, q_ref, qi_ref, wi_ref, kvn_ref, kin_ref, *refs, pg, kc, topk, idx_bits, n_new):
    ckv_refs, cki_refs = refs[:pg], refs[pg:2 * pg]
    o_ref, key_ref, kv_ref, qs_ref, m_ref, l_ref, acc_ref = refs[2 * pg:]
    tq = q_ref.shape[1]
    j = pl.program_id(1)
    n_steps = pl.num_programs(1)
    past = kv_ref.shape[0] - PAGE_SIZE
    qi = qi_ref[0].astype(BF16)
    wi = wi_ref[0] * IDX_SCALE
    span = pg * PAGE_SIZE
    s0 = pl.multiple_of(j * span, span)
    for k in range(pg):
        kv_ref[pl.ds(s0 + k * PAGE_SIZE, PAGE_SIZE), :] = ckv_refs[k][0, 0].astype(BF16)
    ki_step = jnp.concatenate([r[0, 0] for r in cki_refs], axis=0).astype(BF16)
    key_ref[:, pl.ds(s0, span)] = _to_key(_index_scores(qi, wi, ki_step))

    @pl.when(j == n_steps - 1)
    def _():
        nck = (past + PAGE_SIZE) // kc
        kv_ref[past:past + PAGE_SIZE, :] = kvn_ref[0]
        trow = lax.broadcasted_iota(jnp.int32, (tq, PAGE_SIZE), 0)
        ncol = lax.broadcasted_iota(jnp.int32, (tq, PAGE_SIZE), 1)
        visible = (ncol <= trow) & (ncol < n_new)
        key_ref[:, past:past + PAGE_SIZE] = jnp.where(visible, _to_key(_index_scores(qi, wi, kin_ref[0])), INT_MIN)
        _stack_heads(q_ref, qs_ref, tq)
        thr = _topk_threshold(key_ref, nck, kc, topk, idx_bits)
        _flash_init(m_ref, l_ref, acc_ref)

        def attend(c, carry):
            c0 = pl.multiple_of(c * kc, kc)
            sel = key_ref[:, pl.ds(c0, kc)] >= thr
            _grouped_flash_step(lambda h: sel, kv_ref[pl.ds(c0, kc), 0:KV_WIDTH],
                                kv_ref[pl.ds(c0, kc), KV_WIDTH:2 * KV_WIDTH], qs_ref, m_ref, l_ref, acc_ref, tq)
            return carry

        lax.fori_loop(0, nck, attend, 0)
        for hd in range(N_HEADS):
            o_ref[0, :, hd * HEAD_DIM:(hd + 1) * HEAD_DIM] = _flash_result(
                slice(hd * tq, (hd + 1) * tq), l_ref, acc_ref).astype(o_ref.dtype)


PAGES_PER_STEP = 16


def _pages_per_step(n_pages):
    return max(d for d in range(1, PAGES_PER_STEP + 1) if n_pages % d == 0)


def _page_specs(width, col, layer, pg):
    return [pl.BlockSpec((1, 1, PAGE_SIZE, width), lambda b, j, pt, k=k: (layer, pt[b, j * pg + k], 0, col))
            for k in range(pg)]


def _key_chunk(total):
    n = total // LANES
    return LANES * max(d for d in range(1, 9) if n % d == 0)


def dsa_sample_attend(q, qi, wi, kv_new, ki_new, cache_kv, cache_ki, layer, page_table, n_new):
    DB, tq, D = q.shape
    n_pages = page_table.shape[1]
    past = n_pages * PAGE_SIZE
    total = past + PAGE_SIZE
    kc = _key_chunk(total)
    topk = min(DSA_TOPK, (past + n_new) // 4)
    per_b = lambda r, w: pl.BlockSpec((1, r, w), lambda b, j, pt: (b, 0, 0))
    pg = _pages_per_step(n_pages)
    grid_spec = pltpu.PrefetchScalarGridSpec(
        num_scalar_prefetch=1,
        grid=(DB, n_pages // pg),
        in_specs=[per_b(tq, D), per_b(tq, qi.shape[-1]), per_b(tq, wi.shape[-1]),
                  per_b(PAGE_SIZE, 2 * KV_WIDTH), per_b(PAGE_SIZE, IDX_DIM)]
        + _page_specs(2 * KV_WIDTH, 0, layer, pg) + _page_specs(IDX_DIM, 0, layer, pg),
        out_specs=per_b(tq, D),
        scratch_shapes=[pltpu.VMEM((tq, total), jnp.int32),
                        pltpu.VMEM((total, 2 * KV_WIDTH), BF16)] + _flash_scratch(tq),
    )
    return pl.pallas_call(
        functools.partial(_dsa_sample_kernel, pg=pg, kc=kc, topk=topk, idx_bits=total.bit_length(), n_new=n_new),
        out_shape=jax.ShapeDtypeStruct((DB, tq, D), BF16),
        grid_spec=grid_spec,
        compiler_params=_cparams("parallel", "arbitrary"),
        name="dsa_sample_attend",
    )(page_table, q, qi, wi, kv_new, ki_new, *([cache_kv] * pg), *([cache_ki] * pg))


def _nsa_compress_kernel(pt_ref, *refs, pg):
    x_refs = refs[:pg]
    pe_ref, w1_ref, w2_ref, o_ref, rows_ref = refs[pg:]
    j = pl.program_id(2)
    pair = 2 * CMP_LEN
    per_step = pg * PAGE_SIZE // pair
    x = jnp.concatenate([r[0, 0] for r in x_refs], axis=0) if pg > 1 else x_refs[0][0, 0]
    rows_ref[:, pl.ds(pl.multiple_of(j * per_step, per_step), per_step), :] = jnp.swapaxes(
        x.reshape(per_step, pair, x.shape[-1]), 0, 1)

    @pl.when(j == pl.num_programs(2) - 1)
    def _():
        acc = None
        for l in range(CMP_LEN):
            xl = (jnp.concatenate([rows_ref[l], rows_ref[CMP_LEN + l]], axis=0) + pe_ref[0, l:l + 1, :]).astype(BF16)
            part = jnp.dot(xl, w1_ref[0, l], preferred_element_type=F32)
            acc = part if acc is None else acc + part
        hid = _gelu(acc).astype(BF16)
        o_ref[0] = jnp.dot(hid, w2_ref[0], preferred_element_type=F32).astype(o_ref.dtype)


def nsa_compress(rows, layer, page_table, pe_t, w1_bd, w2_bd):
    NB, n_pages = page_table.shape
    nc = n_pages * PAGE_SIZE // CMP_LEN
    pg = _pages_per_step(n_pages)
    grid_spec = pltpu.PrefetchScalarGridSpec(
        num_scalar_prefetch=1,
        grid=(NB, 2, n_pages // pg),
        in_specs=[pl.BlockSpec((1, 1, PAGE_SIZE, KV_WIDTH), lambda b, s, j, pt, k=k: (layer, pt[b, j * pg + k], 0, s))
                  for k in range(pg)]
        + [pl.BlockSpec((1, CMP_LEN, KV_WIDTH), lambda b, s, j, pt: (s, 0, 0)),
           pl.BlockSpec((1, CMP_LEN, KV_WIDTH, N_KV_HEADS * CMP_HIDDEN), lambda b, s, j, pt: (s, 0, 0, 0)),
           pl.BlockSpec((1, N_KV_HEADS * CMP_HIDDEN, KV_WIDTH), lambda b, s, j, pt: (s, 0, 0))],
        out_specs=pl.BlockSpec((1, nc, KV_WIDTH), lambda b, s, j, pt: (b, 0, s)),
        scratch_shapes=[pltpu.VMEM((2 * CMP_LEN, nc // 2, KV_WIDTH), F32)],
    )
    return pl.pallas_call(
        functools.partial(_nsa_compress_kernel, pg=pg),
        out_shape=jax.ShapeDtypeStruct((NB, nc, 2 * KV_WIDTH), BF16),
        grid_spec=grid_spec,
        compiler_params=_cparams("parallel", "arbitrary", "arbitrary"),
        name="nsa_compress",
    )(page_table, *([rows] * pg), pe_t, w1_bd, w2_bd)


def nsa_compress_weights(cmp_pe, cmp_w1, cmp_w2):
    eye = jnp.eye(N_KV_HEADS, dtype=F32)
    w1 = cmp_w1.reshape(2, CMP_LEN, HEAD_DIM, CMP_HIDDEN)
    w1_bd = jnp.einsum('hg,kldj->klhdgj', eye, w1).reshape(2, CMP_LEN, KV_WIDTH, N_KV_HEADS * CMP_HIDDEN)
    w2_bd = jnp.einsum('hg,kjd->khjgd', eye, cmp_w2).reshape(2, N_KV_HEADS * CMP_HIDDEN, KV_WIDTH)
    pe_t = jnp.tile(cmp_pe, (1, 1, N_KV_HEADS))
    return pe_t, w1_bd.astype(BF16), w2_bd.astype(BF16)


def _nsa_compressed_branch(qs_ref, kvc, qpos, gate_of, out_ref, tq):
    nc = kvc.shape[0]
    half = nc // 2
    pcol = lax.broadcasted_iota(jnp.int32, (1, nc), 1)
    cidx = jnp.where(pcol < half, 2 * pcol, 2 * (pcol - half) + 1)
    visible = (cidx + 1) * CMP_LEN - 1 <= qpos
    imps = []
    for h in range(N_KV_HEADS):
        cols = slice(h * HEAD_DIM, (h + 1) * HEAD_DIM)
        logits = lax.dot_general(qs_ref[h * GROUP * tq:(h + 1) * GROUP * tq, :], kvc[:, cols], _NT,
                                 preferred_element_type=F32)
        vcols = slice(KV_WIDTH + h * HEAD_DIM, KV_WIDTH + (h + 1) * HEAD_DIM)
        imp = None
        for g in range(GROUP):
            hd = h * GROUP + g
            s = jnp.where(visible, logits[g * tq:(g + 1) * tq], NEG_INF)
            m = jnp.max(s, axis=1, keepdims=True)
            e = jnp.exp(s - jnp.where(m == NEG_INF, 0.0, m))
            pc = e / jnp.maximum(jnp.sum(e, axis=1, keepdims=True), 1e-30)
            o = jnp.dot(pc.astype(BF16), kvc[:, vcols], preferred_element_type=F32)
            out_ref[hd * tq:(hd + 1) * tq, :] = gate_of(hd, 0) * o
            pair = pc[:, :half] + pc[:, half:]
            imp = pair if imp is None else imp + pair
        imps.append(imp)
    return imps


def _select_blocks(imp, cur, n_blocks, n_pick, axis):
    blk = lax.broadcasted_iota(jnp.int32, imp.shape, axis)
    forced = (blk == 0) | (blk == cur) | (blk == cur - 1)
    x = jnp.where(forced, 16.0, imp)
    x = jnp.where(blk <= cur, x, -1.0)
    x = jnp.where(blk < n_blocks, x, -2.0)
    blk_f = blk.astype(F32)
    sel = jnp.zeros(imp.shape, F32)
    for _ in range(n_pick):
        m = jnp.max(x, axis=axis, keepdims=True)
        first = jnp.min(jnp.where(x == m, blk_f, float(imp.shape[axis])), axis=axis, keepdims=True)
        hit = blk_f == first
        sel = jnp.where(hit, 1.0, sel)
        x = jnp.where(hit, -2.0, x)
    return sel


def _nsa_finish_branch(br, gate_of, out_ref, l_ref, acc_ref, tq):
    for hd in range(N_HEADS):
        rows = slice(hd * tq, (hd + 1) * tq)
        out_ref[rows, :] = out_ref[rows, :] + gate_of(hd, br) * _flash_result(rows, l_ref, acc_ref)


def _nsa_selected_branch(sel_ref, kv_chunk, nck, kc, qpos, qs_ref, m_ref, l_ref, acc_ref, tq):
    ns_pad = sel_ref.shape[2]
    _flash_init(m_ref, l_ref, acc_ref)
    brow = lax.broadcasted_iota(jnp.int32, (ns_pad, kc), 0)
    kcol = lax.broadcasted_iota(jnp.int32, (ns_pad, kc), 1)
    kiota = lax.broadcasted_iota(jnp.int32, (1, kc), 1)

    def attend(c, carry):
        c0 = pl.multiple_of(c * kc, kc)
        expand = jnp.where((c0 + kcol) // SEL_BLOCK == brow, 1.0, 0.0).astype(BF16)
        causal = c0 + kiota <= qpos
        kch, vch = kv_chunk(c0)

        def mask_of(h):
            picked = jnp.dot(sel_ref[h], expand, preferred_element_type=F32)
            return jnp.where(causal, picked, 0.0) > 0.5

        _grouped_flash_step(mask_of, kch, vch, qs_ref, m_ref, l_ref, acc_ref, tq)
        return carry

    lax.fori_loop(0, nck, attend, 0)


def _nsa_window_branch(kw, vw, kwpos, qpos, qs_ref, m_ref, l_ref, acc_ref, tq):
    _flash_init(m_ref, l_ref, acc_ref)
    dist = qpos - kwpos
    wmask = (dist >= 0) & (dist < WINDOW) & (kwpos >= 0)
    _grouped_flash_step(lambda h: wmask, kw, vw, qs_ref, m_ref, l_ref, acc_ref, tq)


def _gate_fn(gates_ref):
    sig = jax.nn.sigmoid(gates_ref[0])
    return lambda hd, br: sig[:, hd * 3 + br:hd * 3 + br + 1]


def _nsa_write_out(o_ref, out_ref, tq):
    for hd in range(N_HEADS):
        o_ref[0, :, hd * HEAD_DIM:(hd + 1) * HEAD_DIM] = out_ref[hd * tq:(hd + 1) * tq, :].astype(o_ref.dtype)


def _nsa_finish_branch_t(br, gate_of, out_ref, fl, tq):
    for h in range(N_KV_HEADS):
        res = _flash_result_t(h, *fl)
        for g in range(GROUP):
            hd = h * GROUP + g
            rows = slice(hd * tq, (hd + 1) * tq)
            out_ref[rows, :] = out_ref[rows, :] + gate_of(hd, br) * res[g * tq:(g + 1) * tq]


def _nsa_prompt_kernel(q_ref, gates_ref, kvc_ref, kk_ref, vt_ref, o_ref, selt_ref, out_ref, *fl, kc, wlen):
    qs_ref = fl[0]
    tq = q_ref.shape[1]
    S = kk_ref.shape[1]
    t0 = pl.program_id(1) * tq
    _stack_heads(q_ref, qs_ref, tq)
    qpos = t0 + lax.broadcasted_iota(jnp.int32, (tq, 1), 0)
    qpos_row = t0 + lax.broadcasted_iota(jnp.int32, (1, tq), 1)
    gate_of = _gate_fn(gates_ref)
    imps = _nsa_compressed_branch(qs_ref, kvc_ref[0], qpos, gate_of, out_ref, tq)
    n_blocks = S // SEL_BLOCK
    for h in range(N_KV_HEADS):
        selt_ref[h] = _select_blocks(imps[h].T, qpos_row // SEL_BLOCK, n_blocks, min(N_SEL, n_blocks), 0).astype(BF16)
    nck = (t0 + tq + kc - 1) // kc
    _flash_init_t(*fl)
    krow = lax.broadcasted_iota(jnp.int32, (kc, n_blocks), 0)
    bcol = lax.broadcasted_iota(jnp.int32, (kc, n_blocks), 1)
    kiota = lax.broadcasted_iota(jnp.int32, (kc, 1), 0)

    def chunk(c):
        c0 = pl.multiple_of(jnp.minimum(c, nck - 1) * kc, kc)
        expand = jnp.where((c0 + krow) // SEL_BLOCK == bcol, 1.0, 0.0).astype(BF16)
        causal = c0 + kiota <= qpos_row

        def bias_of(h):
            picked = jnp.dot(expand, selt_ref[h], preferred_element_type=F32)
            return jnp.where(causal, jnp.where(picked > 0.5, 0.0, NEG_INF), NEG_INF)

        return c0, bias_of

    def first_logits(c):
        c0, bias_of = chunk(c)
        return _flash_logits(0, bias_of(0), kk_ref[0, pl.ds(c0, kc), 0:KV_WIDTH], qs_ref, fl[3], tq)

    def attend(c, m_first):
        c0, bias_of = chunk(c)
        return _flash_step_t(bias_of, kk_ref[0, pl.ds(c0, kc), 0:KV_WIDTH], vt_ref[0, 0:KV_WIDTH, pl.ds(c0, kc)],
                             *fl, tq=tq, first=m_first, prefetch=lambda: first_logits(c + 1))

    lax.fori_loop(0, nck, attend, first_logits(0))
    _nsa_finish_branch_t(1, gate_of, out_ref, fl, tq)
    start = pl.multiple_of(jnp.clip(t0 + tq - wlen, 0, S - wlen), tq)
    dist = qpos_row - (start + lax.broadcasted_iota(jnp.int32, (wlen, 1), 0))
    wbias = jnp.where(dist >= 0, jnp.where(dist < WINDOW, 0.0, NEG_INF), NEG_INF)
    _flash_init_t(*fl)
    _flash_step_t(lambda h: wbias, kk_ref[0, pl.ds(start, wlen), KV_WIDTH:2 * KV_WIDTH],
                  vt_ref[0, KV_WIDTH:2 * KV_WIDTH, pl.ds(start, wlen)], *fl, tq=tq)
    _nsa_finish_branch_t(2, gate_of, out_ref, fl, tq)
    _nsa_write_out(o_ref, out_ref, tq)


def nsa_prompt_attend(q, gates, kvc, kk, vt, tq=Q_BLOCK, kc=512):
    B, S, D = q.shape
    kc = min(kc, S)
    wlen = min(WINDOW + tq, S)
    n_blocks = S // SEL_BLOCK
    blk = lambda w: pl.BlockSpec((1, tq, w), lambda b, i: (b, i, 0))
    return pl.pallas_call(
        functools.partial(_nsa_prompt_kernel, kc=kc, wlen=wlen),
        out_shape=jax.ShapeDtypeStruct((B, S, D), BF16),
        grid=(B, S // tq),
        in_specs=[blk(D), blk(gates.shape[-1]),
                  pl.BlockSpec((1,) + kvc.shape[1:], lambda b, i: (b, 0, 0)),
                  pl.BlockSpec((1, S, 2 * KV_WIDTH), lambda b, i: (b, 0, 0), pipeline_mode=pl.Buffered(1)),
                  pl.BlockSpec((1, 2 * KV_WIDTH, S), lambda b, i: (b, 0, 0), pipeline_mode=pl.Buffered(1))],
        out_specs=blk(D),
        scratch_shapes=[pltpu.VMEM((N_KV_HEADS, n_blocks, tq), BF16),
                        pltpu.VMEM((N_HEADS * tq, HEAD_DIM), F32)] + _flash_scratch_t(tq, max(kc, wlen)),
        compiler_params=_cparams("parallel", "arbitrary"),
        name="nsa_prompt_attend",
    )(q, gates, kvc, kk, vt)


def _nsa_sample_kernel(pt_ref, q_ref, gates_ref, kvc_ref, kvn_ref, win_ref, winn_ref, *refs, pg, kc, n_new):
    csel_refs = refs[:pg]
    o_ref, kv_ref, wkv_ref, sel_ref, out_ref, qs_ref, m_ref, l_ref, acc_ref = refs[pg:]
    tq = q_ref.shape[1]
    j = pl.program_id(1)
    past = kv_ref.shape[0] - PAGE_SIZE
    for k in range(pg):
        p0 = pl.multiple_of((j * pg + k) * PAGE_SIZE, PAGE_SIZE)
        kv_ref[pl.ds(p0, PAGE_SIZE), :] = csel_refs[k][0, 0].astype(BF16)

    @pl.when(j == pl.num_programs(1) - 1)
    def _():
        kv_ref[past:past + PAGE_SIZE, :] = kvn_ref[0]
        wb = win_ref.shape[1]
        wkv_ref[0:wb, :] = win_ref[0].astype(BF16)
        wkv_ref[wb:wb + PAGE_SIZE, :] = winn_ref[0]
        _stack_heads(q_ref, qs_ref, tq)
        qpos = past + jnp.minimum(lax.broadcasted_iota(jnp.int32, (tq, 1), 0), n_new - 1)
        gate_of = _gate_fn(gates_ref)
        imps = _nsa_compressed_branch(qs_ref, kvc_ref[0], qpos, gate_of, out_ref, tq)
        n_blocks = -(-(past + n_new) // SEL_BLOCK)
        ns_pad = sel_ref.shape[2]
        for h in range(N_KV_HEADS):
            imp = imps[h]
            imp = jnp.concatenate([imp, jnp.zeros((tq, ns_pad - imp.shape[1]), F32)], axis=1)
            sel_ref[h] = _select_blocks(imp, qpos // SEL_BLOCK, n_blocks, min(N_SEL, n_blocks), 1).astype(BF16)
        nck = (past + PAGE_SIZE) // kc
        kv_chunk = lambda c0: (kv_ref[pl.ds(c0, kc), 0:KV_WIDTH], kv_ref[pl.ds(c0, kc), KV_WIDTH:2 * KV_WIDTH])
        _nsa_selected_branch(sel_ref, kv_chunk, nck, kc, qpos, qs_ref, m_ref, l_ref, acc_ref, tq)
        _nsa_finish_branch(1, gate_of, out_ref, l_ref, acc_ref, tq)
        kwpos = past - wb + lax.broadcasted_iota(jnp.int32, (1, wb + PAGE_SIZE), 1)
        _nsa_window_branch(wkv_ref[:, 0:KV_WIDTH], wkv_ref[:, KV_WIDTH:2 * KV_WIDTH], kwpos, qpos,
                           qs_ref, m_ref, l_ref, acc_ref, tq)
        _nsa_finish_branch(2, gate_of, out_ref, l_ref, acc_ref, tq)
        _nsa_write_out(o_ref, out_ref, tq)


def nsa_sample_attend(q, gates, kvc, sel_new, win_buf, win_new, cache_kv, layer, page_table, n_new):
    DB, tq, D = q.shape
    n_pages = page_table.shape[1]
    past = n_pages * PAGE_SIZE
    total = past + PAGE_SIZE
    kc = _key_chunk(total)
    wb = win_buf.shape[1]
    ns_pad = -(-(total // SEL_BLOCK) // LANES) * LANES
    per_b = lambda r, w: pl.BlockSpec((1, r, w), lambda b, j, pt: (b, 0, 0))
    pg = _pages_per_step(n_pages)
    grid_spec = pltpu.PrefetchScalarGridSpec(
        num_scalar_prefetch=1,
        grid=(DB, n_pages // pg),
        in_specs=[per_b(tq, D), per_b(tq, gates.shape[-1]), per_b(kvc.shape[1], kvc.shape[2]),
                  per_b(PAGE_SIZE, 2 * KV_WIDTH), per_b(wb, 2 * KV_WIDTH), per_b(PAGE_SIZE, 2 * KV_WIDTH)]
        + _page_specs(2 * KV_WIDTH, 1, layer, pg),
        out_specs=per_b(tq, D),
        scratch_shapes=[pltpu.VMEM((total, 2 * KV_WIDTH), BF16),
                        pltpu.VMEM((wb + PAGE_SIZE, 2 * KV_WIDTH), BF16),
                        pltpu.VMEM((N_KV_HEADS, tq, ns_pad), BF16),
                        pltpu.VMEM((N_HEADS * tq, HEAD_DIM), F32)] + _flash_scratch(tq),
    )
    return pl.pallas_call(
        functools.partial(_nsa_sample_kernel, pg=pg, kc=kc, n_new=n_new),
        out_shape=jax.ShapeDtypeStruct((DB, tq, D), BF16),
        grid_spec=grid_spec,
        compiler_params=_cparams("parallel", "arbitrary"),
        name="nsa_sample_attend",
    )(page_table, q, gates, kvc, sel_new, win_buf, win_new, *([cache_kv] * pg))


HALO = 8
G_IG, G_LF, G_CUM = 0, MLSTM_HEADS, 2 * MLSTM_HEADS
MASKED_GATE = -1e30


def _split3(x):
    hi = x.astype(BF16)
    r1 = x - hi.astype(F32)
    mid = r1.astype(BF16)
    lo = (r1 - mid.astype(F32)).astype(BF16)
    return hi, mid, lo


def _mlstm_pre_kernel(xm_ref, halo_ref, cw_ref, cb_ref, wq_ref, wk_ref, wv_ref, wvt_ref, wg_ref, bg_ref,
                      q_ref, k_ref, v_ref, vt_ref, xc_ref, g_ref, gt_ref, xp_ref, *, chunk, n_valid):
    tm = xm_ref.shape[1]
    hd = MLSTM_HEAD_DIM
    xp_ref[0:HALO, :] = halo_ref[0, 0]
    xp_ref[HALO:HALO + tm, :] = xm_ref[0]
    y = cb_ref[...]
    for j in range(CONV_WIDTH):
        off = HALO - (CONV_WIDTH - 1) + j
        y = y + cw_ref[j:j + 1, :] * xp_ref[off:off + tm, :]
    xc = y * jax.nn.sigmoid(y)
    xc_ref[0] = xc
    xcb = xc.astype(BF16)
    xmb = xm_ref[0].astype(BF16)
    g = bg_ref[...]
    for h in range(MLSTM_HEADS):
        cols = slice(h * hd, (h + 1) * hd)
        q = jnp.dot(xcb[:, cols], wq_ref[h], preferred_element_type=F32).astype(BF16)
        k = (jnp.dot(xcb[:, cols], wk_ref[h], preferred_element_type=F32) * hd ** -0.5).astype(BF16)
        v = jnp.dot(xmb[:, cols], wv_ref[h], preferred_element_type=F32).astype(BF16)
        q_ref[0, :, cols] = q
        k_ref[0, :, cols] = k
        v_ref[0, :, cols] = v
        vt_ref[0, cols, :] = lax.dot_general(wvt_ref[h], xmb[:, cols], _NT, preferred_element_type=F32).astype(BF16)
        for i, a in enumerate((q, k, v)):
            g = g + jnp.dot(a, wg_ref[i * MLSTM_INNER + h * hd:i * MLSTM_INNER + (h + 1) * hd, :],
                            preferred_element_type=F32)
    col = lax.broadcasted_iota(jnp.int32, (tm, LANES), 1)
    row = pl.program_id(1) * tm + lax.broadcasted_iota(jnp.int32, (tm, LANES), 0)
    log_f = jnp.minimum(g, 0.0) - jnp.log1p(jnp.exp(-jnp.abs(g)))
    g = jnp.where(col < G_LF, g, log_f)
    live = row < n_valid
    g = jnp.where(live, g, jnp.where(col < G_LF, MASKED_GATE, 0.0))
    tri = (lax.broadcasted_iota(jnp.int32, (chunk, chunk), 0) >= lax.broadcasted_iota(jnp.int32, (chunk, chunk), 1))
    tri = jnp.where(tri, 1.0, 0.0).astype(BF16)
    for c in range(tm // chunk):
        rows = slice(c * chunk, (c + 1) * chunk)
        cum = None
        for piece in _split3(g[rows]):
            part = jnp.dot(tri, piece, preferred_element_type=F32)
            cum = part if cum is None else cum + part
        out = jnp.where(col[rows] < G_CUM, g[rows], pltpu.roll(cum, G_CUM - G_LF, 1))
        g_ref[0, rows, :] = out
        gt_ref[0, :, rows] = out.T[0:gt_ref.shape[1], :]


def mlstm_pre(xm, halo, conv_w, conv_b, wq, wk, wv, wvt, wg, bg, tm, chunk, n_valid):
    NB, S, W = xm.shape
    hd = MLSTM_HEAD_DIM
    const = lambda a: pl.BlockSpec(a.shape, lambda b, i: (0,) * a.ndim)
    row_blk = lambda w: pl.BlockSpec((1, tm, w), lambda b, i: (b, i, 0))
    sds = jax.ShapeDtypeStruct
    return pl.pallas_call(
        functools.partial(_mlstm_pre_kernel, chunk=chunk, n_valid=n_valid),
        out_shape=[sds((NB, S, W), BF16)] * 3 + [sds((NB, W, S), BF16), sds((NB, S, W), F32),
                                                 sds((NB, S, LANES), F32), sds((NB, 2 * HALO, S), F32)],
        grid=(NB, S // tm),
        in_specs=[row_blk(W), pl.BlockSpec((1, 1, HALO, W), lambda b, i: (b, i, 0, 0)),
                  const(conv_w), const(conv_b), const(wq), const(wk), const(wv), const(wvt), const(wg), const(bg)],
        out_specs=[row_blk(W)] * 3 + [pl.BlockSpec((1, W, tm), lambda b, i: (b, 0, i)), row_blk(W), row_blk(LANES),
                                      pl.BlockSpec((1, 2 * HALO, tm), lambda b, i: (b, 0, i))],
        scratch_shapes=[pltpu.VMEM((HALO + tm, W), F32)],
        compiler_params=_cparams("parallel", "parallel"),
        name="mlstm_pre",
    )(xm, halo, conv_w, conv_b, wq, wk, wv, wvt, wg, bg)


def _mlstm_scan_kernel(q_ref, k_ref, v_ref, vt_ref, g_ref, gt_ref, xc_ref, z_ref, ng_ref, sk_ref, c0_ref, n0_ref, m0_ref,
                       o_ref, c_out, n_out, m_out, c_ref, n_ref, m_ref):
    L = q_ref.shape[1]
    h = pl.program_id(1)
    ci = pl.program_id(2)

    @pl.when(ci == 0)
    def _():
        c_ref[...] = c0_ref[0, 0]
        n_ref[...] = n0_ref[0, 0]
        m_ref[...] = m0_ref[0, 0]

    q, k, v = q_ref[0], k_ref[0], v_ref[0]
    col = lax.broadcasted_iota(jnp.int32, (L, LANES), 1)
    b_col = jnp.sum(jnp.where(col == G_CUM + h, g_ref[0], 0.0), axis=1, keepdims=True)
    i_row = gt_ref[0, pl.ds(G_IG + h, 1), :]
    b_row = gt_ref[0, pl.ds(G_CUM + h, 1), :]
    m = m_ref[0:1, 0:1]
    inter = b_col + m
    dmat = b_col - b_row + i_row
    tril = lax.broadcasted_iota(jnp.int32, (L, L), 0) >= lax.broadcasted_iota(jnp.int32, (L, L), 1)
    dmat = jnp.where(tril, dmat, NEG_INF)
    m_loc = jnp.maximum(inter, jnp.max(dmat, axis=1, keepdims=True))
    a = lax.dot_general(q, k, _NT, preferred_element_type=F32) * jnp.exp(dmat - m_loc)
    w_inter = jnp.exp(inter - m_loc)
    c_prev = c_ref[...]
    n_prev = n_ref[...]
    num = (jnp.dot(a.astype(BF16), v, preferred_element_type=F32)
           + w_inter * lax.dot_general(q, c_prev.astype(BF16), _NT, preferred_element_type=F32))
    qn = lax.dot_general(q, n_prev.astype(BF16), _NT, preferred_element_type=F32)[:, 0:1]
    den = jnp.sum(a, axis=1, keepdims=True) + w_inter * qn
    hc = num / jnp.maximum(jnp.abs(den), jnp.exp(-m_loc))
    b_end = b_row[:, L - 1:L]
    lg = b_end - b_row + i_row
    m_new = jnp.maximum(b_end + m, jnp.max(lg, axis=1, keepdims=True))
    wg = jnp.exp(lg - m_new)
    decay = jnp.exp(b_end + m - m_new)
    c_ref[...] = decay * c_prev + jnp.dot((vt_ref[0] * wg).astype(BF16), k, preferred_element_type=F32)
    n_ref[...] = decay * n_prev + jnp.dot(jnp.broadcast_to(wg, (n_ref.shape[0], L)).astype(BF16), k,
                                          preferred_element_type=F32)
    m_ref[...] = jnp.broadcast_to(m_new, m_ref.shape)
    mu = jnp.mean(hc, axis=1, keepdims=True)
    hz = hc - mu
    var = jnp.mean(hz * hz, axis=1, keepdims=True)
    hn = hz * lax.rsqrt(var + LN_EPS) * ng_ref[...]
    z = z_ref[0]
    o_ref[0] = ((hn + sk_ref[...] * xc_ref[0]) * (z * jax.nn.sigmoid(z))).astype(o_ref.dtype)

    @pl.when(ci == pl.num_programs(2) - 1)
    def _():
        c_out[0, 0] = c_ref[...]
        n_out[0, 0] = n_ref[...]
        m_out[0, 0] = m_ref[...]


def mlstm_scan(q, k, v, vt, g, gt, xc, z, norm_g, skip, c0, n0, m0, chunk):
    NB, S, W = q.shape
    hd = MLSTM_HEAD_DIM
    H = W // hd
    seq = lambda: pl.BlockSpec((1, chunk, hd), lambda b, h, c: (b, c, h))
    vec = pl.BlockSpec((1, hd), lambda b, h, c: (0, h))
    st = lambda r, w: pl.BlockSpec((1, 1, r, w), lambda b, h, c: (b, h, 0, 0))
    sds = jax.ShapeDtypeStruct
    return pl.pallas_call(
        _mlstm_scan_kernel,
        out_shape=[sds((NB, S, W), BF16), sds((NB, H, hd, hd), F32), sds((NB, H, HALO, hd), F32), sds((NB, H, HALO, LANES), F32)],
        grid=(NB, H, S // chunk),
        in_specs=[seq(), seq(), seq(), pl.BlockSpec((1, hd, chunk), lambda b, h, c: (b, h, c)),
                  pl.BlockSpec((1, chunk, LANES), lambda b, h, c: (b, c, 0)),
                  pl.BlockSpec((1, 2 * HALO, chunk), lambda b, h, c: (b, 0, c)),
                  seq(), seq(), vec, vec, st(hd, hd), st(HALO, hd), st(HALO, LANES)],
        out_specs=[seq(), st(hd, hd), st(HALO, hd), st(HALO, LANES)],
        scratch_shapes=[pltpu.VMEM((hd, hd), F32), pltpu.VMEM((HALO, hd), F32), pltpu.VMEM((HALO, LANES), F32)],
        compiler_params=_cparams("parallel", "parallel", "arbitrary"),
        name="mlstm_scan",
    )(q, k, v, vt, g, gt, xc, z, norm_g.reshape(1, W), skip.reshape(1, W), c0, n0, m0)


SAMPLE_Q_ROWS = 16
TM_PROMPT = 512
TM_PEER_ROUTE = 256
TM_PEER = 512
TM_MLSTM = 256
CHUNK_MLSTM = 256


def _pad_rows(a, rows):
    return jnp.pad(a, ((0, 0), (0, rows - a.shape[1]), (0, 0)))


def _pad_cols(a, cols):
    return jnp.pad(a, ((0, 0),) * (a.ndim - 1) + ((0, cols - a.shape[-1]),))


def _dsa_layer(xp, xs, mp, ms, w_in, w_o, ln_g, ln_b, cache_kv, cache_ki, layer, page_table, DB, T):
    B, S, D = xp.shape
    wb = w_in.astype(BF16)
    o1, o2, o3, o4 = D, D + KV_WIDTH, D + 2 * KV_WIDTH, D + 2 * KV_WIDTH + IDX_HEADS * IDX_DIM
    weights = [wb[:, :o1], wb[:, o1:o3], wb[:, o1:o2], wb[:, o2:o3], wb[:, o3:o4], wb[:, o4:o4 + IDX_DIM],
               _pad_cols(wb[:, o4 + IDX_DIM:], LANES)]
    outs = [(0, BF16), (1, F32), (2, BF16), (3, BF16), (4, BF16), (5, F32), (5, BF16), (6, F32)]
    w_ob = w_o.astype(BF16)
    q, kv32, kb, _, qi, ki32, kib, wi, vt = proj(xp, mp(0, 0), mp(0, 1), weights, outs, TM_PROMPT,
                                                 weights_t=[wb[:, o2:o3].T])
    o = dsa_prompt_attend(q, qi, wi, kb, vt, kib)
    xp = outproj_postnorm(o, w_ob, xp, mp(0, 2), ln_g, ln_b, TM_PROMPT)
    kv_p = kv32.reshape(B, S, 2, N_KV_HEADS, HEAD_DIM)
    n = DB * T
    q, kv32s, kb, vb, qi, ki32s, kib, wi = proj(xs, ms(0, 0), ms(0, 1), weights, outs, n)
    per_b = lambda a: a.reshape(DB, T, a.shape[-1])
    kv_new = _pad_rows(jnp.concatenate([per_b(kb), per_b(vb)], axis=-1), PAGE_SIZE)
    o = dsa_sample_attend(_pad_rows(per_b(q), SAMPLE_Q_ROWS), _pad_rows(per_b(qi), SAMPLE_Q_ROWS),
                          _pad_rows(per_b(wi), SAMPLE_Q_ROWS), kv_new, _pad_rows(per_b(kib), PAGE_SIZE),
                          cache_kv.reshape(cache_kv.shape[:3] + (2 * KV_WIDTH,)), cache_ki, layer, page_table, T)
    xs = outproj_postnorm(o[:, :T].reshape(1, n, D), w_ob, xs, ms(0, 2), ln_g, ln_b, n)
    kv_s = kv32s.reshape(DB, T, 2, N_KV_HEADS, HEAD_DIM)
    return xp, xs, kv_p, kv_s, ki32, ki32s.reshape(DB, T, IDX_DIM)


def _nsa_layer(xp, xs, mp, ms, w_in, cmp_pe, cmp_w1, cmp_w2, w_o, ln_g, ln_b, cache_kv, win_state, layer,
               page_table, DB, T):
    B, S, D = xp.shape
    wb = w_in.astype(BF16)
    c1, c2, c3 = D + 4 * KV_WIDTH, D + 6 * KV_WIDTH, D + 2 * KV_WIDTH
    sel_k, sel_v = wb[:, c3:c3 + KV_WIDTH], wb[:, c3 + KV_WIDTH:c1]
    win_k, win_v = wb[:, c1:c1 + KV_WIDTH], wb[:, c1 + KV_WIDTH:c2]
    weights = [wb[:, :D], wb[:, D:c1], wb[:, c1:c2], wb[:, c3:c2], _pad_cols(wb[:, c2:], LANES),
               jnp.concatenate([sel_k, win_k], axis=1)]
    outs = [(0, BF16), (1, F32), (2, F32), (3, BF16), (4, F32)]
    w_ob = w_o.astype(BF16)
    cmp_w = nsa_compress_weights(cmp_pe, cmp_w1, cmp_w2)
    q, cs32, win32, kk, gates, vt = proj(xp, mp(0, 0), mp(0, 1), weights,
                                         [(0, BF16), (1, F32), (2, F32), (5, BF16), (4, F32)], TM_PROMPT,
                                         weights_t=[jnp.concatenate([sel_v, win_v], axis=1).T])
    pages = S // PAGE_SIZE
    ident = jnp.arange(B * pages, dtype=jnp.int32).reshape(B, pages)
    kvc = nsa_compress(cs32.reshape(1, B * pages, PAGE_SIZE, 4 * KV_WIDTH), 0, ident, *cmp_w)
    o = nsa_prompt_attend(q, gates, kvc, kk, vt)
    xp = outproj_postnorm(o, w_ob, xp, mp(0, 2), ln_g, ln_b, TM_PROMPT)
    kv_p = cs32.reshape(B, S, 2, 2, N_KV_HEADS, HEAD_DIM)
    keep = min(WINDOW, S)
    win_p = win32[:, S - keep:].reshape(B, keep, 2, N_KV_HEADS, HEAD_DIM)
    n = DB * T
    q, cs32s, win32s, selwin, gates = proj(xs, ms(0, 0), ms(0, 1), weights, outs, n)
    per_b = lambda a: a.reshape(DB, T, a.shape[-1])
    kvc = nsa_compress(cache_kv.reshape(cache_kv.shape[:3] + (4 * KV_WIDTH,)), layer, page_table, *cmp_w)
    selwin = per_b(selwin)
    wbuf = win_state[layer]
    o = nsa_sample_attend(_pad_rows(per_b(q), SAMPLE_Q_ROWS), _pad_rows(per_b(gates), SAMPLE_Q_ROWS), kvc,
                          _pad_rows(selwin[..., :2 * KV_WIDTH], PAGE_SIZE), wbuf.reshape(DB, wbuf.shape[1], 2 * KV_WIDTH),
                          _pad_rows(selwin[..., 2 * KV_WIDTH:], PAGE_SIZE),
                          cache_kv.reshape(cache_kv.shape[:3] + (4 * KV_WIDTH,)), layer, page_table, T)
    xs = outproj_postnorm(o[:, :T].reshape(1, n, D), w_ob, xs, ms(0, 2), ln_g, ln_b, n)
    kv_s = cs32s.reshape(DB, T, 2, 2, N_KV_HEADS, HEAD_DIM)
    win_s = jnp.concatenate([wbuf, win32s.reshape(DB, T, 2, N_KV_HEADS, HEAD_DIM)], axis=1)[:, T:]
    return xp, xs, kv_p, kv_s, win_p, win_s


def _mlstm_layer(xp, xs, mp, ms, w_in, conv_w, conv_b, w_qkv, w_gate, b_gate, norm_g, skip, w_o, ln_g, ln_b,
                 conv_state, c_state, n_state, m_state, DB, T):
    B, S, D = xp.shape
    W, H, hd = MLSTM_INNER, MLSTM_HEADS, MLSTM_HEAD_DIM
    wb = w_in.astype(BF16)
    weights = [wb[:, :W], wb[:, W:]]
    outs = [(0, F32), (1, F32)]
    wq, wk, wv = (w_qkv[i].astype(BF16) for i in range(3))
    pre_w = (conv_w, conv_b.reshape(1, W), wq, wk, wv, jnp.swapaxes(wv, 1, 2), _pad_cols(w_gate, LANES).astype(BF16),
             _pad_cols(b_gate.reshape(1, -1), LANES))
    w_ob = w_o.astype(BF16)
    keep = CONV_WIDTH - 1
    rep = lambda a, r: jnp.broadcast_to(a[..., None, :], a.shape[:-1] + (r, a.shape[-1]))
    xm, z = proj(xp, mp(0, 0), mp(0, 1), weights, outs, TM_PROMPT)
    tm = min(TM_MLSTM, S)
    tiles = xm.reshape(B, S // tm, tm, W)
    halo = jnp.concatenate([jnp.zeros((B, 1, HALO, W), F32), tiles[:, :-1, tm - HALO:]], axis=1)
    chunk = min(CHUNK_MLSTM, S)
    q, k, v, vt, xc, g, gt = mlstm_pre(xm, halo, *pre_w, tm=tm, chunk=chunk, n_valid=S)
    zeros = lambda *s: jnp.zeros(s, F32)
    o, c_p, n_p, m_p = mlstm_scan(q, k, v, vt, g, gt, xc, z, norm_g, skip, zeros(B, H, hd, hd), zeros(B, H, HALO, hd),
                                  zeros(B, H, HALO, LANES), chunk)
    xp = outproj_postnorm(o, w_ob, xp, mp(0, 2), ln_g, ln_b, TM_PROMPT)
    conv_p = jnp.concatenate([zeros(B, keep, W), xm], axis=1)[:, -keep:]
    n = DB * T
    xm_s, z_s = proj(xs, ms(0, 0), ms(0, 1), weights, outs, n)
    xm_s = xm_s.reshape(DB, T, W)
    rows = LANES
    halo = jnp.concatenate([zeros(DB, HALO - keep, W), conv_state], axis=1)[:, None]
    q, k, v, vt, xc, g, gt = mlstm_pre(_pad_rows(xm_s, rows), halo, *pre_w, tm=rows, chunk=rows, n_valid=T)
    m0 = jnp.broadcast_to(m_state[..., None, None], (DB, H, HALO, LANES))
    o, c_s, n_s, m_s = mlstm_scan(q, k, v, vt, g, gt, xc, _pad_rows(z_s.reshape(DB, T, W), rows), norm_g, skip,
                                  c_state, rep(n_state, HALO), m0, rows)
    xs = outproj_postnorm(o[:, :T].reshape(1, n, W), w_ob, xs, ms(0, 2), ln_g, ln_b, n)
    conv_s = jnp.concatenate([conv_state, xm_s], axis=1)[:, -keep:]
    return (xp, xs, conv_p, conv_s, c_p, c_s, n_p[:, :, 0], n_s[:, :, 0], m_p[:, :, 0, 0], m_s[:, :, 0, 0])


def kernel(x_prompt, x_sample, cache_a_kv, cache_a_kidx, cache_b_kv, state_b_win, state_c_conv, state_c_C, state_c_n,
           state_c_m, page_table, c_prompt, c_sample, a_w_in, a_w_o, b_w_in, b_cmp_pe, b_cmp_w1, b_cmp_w2, b_w_o,
           c_w_in, c_conv_w, c_conv_b, c_w_qkv, c_w_gate, c_b_gate, c_norm_g, c_skip, c_w_o,
           ada_w, ada_b, ln_g, ln_b, peer_w_q, peer_sub_keys, peer_u, peer_v):
    B, S, D = x_prompt.shape
    DB, T, _ = x_sample.shape
    n = DB * T
    n_cond = B + DB
    cond = _pad_rows(jnp.concatenate([c_prompt, c_sample], axis=0)[None], -(-n_cond // 8) * 8)[0]
    mods = adaln_all(cond, ada_w, ada_b).reshape(DEPTH, cond.shape[0], 2, 3, D)
    xp, xs = x_prompt, x_sample.reshape(1, n, D)
    outs = {name: [] for name in ("a_kv_p", "a_kv_s", "a_ki_p", "a_ki_s", "b_kv_p", "b_kv_s", "b_win_p", "b_win_s",
                                  "conv_p", "conv_s", "C_p", "C_s", "n_p", "n_s", "m_p", "m_s")}
    for i in range(DEPTH):
        kind, j = i % N_MIXERS, i // N_MIXERS
        mod_p = mods[i, :B]
        mod_s = jnp.repeat(mods[i, B:n_cond], T, axis=0)
        mp = lambda s, r, mod_p=mod_p: mod_p[:, s, r][:, None, :]
        ms = lambda s, r, mod_s=mod_s: mod_s[:, s, r][None]
        if kind == 0:
            xp, xs, kvp, kvs, kip, kis = _dsa_layer(xp, xs, mp, ms, a_w_in[j], a_w_o[j], ln_g[i, 0], ln_b[i, 0],
                                                    cache_a_kv, cache_a_kidx, j, page_table, DB, T)
            for name, val in zip(("a_kv_p", "a_kv_s", "a_ki_p", "a_ki_s"), (kvp, kvs, kip, kis)):
                outs[name].append(val)
        elif kind == 1:
            xp, xs, kvp, kvs, wp, ws = _nsa_layer(xp, xs, mp, ms, b_w_in[j], b_cmp_pe[j], b_cmp_w1[j], b_cmp_w2[j],
                                                  b_w_o[j], ln_g[i, 0], ln_b[i, 0], cache_b_kv, state_b_win, j,
                                                  page_table, DB, T)
            for name, val in zip(("b_kv_p", "b_kv_s", "b_win_p", "b_win_s"), (kvp, kvs, wp, ws)):
                outs[name].append(val)
        else:
            res = _mlstm_layer(xp, xs, mp, ms, c_w_in[j], c_conv_w[j], c_conv_b[j], c_w_qkv[j], c_w_gate[j], c_b_gate[j],
                               c_norm_g[j], c_skip[j], c_w_o[j], ln_g[i, 0], ln_b[i, 0],
                               state_c_conv[j], state_c_C[j], state_c_n[j], state_c_m[j], DB, T)
            xp, xs = res[0], res[1]
            for name, val in zip(("conv_p", "conv_s", "C_p", "C_s", "n_p", "n_s", "m_p", "m_s"), res[2:]):
                outs[name].append(val)
        peer_w = (peer_w_q[i].astype(BF16), peer_sub_keys[i].astype(BF16), peer_u[i].astype(BF16).T,
                  peer_v[i].astype(BF16), ln_g[i, 1], ln_b[i, 1])
        xp = peer_layer(xp, mp(1, 0), mp(1, 1), mp(1, 2), *peer_w, tm_route=min(TM_PEER_ROUTE, S), tm=min(TM_PEER, S))
        xs = peer_layer(xs, ms(1, 0), ms(1, 1), ms(1, 2), *peer_w, tm_route=n, tm=n)
    return (xp, xs.reshape(DB, T, D)) + tuple(jnp.stack(outs[name]) for name in outs)
```
